```python
import math
import jax, jax.numpy as jnp
from jax import lax
import numpy as np

D_MODEL = 1024
BATCH = 4
SEQ = 4096
DEPTH = 1
DEC_BATCH = 128
DEC_SEQ = 4
PAST_LEN = 2048
PAGE_SIZE = 128

N_HEADS = 8
HEAD_DIM = 64
D_ATTN = N_HEADS * HEAD_DIM
D_RNN = 512
N_RG_BLOCKS = 8
RG_BLOCK = D_RNN // N_RG_BLOCKS
CONV_W = 4
RG_C = 8.0
MOBA_BLOCK = 256
MOBA_TOP_K = 3
NUM_BUCKETS = 32
MAX_DISTANCE = 128
D_FF = -(-8 * D_MODEL // (3 * 256)) * 256
D_IN = 3 * D_ATTN + 2 * D_RNN + 2 * D_MODEL
Q_CHUNK = 64
RMS_EPS = 1e-6
NEG_INF = -1e30

kernel_name = "hybrid_moba_rglru_decode_step"


def rmsnorm(x, g):
    x32 = x.astype(jnp.float32)
    y = x32 * lax.rsqrt(jnp.mean(x32 * x32, axis=-1, keepdims=True) + RMS_EPS)
    return (y * g.astype(jnp.float32)).astype(x.dtype)


def t5_bucket(rel):
    n = jnp.maximum(-rel, 0)
    max_exact = NUM_BUCKETS // 2
    nf = jnp.maximum(n, 1).astype(jnp.float32)
    large = max_exact + (jnp.log(nf / max_exact) / math.log(MAX_DISTANCE / max_exact)
                         * (NUM_BUCKETS - max_exact)).astype(jnp.int32)
    large = jnp.minimum(large, NUM_BUCKETS - 1)
    return jnp.where(n < max_exact, n, large)


def moba_attend(q, q_pos, kb, vb, means, rel_bias):
    B, H, Q, dh = q.shape
    NB = kb.shape[2]
    k_sel = max(1, min(MOBA_TOP_K, NB - 1))
    own = q_pos // MOBA_BLOCK
    blk_scores = jnp.einsum('bhqd,bhnd->bhqn', q.astype(jnp.float32), means)
    past = jnp.arange(NB)[None, :] < own[:, None]
    blk_scores = jnp.where(past, blk_scores, NEG_INF)
    _, sel = lax.top_k(blk_scores, k_sel)
    sel_valid = sel < own[:, None]
    own_b = jnp.broadcast_to(own[:, None], (B, H, Q, 1)).astype(sel.dtype)
    idx = jnp.concatenate([sel, own_b], axis=-1)
    blk_valid = jnp.concatenate([sel_valid, jnp.ones((B, H, Q, 1), bool)], axis=-1)
    J = k_sel + 1
    gather = jax.vmap(jax.vmap(lambda t, i: t[i]))
    kg = gather(kb, idx)
    vg = gather(vb, idx)
    k_pos = idx[..., None] * MOBA_BLOCK + jnp.arange(MOBA_BLOCK)
    rel = k_pos - q_pos[:, None, None]
    is_own = jnp.arange(J) == J - 1
    mask = jnp.where(is_own[:, None], rel <= 0, blk_valid[..., None])
    bias = rel_bias.T[jnp.arange(H)[:, None, None, None], t5_bucket(rel)].astype(jnp.float32)
    logits = jnp.einsum('bhqd,bhqjkd->bhqjk', q, kg).astype(jnp.float32) * (HEAD_DIM ** -0.5) + bias
    logits = jnp.where(mask, logits, NEG_INF)
    p = jax.nn.softmax(logits.reshape(B, H, Q, J * MOBA_BLOCK), axis=-1)
    p = p.reshape(B, H, Q, J, MOBA_BLOCK).astype(vg.dtype)
    return jnp.einsum('bhqjk,bhqjkd->bhqd', p, vg)


def moba_chunked(q, q_pos, kb, vb, means, rel_bias, chunk):
    B, H, Q, dh = q.shape
    n = Q // chunk
    qc = q.reshape(B, H, n, chunk, dh).transpose(2, 0, 1, 3, 4)
    pc = q_pos.reshape(n, chunk)
    out = lax.map(lambda a: moba_attend(a[0], a[1], kb, vb, means, rel_bias), (qc, pc))
    return out.transpose(1, 2, 0, 3, 4).reshape(B, H, Q, dh)


def rglru(xc, h0, w_r, b_r, w_i, b_i, lam):
    B, T, _ = xc.shape
    x32 = xc.astype(jnp.float32)
    xb = x32.reshape(B, T, N_RG_BLOCKS, RG_BLOCK)
    r = jax.nn.sigmoid(jnp.einsum('btnc,ncd->btnd', xb, w_r.astype(jnp.float32)).reshape(B, T, D_RNN)
                       + b_r.astype(jnp.float32))
    i = jax.nn.sigmoid(jnp.einsum('btnc,ncd->btnd', xb, w_i.astype(jnp.float32)).reshape(B, T, D_RNN)
                       + b_i.astype(jnp.float32))
    log_a = -RG_C * r * jax.nn.softplus(-lam.astype(jnp.float32))
    a = jnp.exp(log_a)
    b = jnp.sqrt(-jnp.expm1(2.0 * log_a)) * (i * x32)

    def combine(c1, c2):
        a1, b1 = c1
        a2, b2 = c2
        return a1 * a2, a2 * b1 + b2

    a_c, b_c = lax.associative_scan(combine, (a, b), axis=1)
    h = a_c * h0.astype(jnp.float32)[:, None, :] + b_c
    return h, h[:, -1]


def layer(x, q_pos, k_past, v_past, conv_past, h_past, chunk, w):
    (g_mix_pre, g_mix_post, w_in, rel_bias, conv_w, conv_b, w_r, b_r, w_i, b_i, lam,
     w_pa, w_pr, w_out, g_ffn_pre, g_ffn_post, w_gate_up, w_down) = w
    B, T, _ = x.shape
    h = rmsnorm(x, g_mix_pre)
    proj = h @ w_in
    q, k, v, xr, gb, gates = jnp.split(
        proj, [D_ATTN, 2 * D_ATTN, 3 * D_ATTN, 3 * D_ATTN + D_RNN, 3 * D_ATTN + 2 * D_RNN], axis=-1)
    q = q.reshape(B, T, N_HEADS, HEAD_DIM)
    k = k.reshape(B, T, N_HEADS, HEAD_DIM)
    v = v.reshape(B, T, N_HEADS, HEAD_DIM)
    k_all = jnp.concatenate([k_past.astype(k.dtype), k], axis=1)
    v_all = jnp.concatenate([v_past.astype(v.dtype), v], axis=1)
    T_all = k_all.shape[1]
    NB = -(-T_all // MOBA_BLOCK)
    pad = ((0, 0), (0, NB * MOBA_BLOCK - T_all), (0, 0), (0, 0))
    kb = jnp.pad(k_all, pad).reshape(B, NB, MOBA_BLOCK, N_HEADS, HEAD_DIM).transpose(0, 3, 1, 2, 4)
    vb = jnp.pad(v_all, pad).reshape(B, NB, MOBA_BLOCK, N_HEADS, HEAD_DIM).transpose(0, 3, 1, 2, 4)
    means = jnp.mean(kb.astype(jnp.float32), axis=3)
    o_attn = moba_chunked(q.transpose(0, 2, 1, 3), q_pos, kb, vb, means, rel_bias, chunk)
    o_attn = o_attn.transpose(0, 2, 1, 3).reshape(B, T, D_ATTN)
    xpad = jnp.concatenate([conv_past.astype(xr.dtype), xr], axis=1)
    xc = sum(xpad[:, j:j + T] * conv_w[j] for j in range(CONV_W)) + conv_b
    new_conv = xpad[:, -(CONV_W - 1):]
    hseq, h_last = rglru(xc, h_past, w_r, b_r, w_i, b_i, lam)
    o_rnn = hseq.astype(x.dtype) * jax.nn.gelu(gb)
    gate_a, gate_r = jnp.split(gates, 2, axis=-1)
    merged = jax.nn.sigmoid(gate_a) * (o_attn @ w_pa) + jax.nn.sigmoid(gate_r) * (o_rnn @ w_pr)
    x = x + rmsnorm(merged @ w_out, g_mix_post)
    f = rmsnorm(x, g_ffn_pre)
    g_, u_ = jnp.split(f @ w_gate_up, 2, axis=-1)
    x = x + rmsnorm((jax.nn.silu(g_) * u_) @ w_down, g_ffn_post)
    return x, k, v, new_conv, h_last.astype(h_past.dtype)


def setup_inputs(seed: int = 0) -> dict:
    key = jax.random.key(seed)
    ks = jax.random.split(key, 32)
    nrm = lambda k, shape, s: jax.random.normal(k, shape, jnp.float32) * s
    n_pages = PAST_LEN // PAGE_SIZE
    used = DEC_BATCH * n_pages
    n_phys = used + max(1, used // 4)
    page_table = jax.random.permutation(ks[4], n_phys)[:used].reshape(DEC_BATCH, n_pages).astype(jnp.int32)
    a0 = jax.random.uniform(ks[17], (DEPTH, D_RNN), jnp.float32, 0.9, 0.999)
    s = a0 ** (1.0 / RG_C)
    rg_lambda = jnp.log(s) - jnp.log1p(-s)
    return {
        "x_prompt": nrm(ks[0], (BATCH, SEQ, D_MODEL), 1.0),
        "x_sample": nrm(ks[1], (DEC_BATCH, DEC_SEQ, D_MODEL), 1.0),
        "cache_k": nrm(ks[2], (DEPTH, n_phys, PAGE_SIZE, N_HEADS, HEAD_DIM), 1.0),
        "cache_v": nrm(ks[3], (DEPTH, n_phys, PAGE_SIZE, N_HEADS, HEAD_DIM), 1.0),
        "page_table": page_table,
        "state_conv": nrm(ks[5], (DEPTH, DEC_BATCH, CONV_W - 1, D_RNN), 1.0),
        "state_h": nrm(ks[6], (DEPTH, DEC_BATCH, D_RNN), 0.5),
        "norm_mix_pre": 1.0 + nrm(ks[7], (DEPTH, D_MODEL), 0.01),
        "norm_mix_post": 1.0 + nrm(ks[8], (DEPTH, D_MODEL), 0.01),
        "w_in": nrm(ks[9], (DEPTH, D_MODEL, D_IN), D_MODEL ** -0.5),
        "rel_bias": nrm(ks[10], (NUM_BUCKETS, N_HEADS), 0.5),
        "conv_w": nrm(ks[11], (DEPTH, CONV_W, D_RNN), 0.5),
        "conv_b": nrm(ks[12], (DEPTH, D_RNN), 0.01),
        "rg_w_r": nrm(ks[13], (DEPTH, N_RG_BLOCKS, RG_BLOCK, RG_BLOCK), RG_BLOCK ** -0.5),
        "rg_b_r": nrm(ks[14], (DEPTH, D_RNN), 0.01),
        "rg_w_i": nrm(ks[15], (DEPTH, N_RG_BLOCKS, RG_BLOCK, RG_BLOCK), RG_BLOCK ** -0.5),
        "rg_b_i": nrm(ks[16], (DEPTH, D_RNN), 0.01),
        "rg_lambda": rg_lambda,
        "w_proj_attn": nrm(ks[18], (DEPTH, D_ATTN, D_MODEL), D_ATTN ** -0.5),
        "w_proj_rnn": nrm(ks[19], (DEPTH, D_RNN, D_MODEL), D_RNN ** -0.5),
        "w_out": nrm(ks[20], (DEPTH, D_MODEL, D_MODEL), D_MODEL ** -0.5),
        "norm_ffn_pre": 1.0 + nrm(ks[21], (DEPTH, D_MODEL), 0.01),
        "norm_ffn_post": 1.0 + nrm(ks[22], (DEPTH, D_MODEL), 0.01),
        "w_gate_up": nrm(ks[23], (DEPTH, D_MODEL, 2 * D_FF), D_MODEL ** -0.5),
        "w_down": nrm(ks[24], (DEPTH, D_FF, D_MODEL), D_FF ** -0.5),
    }


def reference(x_prompt, x_sample, cache_k, cache_v, page_table, state_conv, state_h,
              norm_mix_pre, norm_mix_post, w_in, rel_bias, conv_w, conv_b,
              rg_w_r, rg_b_r, rg_w_i, rg_b_i, rg_lambda, w_proj_attn, w_proj_rnn, w_out,
              norm_ffn_pre, norm_ffn_post, w_gate_up, w_down):
    Bp, Tp, _ = x_prompt.shape
    Bs, Ts, _ = x_sample.shape
    past_len = page_table.shape[1] * cache_k.shape[2]
    pos_prompt = jnp.arange(Tp, dtype=jnp.int32)
    pos_sample = past_len + jnp.arange(Ts, dtype=jnp.int32)
    chunk_prompt = math.gcd(Tp, Q_CHUNK)
    xp, xs = x_prompt, x_sample
    nkp, nvp, nks, nvs, ncp, nhp, ncs, nhs = [], [], [], [], [], [], [], []
    for l in range(DEPTH):
        wl = (norm_mix_pre[l], norm_mix_post[l], w_in[l], rel_bias, conv_w[l], conv_b[l],
              rg_w_r[l], rg_b_r[l], rg_w_i[l], rg_b_i[l], rg_lambda[l], w_proj_attn[l],
              w_proj_rnn[l], w_out[l], norm_ffn_pre[l], norm_ffn_post[l], w_gate_up[l], w_down[l])
        xp, kp, vp, cp, hp = layer(
            xp, pos_prompt,
            jnp.zeros((Bp, 0, N_HEADS, HEAD_DIM), xp.dtype), jnp.zeros((Bp, 0, N_HEADS, HEAD_DIM), xp.dtype),
            jnp.zeros((Bp, CONV_W - 1, D_RNN), xp.dtype), jnp.zeros((Bp, D_RNN), state_h.dtype),
            chunk_prompt, wl)
        k_past = cache_k[l][page_table].reshape(Bs, past_len, N_HEADS, HEAD_DIM)
        v_past = cache_v[l][page_table].reshape(Bs, past_len, N_HEADS, HEAD_DIM)
        xs, ks_, vs_, cs, hs = layer(xs, pos_sample, k_past, v_past, state_conv[l], state_h[l], 1, wl)
        nkp.append(kp); nvp.append(vp); nks.append(ks_); nvs.append(vs_)
        ncp.append(cp); nhp.append(hp); ncs.append(cs); nhs.append(hs)
    return (xp, xs, jnp.stack(nkp), jnp.stack(nvp), jnp.stack(nks), jnp.stack(nvs),
            jnp.stack(ncp), jnp.stack(nhp), jnp.stack(ncs), jnp.stack(nhs))
```

```python
import functools
import math

import numpy as np
import jax
import jax.numpy as jnp
from jax import lax
from jax.experimental import pallas as pl
from jax.experimental.pallas import tpu as pltpu

N_HEADS = 8
HEAD_DIM = 64
D_ATTN = N_HEADS * HEAD_DIM
D_RNN = 512
N_RG_BLOCKS = 8
CONV_W = 4
RG_C = 8.0
MOBA_BLOCK = 256
MOBA_TOP_K = 3
NUM_BUCKETS = 32
MAX_DISTANCE = 128
RMS_EPS = 1e-6
NEG_INF = -1e30

SUBLANES = 8
VMEM_LIMIT = 56 * 1024 * 1024
F32 = jnp.float32
BF16 = jnp.bfloat16
HIGHEST = lax.Precision.HIGHEST


def _params(*sem):
    return pltpu.CompilerParams(dimension_semantics=sem, vmem_limit_bytes=VMEM_LIMIT)


def _resident(shape):
    nd = len(shape)
    return pl.BlockSpec(shape, lambda *_: (0,) * nd, pipeline_mode=pl.Buffered(1))


def _rms(x, g):
    return x * lax.rsqrt(jnp.mean(x * x, axis=-1, keepdims=True) + RMS_EPS) * g


def _t5_bucket_np(n):
    n = np.maximum(n, 0)
    max_exact = NUM_BUCKETS // 2
    nf = np.maximum(n, 1).astype(np.float32)
    large = max_exact + (np.log(nf / np.float32(max_exact)) / np.float32(math.log(MAX_DISTANCE / max_exact))
                         * np.float32(NUM_BUCKETS - max_exact)).astype(np.int32)
    large = np.minimum(large, NUM_BUCKETS - 1)
    return np.where(n < max_exact, n, large).astype(np.int32)


def _in_proj_kernel(x_ref, g_ref, w_ref, q_ref, k_ref, v_ref, kb_ref, vt_ref, xr_ref, gg_ref, sa_ref, sr_ref):
    d_model = x_ref.shape[-1]
    hb = _rms(x_ref[...], g_ref[...]).astype(BF16)

    def proj(lo, width):
        return jnp.dot(hb, w_ref[:, lo:lo + width], preferred_element_type=F32)

    q_ref[...] = proj(0, D_ATTN)
    k = proj(D_ATTN, D_ATTN)
    k_ref[...] = k
    kb_ref[...] = k.astype(BF16)
    v = proj(2 * D_ATTN, D_ATTN)
    v_ref[...] = v
    for r in range(vt_ref.shape[0]):
        vt_ref[r] = v[r * MOBA_BLOCK:(r + 1) * MOBA_BLOCK, :].T.astype(BF16)
    xr_ref[...] = proj(3 * D_ATTN, D_RNN)
    gg_ref[...] = jax.nn.gelu(proj(3 * D_ATTN + D_RNN, D_RNN))
    base = 3 * D_ATTN + 2 * D_RNN
    sa_ref[...] = jax.nn.sigmoid(proj(base, d_model))
    sr_ref[...] = jax.nn.sigmoid(proj(base + d_model, d_model))


def _in_proj(x2d, g, w_in_b, tm):
    n, d_model = x2d.shape
    assert n % tm == 0 and tm % MOBA_BLOCK == 0
    blocks_per_tile = tm // MOBA_BLOCK
    row = lambda width: pl.BlockSpec((tm, width), lambda i: (i, 0))
    f32 = lambda width: jax.ShapeDtypeStruct((n, width), F32)
    return pl.pallas_call(
        _in_proj_kernel,
        grid=(n // tm,),
        in_specs=[row(d_model), _resident((1, d_model)), _resident(w_in_b.shape)],
        out_specs=[row(D_ATTN), row(D_ATTN), row(D_ATTN), row(D_ATTN),
                   pl.BlockSpec((blocks_per_tile, D_ATTN, MOBA_BLOCK), lambda i: (i, 0, 0)),
                   row(D_RNN), row(D_RNN), row(d_model), row(d_model)],
        out_shape=[f32(D_ATTN), f32(D_ATTN), f32(D_ATTN), jax.ShapeDtypeStruct((n, D_ATTN), BF16),
                   jax.ShapeDtypeStruct((n // MOBA_BLOCK, D_ATTN, MOBA_BLOCK), BF16),
                   f32(D_RNN), f32(D_RNN), f32(d_model), f32(d_model)],
        compiler_params=_params("parallel"),
        name="in_proj",
    )(x2d, g, w_in_b)


def _means_kernel(k_ref, mb_ref):
    nb = k_ref.shape[0]
    head_of_lane = lax.broadcasted_iota(jnp.int32, (1, D_ATTN), 1) // HEAD_DIM
    for n in range(nb):
        mean = jnp.sum(k_ref[n], axis=0, keepdims=True) * (1.0 / MOBA_BLOCK)
        for h in range(N_HEADS):
            mb_ref[h * nb + n:h * nb + n + 1, :] = jnp.where(head_of_lane == h, mean, 0.0)


def _block_means(k4):
    b, nb = k4.shape[:2]
    return pl.pallas_call(
        _means_kernel,
        grid=(b,),
        in_specs=[pl.BlockSpec((None, nb, MOBA_BLOCK, D_ATTN), lambda i: (i, 0, 0, 0))],
        out_specs=pl.BlockSpec((None, N_HEADS * nb, D_ATTN), lambda i: (i, 0, 0)),
        out_shape=jax.ShapeDtypeStruct((b, N_HEADS * nb, D_ATTN), F32),
        compiler_params=_params("parallel"),
        name="block_means",
    )(k4)


def _bias_table_kernel(rb_ref, bucket_ref, out_ref):
    bucket = bucket_ref[...]
    for h in range(N_HEADS):
        acc = jnp.full(bucket.shape, NEG_INF, F32)
        for b in range(NUM_BUCKETS):
            acc = jnp.where(bucket == b, rb_ref[b, h], acc)
        out_ref[h] = acc


def _prompt_bias_tables(rel_bias):
    kr = np.arange(2 * MOBA_BLOCK)[:, None]
    qr = np.arange(MOBA_BLOCK)[None, :]
    dist = qr + MOBA_BLOCK - kr
    bucket = np.where(dist >= 0, _t5_bucket_np(dist), -1).astype(np.int32)
    return pl.pallas_call(
        _bias_table_kernel,
        in_specs=[pl.BlockSpec(memory_space=pltpu.SMEM), pl.BlockSpec(memory_space=pltpu.VMEM)],
        out_specs=pl.BlockSpec(memory_space=pltpu.VMEM),
        out_shape=jax.ShapeDtypeStruct((N_HEADS,) + bucket.shape, F32),
        name="prompt_bias_tables",
    )(rel_bias, jnp.asarray(bucket))


def _sample_bias_kernel(rbt_ref, bucket_ref, out_ref):
    bucket = bucket_ref[...]
    acc = jnp.full(bucket.shape, NEG_INF, F32)
    for b in range(NUM_BUCKETS):
        acc = jnp.where(bucket == b, rbt_ref[:, b:b + 1], acc)
    out_ref[...] = acc


def _sample_bias_table(rel_bias, past_len, t_new, own_width):
    t = np.repeat(np.arange(t_new), N_HEADS)[:, None]
    kpos = np.arange(past_len + own_width)[None, :]
    dist = past_len + t - kpos
    valid = (dist >= 0) & (kpos < past_len + t_new)
    bucket = np.where(valid, _t5_bucket_np(dist), -1).astype(np.int32)
    rbt = jnp.tile(rel_bias.T, (t_new, 1))
    return pl.pallas_call(
        _sample_bias_kernel,
        out_shape=jax.ShapeDtypeStruct(bucket.shape, F32),
        name="sample_bias_table",
    )(rbt, jnp.asarray(bucket))


def _select_rows(scores, n_valid, axis):
    nb = scores.shape[axis]
    idx = lax.broadcasted_iota(jnp.int32, scores.shape, axis)
    beaten = jnp.zeros(scores.shape, jnp.int32)
    for n in range(nb):
        other = lax.slice_in_dim(scores, n, n + 1, axis=axis)
        beats = (other > scores) | ((other == scores) & (n < idx))
        beaten = beaten + jnp.where(beats, jnp.where(n < n_valid, 1, 0), 0)
    return (beaten < MOBA_TOP_K) & (idx < n_valid)


def _prompt_attn_kernel(rb_ref, q_ref, kb_ref, vt_ref, mb_ref, tab_ref, o_ref, sel_ref, ot_ref):
    i = pl.program_id(1)
    nb = kb_ref.shape[0]
    prev = jnp.maximum(i - 1, 0)
    qt = q_ref[...].T
    sct = jnp.dot(mb_ref[...], qt, precision=HIGHEST, preferred_element_type=F32)
    row_in_pair = lax.broadcasted_iota(jnp.int32, (2 * HEAD_DIM, MOBA_BLOCK), 0) // HEAD_DIM

    for h in range(N_HEADS):
        far_bias = rb_ref[NUM_BUCKETS - 1, h]
        selected = _select_rows(sct[h * nb:(h + 1) * nb, :], i, axis=0)
        block_id = lax.broadcasted_iota(jnp.int32, selected.shape, 0)
        sel_ref[...] = jnp.where(selected, jnp.where(block_id == i - 1, 0.0, far_bias), NEG_INF)

        pair = h // 2
        q_pair = qt[pair * 2 * HEAD_DIM:(pair + 1) * 2 * HEAD_DIM, :] * (HEAD_DIM ** -0.5)
        qm = jnp.where(row_in_pair == h % 2, q_pair, 0.0).astype(BF16)

        def block_logits(j):
            k_pair = kb_ref[j, :, pair * 2 * HEAD_DIM:(pair + 1) * 2 * HEAD_DIM]
            return jnp.dot(k_pair, qm, preferred_element_type=F32)

        def v_block(j):
            return vt_ref[j, h * HEAD_DIM:(h + 1) * HEAD_DIM, :]

        s = block_logits(i) + tab_ref[h, MOBA_BLOCK:, :]
        m = jnp.max(s, axis=0, keepdims=True)
        p = jnp.exp(s - m)
        l = jnp.sum(p, axis=0, keepdims=True)
        acc = jnp.dot(v_block(i), p.astype(BF16), preferred_element_type=F32)

        def step(j, bias, carry):
            m, l, acc = carry
            s = block_logits(j) + bias + sel_ref[pl.ds(j, 1), :]
            m_new = jnp.maximum(m, jnp.max(s, axis=0, keepdims=True))
            alpha = jnp.exp(m - m_new)
            p = jnp.exp(s - m_new)
            l = alpha * l + jnp.sum(p, axis=0, keepdims=True)
            acc = alpha * acc + jnp.dot(v_block(j), p.astype(BF16), preferred_element_type=F32)
            return m_new, l, acc

        carry = step(prev, tab_ref[h, :MOBA_BLOCK, :], (m, l, acc))
        m, l, acc = lax.fori_loop(0, prev, lambda j, c: step(j, 0.0, c), carry)
        ot_ref[h * HEAD_DIM:(h + 1) * HEAD_DIM, :] = acc / l
    o_ref[...] = ot_ref[...].T


def _prompt_attention(rel_bias, q, kb4, vt4, mb, tables):
    b, nb = kb4.shape[:2]
    t = nb * MOBA_BLOCK
    return pl.pallas_call(
        _prompt_attn_kernel,
        grid=(b, nb),
        in_specs=[pl.BlockSpec(memory_space=pltpu.SMEM),
                  pl.BlockSpec((None, MOBA_BLOCK, D_ATTN), lambda bi, i: (bi, i, 0)),
                  pl.BlockSpec((None, nb, MOBA_BLOCK, D_ATTN), lambda bi, i: (bi, 0, 0, 0)),
                  pl.BlockSpec((None, nb, D_ATTN, MOBA_BLOCK), lambda bi, i: (bi, 0, 0, 0)),
                  pl.BlockSpec((None, N_HEADS * nb, D_ATTN), lambda bi, i: (bi, 0, 0)),
                  _resident(tables.shape)],
        out_specs=pl.BlockSpec((None, MOBA_BLOCK, D_ATTN), lambda bi, i: (bi, i, 0)),
        out_shape=jax.ShapeDtypeStruct((b, t, D_ATTN), F32),
        scratch_shapes=[pltpu.VMEM((nb, MOBA_BLOCK), F32), pltpu.VMEM((D_ATTN, MOBA_BLOCK), F32)],
        compiler_params=_params("parallel", "arbitrary"),
        name="prompt_attention",
    )(rel_bias, q, kb4, vt4, mb, tables)


def _sample_attn_kernel(n_pages, pt_ref, q_ref, kn_ref, vn_ref, bias_ref, *refs):
    k_pages = refs[:n_pages]
    v_pages = refs[n_pages:2 * n_pages]
    o_ref = refs[2 * n_pages]
    knp_ref, vnp_ref, means_ref = refs[2 * n_pages + 1:]
    del pt_ref
    t_new = q_ref.shape[0]
    page = k_pages[0].shape[0]
    pages_per_block = MOBA_BLOCK // page
    nb = n_pages // pages_per_block
    past_len = n_pages * page

    head_of_row = lax.broadcasted_iota(jnp.int32, (N_HEADS, D_ATTN), 0)
    head_of_lane = lax.broadcasted_iota(jnp.int32, (N_HEADS, D_ATTN), 1) // HEAD_DIM
    head_mask = head_of_row == head_of_lane
    qrows = jnp.concatenate(
        [jnp.where(head_mask, jnp.broadcast_to(q_ref[t:t + 1, :], (N_HEADS, D_ATTN)), 0.0) for t in range(t_new)],
        axis=0)

    for n in range(nb):
        means_ref[n:n + 1, :] = sum(jnp.sum(k_pages[n * pages_per_block + r][...], axis=0, keepdims=True)
                                    for r in range(pages_per_block)) * (1.0 / MOBA_BLOCK)
    nt = (((1,), (1,)), ((), ()))
    scores = lax.dot_general(qrows, means_ref[...], nt, precision=HIGHEST, preferred_element_type=F32)
    selected = jnp.where(_select_rows(scores, nb, axis=1), 1.0, 0.0)

    qb = (qrows * (HEAD_DIM ** -0.5)).astype(BF16)
    s_pages = []
    for p in range(n_pages):
        s = lax.dot_general(qb, k_pages[p][...].astype(BF16), nt, preferred_element_type=F32)
        keep = selected[:, p // pages_per_block:p // pages_per_block + 1] > 0.5
        s_pages.append(jnp.where(keep, s + bias_ref[:, p * page:(p + 1) * page], NEG_INF))
    s_past = jnp.concatenate(s_pages, axis=1)

    knp_ref[...] = jnp.zeros(knp_ref.shape, F32)
    vnp_ref[...] = jnp.zeros(vnp_ref.shape, F32)
    knp_ref[0:t_new, :] = kn_ref[...]
    vnp_ref[0:t_new, :] = vn_ref[...]
    s_own = (lax.dot_general(qb, knp_ref[...].astype(BF16), nt, preferred_element_type=F32)
             + bias_ref[:, past_len:])

    m = jnp.maximum(jnp.max(s_past, axis=1, keepdims=True), jnp.max(s_own, axis=1, keepdims=True))
    p_past = jnp.exp(s_past - m)
    p_own = jnp.exp(s_own - m)
    l = jnp.sum(p_past, axis=1, keepdims=True) + jnp.sum(p_own, axis=1, keepdims=True)
    o = jnp.dot(p_own.astype(BF16), vnp_ref[...].astype(BF16), preferred_element_type=F32)
    for p in range(n_pages):
        o = o + jnp.dot(p_past[:, p * page:(p + 1) * page].astype(BF16), v_pages[p][...].astype(BF16),
                        preferred_element_type=F32)
    o = o / l
    for t in range(t_new):
        o_ref[t:t + 1, :] = jnp.sum(jnp.where(head_mask, o[t * N_HEADS:(t + 1) * N_HEADS, :], 0.0),
                                    axis=0, keepdims=True)


def _sample_attention(page_table, q, k_new, v_new, bias, cache_k, cache_v):
    bs, t_new, _ = q.shape
    n_pages = page_table.shape[1]
    page = cache_k.shape[1]
    own_width = bias.shape[1] - n_pages * page
    per_seq = pl.BlockSpec((None, t_new, D_ATTN), lambda s, pt: (s, 0, 0))
    page_spec = lambda p: pl.BlockSpec((None, page, D_ATTN), lambda s, pt: (pt[s, p], 0, 0))
    grid_spec = pltpu.PrefetchScalarGridSpec(
        num_scalar_prefetch=1,
        grid=(bs,),
        in_specs=[per_seq, per_seq, per_seq, pl.BlockSpec(bias.shape, lambda s, pt: (0, 0))]
                 + [page_spec(p) for p in range(n_pages)] * 2,
        out_specs=per_seq,
        scratch_shapes=[pltpu.VMEM((own_width, D_ATTN), F32), pltpu.VMEM((own_width, D_ATTN), F32),
                        pltpu.VMEM((n_pages * page // MOBA_BLOCK, D_ATTN), F32)],
    )
    return pl.pallas_call(
        functools.partial(_sample_attn_kernel, n_pages),
        grid_spec=grid_spec,
        out_shape=jax.ShapeDtypeStruct((bs, t_new, D_ATTN), F32),
        compiler_params=_params("parallel"),
        name="sample_attention",
    )(page_table, q, k_new, v_new, bias, *([cache_k] * n_pages), *([cache_v] * n_pages))


def _rglru_coeffs(xc, wr_ref, br_ref, wi_ref, bi_ref, lam_ref):
    xb = xc.astype(BF16)
    r = jax.nn.sigmoid(jnp.dot(xb, wr_ref[...], preferred_element_type=F32) + br_ref[...])
    i = jax.nn.sigmoid(jnp.dot(xb, wi_ref[...], preferred_element_type=F32) + bi_ref[...])
    log_a = -RG_C * r * jax.nn.softplus(-lam_ref[...])
    a = jnp.exp(log_a)
    b = jnp.sqrt(-jnp.tanh(log_a) * (a * a + 1.0)) * (i * xc)
    return a, b


def _prompt_rnn_kernel(xr_ref, gg_ref, cw_ref, cb_ref, wr_ref, br_ref, wi_ref, bi_ref, lam_ref,
                       o_ref, hl_ref, xbuf_ref, h_ref):
    tt = xr_ref.shape[0]
    halo = SUBLANES

    @pl.when(pl.program_id(1) == 0)
    def _():
        xbuf_ref[0:halo, :] = jnp.zeros((halo, D_RNN), F32)
        h_ref[...] = jnp.zeros(h_ref.shape, F32)

    xbuf_ref[halo:, :] = xr_ref[...]
    xc = cb_ref[...] + sum(xbuf_ref[halo - (CONV_W - 1 - j):halo - (CONV_W - 1 - j) + tt, :] * cw_ref[j:j + 1, :]
                           for j in range(CONV_W))
    xbuf_ref[0:halo, :] = xbuf_ref[tt:tt + halo, :]
    a, b = _rglru_coeffs(xc, wr_ref, br_ref, wi_ref, bi_ref, lam_ref)

    row = lax.broadcasted_iota(jnp.int32, (tt, D_RNN), 0) % SUBLANES
    shift = 1
    while shift < SUBLANES:
        a_prev = pltpu.roll(a, shift, axis=0)
        b_prev = pltpu.roll(b, shift, axis=0)
        ok = row >= shift
        b = jnp.where(ok, a * b_prev + b, b)
        a = jnp.where(ok, a * a_prev, a)
        shift *= 2
    h = h_ref[...]
    for g in range(tt // SUBLANES):
        sl = slice(g * SUBLANES, (g + 1) * SUBLANES)
        hg = a[sl, :] * h + b[sl, :]
        o_ref[sl, :] = hg * gg_ref[sl, :]
        h = hg[SUBLANES - 1:SUBLANES, :]
    h_ref[...] = h
    hl_ref[...] = h


def _prompt_rnn(xr, gg, weights, tt):
    b, t, _ = xr.shape
    assert t % tt == 0 and tt % SUBLANES == 0
    tile = pl.BlockSpec((None, tt, D_RNN), lambda bi, i: (bi, i, 0))
    return pl.pallas_call(
        _prompt_rnn_kernel,
        grid=(b, t // tt),
        in_specs=[tile, tile] + [_resident(w.shape) for w in weights],
        out_specs=[tile, pl.BlockSpec((None, 1, D_RNN), lambda bi, i: (bi, 0, 0))],
        out_shape=[jax.ShapeDtypeStruct((b, t, D_RNN), F32), jax.ShapeDtypeStruct((b, 1, D_RNN), F32)],
        scratch_shapes=[pltpu.VMEM((tt + SUBLANES, D_RNN), F32), pltpu.VMEM((1, D_RNN), F32)],
        compiler_params=_params("parallel", "arbitrary"),
        name="prompt_rnn",
    )(xr, gg, *weights)


def _sample_rnn_kernel(xr_ref, gg_ref, conv_ref, h0_ref, cw_ref, cb_ref, wr_ref, br_ref, wi_ref, bi_ref, lam_ref,
                       o_ref, hl_ref):
    t_new = xr_ref.shape[0]
    past = [conv_ref[j] for j in range(CONV_W - 1)] + [xr_ref[t] for t in range(t_new)]
    h = h0_ref[...]
    for t in range(t_new):
        xc = cb_ref[...] + sum(past[t + j] * cw_ref[j:j + 1, :] for j in range(CONV_W))
        a, b = _rglru_coeffs(xc, wr_ref, br_ref, wi_ref, bi_ref, lam_ref)
        h = a * h + b
        o_ref[t] = h * gg_ref[t]
    hl_ref[...] = h


def _sample_rnn(xr_t, gg_t, conv_t, h0, weights):
    return pl.pallas_call(
        _sample_rnn_kernel,
        out_shape=[jax.ShapeDtypeStruct(xr_t.shape, F32), jax.ShapeDtypeStruct(h0.shape, F32)],
        compiler_params=pltpu.CompilerParams(vmem_limit_bytes=VMEM_LIMIT),
        name="sample_rnn",
    )(xr_t, gg_t, conv_t, h0, *weights)


def _merge_kernel(x_ref, oa_ref, orn_ref, sa_ref, sr_ref, wpa_ref, wpr_ref, wo_ref, g_ref, y_ref):
    pa = jnp.dot(oa_ref[...].astype(BF16), wpa_ref[...], preferred_element_type=F32)
    pr = jnp.dot(orn_ref[...].astype(BF16), wpr_ref[...], preferred_element_type=F32)
    merged = sa_ref[...] * pa + sr_ref[...] * pr
    out = jnp.dot(merged.astype(BF16), wo_ref[...], preferred_element_type=F32)
    y_ref[...] = x_ref[...] + _rms(out, g_ref[...])


def _merge(x2d, oa, orn, sa, sr, wpa, wpr, wo, g, tm):
    n, d_model = x2d.shape
    row = lambda width: pl.BlockSpec((tm, width), lambda i: (i, 0))
    return pl.pallas_call(
        _merge_kernel,
        grid=(n // tm,),
        in_specs=[row(d_model), row(D_ATTN), row(D_RNN), row(d_model), row(d_model),
                  _resident(wpa.shape), _resident(wpr.shape), _resident(wo.shape), _resident(g.shape)],
        out_specs=row(d_model),
        out_shape=jax.ShapeDtypeStruct((n, d_model), F32),
        compiler_params=_params("parallel"),
        name="merge",
    )(x2d, oa, orn, sa, sr, wpa, wpr, wo, g)


def _ffn_kernel(chunk, x_ref, gpre_ref, wgu_ref, wd_ref, gpost_ref, y_ref):
    d_ff = wd_ref.shape[0]
    x = x_ref[...]
    fb = _rms(x, gpre_ref[...]).astype(BF16)
    y = jnp.zeros(x.shape, F32)
    for c in range(0, d_ff, chunk):
        gate = jnp.dot(fb, wgu_ref[:, c:c + chunk], preferred_element_type=F32)
        up = jnp.dot(fb, wgu_ref[:, d_ff + c:d_ff + c + chunk], preferred_element_type=F32)
        act = (jax.nn.silu(gate) * up).astype(BF16)
        y = y + jnp.dot(act, wd_ref[c:c + chunk, :], preferred_element_type=F32)
    y_ref[...] = x + _rms(y, gpost_ref[...])


def _ffn(x2d, gpre, wgu, wd, gpost, tm, chunk):
    n, d_model = x2d.shape
    assert wd.shape[0] % chunk == 0
    row = pl.BlockSpec((tm, d_model), lambda i: (i, 0))
    return pl.pallas_call(
        functools.partial(_ffn_kernel, chunk),
        grid=(n // tm,),
        in_specs=[row, _resident(gpre.shape), _resident(wgu.shape), _resident(wd.shape), _resident(gpost.shape)],
        out_specs=row,
        out_shape=jax.ShapeDtypeStruct((n, d_model), F32),
        compiler_params=_params("parallel"),
        name="ffn",
    )(x2d, gpre, wgu, wd, gpost)


def _block_diag(w):
    nblk, c, _ = w.shape
    eye = jnp.eye(nblk, dtype=w.dtype)
    return (w[:, :, None, :] * eye[:, None, :, None]).reshape(nblk * c, nblk * c)


def kernel(x_prompt, x_sample, cache_k, cache_v, page_table, state_conv, state_h, norm_mix_pre, norm_mix_post,
           w_in, rel_bias, conv_w, conv_b, rg_w_r, rg_b_r, rg_w_i, rg_b_i, rg_lambda, w_proj_attn, w_proj_rnn,
           w_out, norm_ffn_pre, norm_ffn_post, w_gate_up, w_down):
    depth = w_in.shape[0]
    assert depth == 1
    bp, tp, d_model = x_prompt.shape
    bs, ts, _ = x_sample.shape
    n_phys, page = cache_k.shape[1:3]
    n_pages = page_table.shape[1]
    past_len = n_pages * page
    assert tp % MOBA_BLOCK == 0 and past_len % MOBA_BLOCK == 0 and MOBA_BLOCK % page == 0
    assert past_len // MOBA_BLOCK > MOBA_TOP_K and CONV_W - 1 <= ts <= MOBA_BLOCK
    nbp = tp // MOBA_BLOCK
    l = 0
    row2 = lambda v: v.reshape(1, -1)

    w_in_b = w_in[l].astype(BF16)
    rnn_w = (conv_w[l], row2(conv_b[l]), _block_diag(rg_w_r[l]).astype(BF16), row2(rg_b_r[l]),
             _block_diag(rg_w_i[l]).astype(BF16), row2(rg_b_i[l]), row2(rg_lambda[l]))
    wpa, wpr, wo = w_proj_attn[l].astype(BF16), w_proj_rnn[l].astype(BF16), w_out[l].astype(BF16)
    wgu, wd = w_gate_up[l].astype(BF16), w_down[l].astype(BF16)
    g_pre, g_post = row2(norm_mix_pre[l]), row2(norm_mix_post[l])
    gf_pre, gf_post = row2(norm_ffn_pre[l]), row2(norm_ffn_post[l])

    def tail(x2d, oa, orn, sa, sr, tm):
        x1 = _merge(x2d, oa, orn, sa, sr, wpa, wpr, wo, g_post, tm)
        return _ffn(x1, gf_pre, wgu, wd, gf_post, tm, 256)

    tm = 512
    xp2 = x_prompt.reshape(bp * tp, d_model)
    q, k, v, kb, vt, xr, gg, sa, sr = _in_proj(xp2, g_pre, w_in_b, tm)
    mb = _block_means(k.reshape(bp, nbp, MOBA_BLOCK, D_ATTN))
    tables = _prompt_bias_tables(rel_bias)
    oa = _prompt_attention(rel_bias, q.reshape(bp, tp, D_ATTN), kb.reshape(bp, nbp, MOBA_BLOCK, D_ATTN),
                           vt.reshape(bp, nbp, D_ATTN, MOBA_BLOCK), mb, tables)
    xr3 = xr.reshape(bp, tp, D_RNN)
    orn, h_last = _prompt_rnn(xr3, gg.reshape(bp, tp, D_RNN), rnn_w, 256)
    y_prompt = tail(xp2, oa.reshape(bp * tp, D_ATTN), orn.reshape(bp * tp, D_RNN), sa, sr, tm).reshape(bp, tp, d_model)
    new_k_prompt = k.reshape(1, bp, tp, N_HEADS, HEAD_DIM)
    new_v_prompt = v.reshape(1, bp, tp, N_HEADS, HEAD_DIM)
    new_conv_prompt = xr3[:, tp - (CONV_W - 1):, :][None]
    new_h_prompt = h_last.reshape(1, bp, D_RNN).astype(state_h.dtype)

    ns = bs * ts
    xs2 = x_sample.reshape(ns, d_model)
    qs, ks, vs, _, _, xrs, ggs, sas, srs = _in_proj(xs2, g_pre, w_in_b, ns)
    own_width = 128
    bias_s = _sample_bias_table(rel_bias, past_len, ts, own_width)
    oas = _sample_attention(page_table, qs.reshape(bs, ts, D_ATTN), ks.reshape(bs, ts, D_ATTN),
                            vs.reshape(bs, ts, D_ATTN), bias_s,
                            cache_k[l].reshape(n_phys, page, D_ATTN), cache_v[l].reshape(n_phys, page, D_ATTN))
    xrs3 = xrs.reshape(bs, ts, D_RNN)
    to_time_major = lambda a: jnp.swapaxes(a, 0, 1)
    orns_t, hs_last = _sample_rnn(to_time_major(xrs3), to_time_major(ggs.reshape(bs, ts, D_RNN)),
                                  to_time_major(state_conv[l]), state_h[l], rnn_w)
    orns = to_time_major(orns_t).reshape(ns, D_RNN)
    y_sample = tail(xs2, oas.reshape(ns, D_ATTN), orns, sas, srs, ns).reshape(bs, ts, d_model)
    new_k_sample = ks.reshape(1, bs, ts, N_HEADS, HEAD_DIM)
    new_v_sample = vs.reshape(1, bs, ts, N_HEADS, HEAD_DIM)
    new_conv_sample = xrs3[:, ts - (CONV_W - 1):, :][None]
    new_h_sample = hs_last[None].astype(state_h.dtype)

    return (y_prompt, y_sample, new_k_prompt, new_v_prompt, new_k_sample, new_v_sample,
            new_conv_prompt, new_h_prompt, new_conv_sample, new_h_sample)
```

```python
import functools
import math

import numpy as np
import jax
import jax.numpy as jnp
from jax import lax
from jax.experimental import pallas as pl
from jax.experimental.pallas import tpu as pltpu

N_HEADS = 8
HEAD_DIM = 64
D_ATTN = N_HEADS * HEAD_DIM
D_RNN = 512
N_RG_BLOCKS = 8
CONV_W = 4
RG_C = 8.0
MOBA_BLOCK = 256
MOBA_TOP_K = 3
NUM_BUCKETS = 32
MAX_DISTANCE = 128
RMS_EPS = 1e-6
NEG_INF = -1e30

SUBLANES = 8
VMEM_LIMIT = 56 * 1024 * 1024
F32 = jnp.float32
BF16 = jnp.bfloat16
HIGHEST = lax.Precision.HIGHEST


def _params(*sem):
    return pltpu.CompilerParams(dimension_semantics=sem, vmem_limit_bytes=VMEM_LIMIT)


def _resident(shape):
    nd = len(shape)
    return pl.BlockSpec(shape, lambda *_: (0,) * nd, pipeline_mode=pl.Buffered(1))


def _rms(x, g):
    return x * lax.rsqrt(jnp.mean(x * x, axis=-1, keepdims=True) + RMS_EPS) * g


def _t5_bucket_np(n):
    n = np.maximum(n, 0)
    max_exact = NUM_BUCKETS // 2
    nf = np.maximum(n, 1).astype(np.float32)
    large = max_exact + (np.log(nf / np.float32(max_exact)) / np.float32(math.log(MAX_DISTANCE / max_exact))
                         * np.float32(NUM_BUCKETS - max_exact)).astype(np.int32)
    large = np.minimum(large, NUM_BUCKETS - 1)
    return np.where(n < max_exact, n, large).astype(np.int32)


def _in_proj_kernel(natural_kv, x_ref, g_ref, w_ref, *out_refs):
    if natural_kv:
        q_ref, k_ref, v_ref, kt_ref, vt_ref, xr_ref, gg_ref, sa_ref, sr_ref = out_refs
    else:
        q_ref, kb_ref, vtb_ref, means_ref, kt_ref, vt_ref, xr_ref, gg_ref, sa_ref, sr_ref = out_refs
    d_model = x_ref.shape[-1]
    hb = _rms(x_ref[...], g_ref[...]).astype(BF16)

    def proj(lo, width):
        return jnp.dot(hb, w_ref[:, lo:lo + width], preferred_element_type=F32)

    q_ref[...] = proj(0, D_ATTN)
    k = proj(D_ATTN, D_ATTN)
    v = proj(2 * D_ATTN, D_ATTN)
    groups, _, width = kt_ref.shape
    for r in range(groups):
        kt_ref[r] = k[r * width:(r + 1) * width, :].T
        vt_ref[r] = v[r * width:(r + 1) * width, :].T
    if natural_kv:
        k_ref[...] = k
        v_ref[...] = v
    else:
        kb_ref[...] = k.astype(BF16)
        for r in range(vtb_ref.shape[0]):
            blk = slice(r * MOBA_BLOCK, (r + 1) * MOBA_BLOCK)
            vtb_ref[r] = v[blk, :].T.astype(BF16)
            means_ref[r] = jnp.sum(k[blk, :], axis=0, keepdims=True) * (1.0 / MOBA_BLOCK)
    xr_ref[...] = proj(3 * D_ATTN, D_RNN)
    gg_ref[...] = jax.nn.gelu(proj(3 * D_ATTN + D_RNN, D_RNN))
    base = 3 * D_ATTN + 2 * D_RNN
    sa_ref[...] = jax.nn.sigmoid(proj(base, d_model))
    sr_ref[...] = jax.nn.sigmoid(proj(base + d_model, d_model))


def _in_proj(x2d, g, w_in_b, tm, group_len, natural_kv):
    n, d_model = x2d.shape
    assert n % tm == 0 and n % group_len == 0 and (tm % group_len == 0 or group_len % tm == 0)
    tiles_per_group = max(group_len // tm, 1)
    t_block = (max(tm // group_len, 1), D_ATTN, min(tm, group_len))
    t_spec = pl.BlockSpec(t_block, lambda i: (i // tiles_per_group, 0, i % tiles_per_group))
    t_shape = jax.ShapeDtypeStruct((n // group_len, D_ATTN, group_len), F32)
    row = lambda width: pl.BlockSpec((tm, width), lambda i: (i, 0))
    f32 = lambda width: jax.ShapeDtypeStruct((n, width), F32)
    if natural_kv:
        kv_specs = [row(D_ATTN), row(D_ATTN)]
        kv_shapes = [f32(D_ATTN), f32(D_ATTN)]
    else:
        assert tm % MOBA_BLOCK == 0
        blocks = tm // MOBA_BLOCK
        kv_specs = [row(D_ATTN), pl.BlockSpec((blocks, D_ATTN, MOBA_BLOCK), lambda i: (i, 0, 0)),
                    pl.BlockSpec((blocks, 1, D_ATTN), lambda i: (i, 0, 0))]
        kv_shapes = [jax.ShapeDtypeStruct((n, D_ATTN), BF16),
                     jax.ShapeDtypeStruct((n // MOBA_BLOCK, D_ATTN, MOBA_BLOCK), BF16),
                     jax.ShapeDtypeStruct((n // MOBA_BLOCK, 1, D_ATTN), F32)]
    return pl.pallas_call(
        functools.partial(_in_proj_kernel, natural_kv),
        grid=(n // tm,),
        in_specs=[row(d_model), _resident((1, d_model)), _resident(w_in_b.shape)],
        out_specs=[row(D_ATTN)] + kv_specs + [t_spec, t_spec, row(D_RNN), row(D_RNN), row(d_model), row(d_model)],
        out_shape=[f32(D_ATTN)] + kv_shapes + [t_shape, t_shape, f32(D_RNN), f32(D_RNN), f32(d_model), f32(d_model)],
        compiler_params=_params("parallel"),
        name="in_proj",
    )(x2d, g, w_in_b)


def _bias_table_kernel(rb_ref, bucket_ref, out_ref):
    bucket = bucket_ref[...]
    for h in range(N_HEADS):
        acc = jnp.full(bucket.shape, NEG_INF, F32)
        for b in range(NUM_BUCKETS):
            acc = jnp.where(bucket == b, rb_ref[b, h], acc)
        out_ref[h] = acc


def _prompt_bias_tables(rel_bias):
    kr = np.arange(2 * MOBA_BLOCK)[:, None]
    qr = np.arange(MOBA_BLOCK)[None, :]
    dist = qr + MOBA_BLOCK - kr
    bucket = np.where(dist >= 0, _t5_bucket_np(dist), -1).astype(np.int32)
    return pl.pallas_call(
        _bias_table_kernel,
        in_specs=[pl.BlockSpec(memory_space=pltpu.SMEM), pl.BlockSpec(memory_space=pltpu.VMEM)],
        out_specs=pl.BlockSpec(memory_space=pltpu.VMEM),
        out_shape=jax.ShapeDtypeStruct((N_HEADS,) + bucket.shape, F32),
        name="prompt_bias_tables",
    )(rel_bias, jnp.asarray(bucket))


def _sample_bias_kernel(rbt_ref, bucket_ref, out_ref):
    bucket = bucket_ref[...]
    acc = jnp.full(bucket.shape, NEG_INF, F32)
    for b in range(NUM_BUCKETS):
        acc = jnp.where(bucket == b, rbt_ref[:, b:b + 1], acc)
    out_ref[...] = acc


def _sample_bias_table(rel_bias, past_len, t_new, own_width):
    t = np.repeat(np.arange(t_new), N_HEADS)[:, None]
    kpos = np.arange(past_len + own_width)[None, :]
    dist = past_len + t - kpos
    valid = (dist >= 0) & (kpos < past_len + t_new)
    bucket = np.where(valid, _t5_bucket_np(dist), -1).astype(np.int32)
    rbt = jnp.tile(rel_bias.T, (t_new, 1))
    return pl.pallas_call(
        _sample_bias_kernel,
        out_shape=jax.ShapeDtypeStruct(bucket.shape, F32),
        name="sample_bias_table",
    )(rbt, jnp.asarray(bucket))


def _select_rows(scores, n_valid, axis):
    nb = scores.shape[axis]
    idx = lax.broadcasted_iota(jnp.int32, scores.shape, axis)
    beaten = jnp.zeros(scores.shape, jnp.int32)
    for n in range(nb):
        other = lax.slice_in_dim(scores, n, n + 1, axis=axis)
        beats = (other > scores) | ((other == scores) & (n < idx))
        beaten = beaten + jnp.where(beats, jnp.where(n < n_valid, 1, 0), 0)
    return (beaten < MOBA_TOP_K) & (idx < n_valid)


def _prompt_attn_kernel(rb_ref, q_ref, kb_ref, vtb_ref, means_ref, tab_ref, o_ref,
                        mb_ref, sel_ref, qm_ref, m_ref, l_ref, acc_ref):
    i = pl.program_id(1)
    nb = kb_ref.shape[0]
    prev = jnp.maximum(i - 1, 0)
    pair_rows = 2 * HEAD_DIM
    qt = q_ref[...].T

    means = means_ref[...]
    head_of_lane = lax.broadcasted_iota(jnp.int32, means.shape, 1) // HEAD_DIM
    for h in range(N_HEADS):
        mb_ref[h * nb:(h + 1) * nb, :] = jnp.where(head_of_lane == h, means, 0.0)
    sct = jnp.dot(mb_ref[...], qt, precision=HIGHEST, preferred_element_type=F32)

    row_in_pair = lax.broadcasted_iota(jnp.int32, (pair_rows, MOBA_BLOCK), 0) // HEAD_DIM
    block_id = lax.broadcasted_iota(jnp.int32, (nb, MOBA_BLOCK), 0)
    for h in range(N_HEADS):
        far_bias = rb_ref[NUM_BUCKETS - 1, h]
        selected = _select_rows(sct[h * nb:(h + 1) * nb, :], i, axis=0)
        sel_ref[h * nb:(h + 1) * nb, :] = jnp.where(
            selected, jnp.where(block_id == i - 1, 0.0, far_bias), NEG_INF)
        q_pair = qt[(h // 2) * pair_rows:(h // 2 + 1) * pair_rows, :] * (HEAD_DIM ** -0.5)
        qm_ref[h] = jnp.where(row_in_pair == h % 2, q_pair, 0.0).astype(BF16)

    def attend(j, table_rows, first):
        for h in range(N_HEADS):
            pair = h // 2
            hs = slice(h * HEAD_DIM, (h + 1) * HEAD_DIM)
            k_pair = kb_ref[j, :, pair * pair_rows:(pair + 1) * pair_rows]
            s = jnp.dot(k_pair, qm_ref[h], preferred_element_type=F32)
            if table_rows is not None:
                s = s + tab_ref[h, table_rows, :]
            v_t = vtb_ref[j, hs, :]
            if first:
                m_new = jnp.max(s, axis=0, keepdims=True)
                p = jnp.exp(s - m_new)
                l_ref[h:h + 1, :] = jnp.sum(p, axis=0, keepdims=True)
                acc_ref[hs, :] = jnp.dot(v_t, p.astype(BF16), preferred_element_type=F32)
            else:
                s = s + sel_ref[pl.ds(h * nb + j, 1), :]
                m_old = m_ref[h:h + 1, :]
                m_new = jnp.maximum(m_old, jnp.max(s, axis=0, keepdims=True))
                alpha = jnp.exp(m_old - m_new)
                p = jnp.exp(s - m_new)
                l_ref[h:h + 1, :] = alpha * l_ref[h:h + 1, :] + jnp.sum(p, axis=0, keepdims=True)
                acc_ref[hs, :] = alpha * acc_ref[hs, :] + jnp.dot(v_t, p.astype(BF16), preferred_element_type=F32)
            m_ref[h:h + 1, :] = m_new

    attend(i, slice(MOBA_BLOCK, 2 * MOBA_BLOCK), True)
    attend(prev, slice(0, MOBA_BLOCK), False)

    def far_block(j, carry):
        attend(j, None, False)
        return carry

    lax.fori_loop(0, prev, far_block, 0)
    for h in range(N_HEADS):
        hs = slice(h * HEAD_DIM, (h + 1) * HEAD_DIM)
        acc_ref[hs, :] = acc_ref[hs, :] / l_ref[h:h + 1, :]
    o_ref[...] = acc_ref[...].T


def _prompt_attention(rel_bias, q, kb4, vtb4, means, tables):
    b, nb = kb4.shape[:2]
    t = nb * MOBA_BLOCK
    return pl.pallas_call(
        _prompt_attn_kernel,
        grid=(b, nb),
        in_specs=[pl.BlockSpec(memory_space=pltpu.SMEM),
                  pl.BlockSpec((None, MOBA_BLOCK, D_ATTN), lambda bi, i: (bi, i, 0)),
                  pl.BlockSpec((None, nb, MOBA_BLOCK, D_ATTN), lambda bi, i: (bi, 0, 0, 0)),
                  pl.BlockSpec((None, nb, D_ATTN, MOBA_BLOCK), lambda bi, i: (bi, 0, 0, 0)),
                  pl.BlockSpec((None, nb, D_ATTN), lambda bi, i: (bi, 0, 0)),
                  _resident(tables.shape)],
        out_specs=pl.BlockSpec((None, MOBA_BLOCK, D_ATTN), lambda bi, i: (bi, i, 0)),
        out_shape=jax.ShapeDtypeStruct((b, t, D_ATTN), F32),
        scratch_shapes=[pltpu.VMEM((N_HEADS * nb, D_ATTN), F32),
                        pltpu.VMEM((N_HEADS * nb, MOBA_BLOCK), F32),
                        pltpu.VMEM((N_HEADS, 2 * HEAD_DIM, MOBA_BLOCK), BF16),
                        pltpu.VMEM((N_HEADS, MOBA_BLOCK), F32),
                        pltpu.VMEM((N_HEADS, MOBA_BLOCK), F32),
                        pltpu.VMEM((D_ATTN, MOBA_BLOCK), F32)],
        compiler_params=_params("parallel", "arbitrary"),
        name="prompt_attention",
    )(rel_bias, q, kb4, vtb4, means, tables)


def _sample_attn_kernel(n_pages, pt_ref, q_ref, kn_ref, vn_ref, bias_ref, *refs):
    k_pages = refs[:n_pages]
    v_pages = refs[n_pages:2 * n_pages]
    o_ref = refs[2 * n_pages]
    knp_ref, vnp_ref = refs[2 * n_pages + 1:]
    del pt_ref
    t_new, seqs_per_block, _ = q_ref.shape
    page = k_pages[0].shape[1]
    pages_per_block = MOBA_BLOCK // page
    nb = n_pages // pages_per_block
    past_len = n_pages * page
    row = pl.ds(pl.program_id(0) % seqs_per_block, 1)

    head_of_row = lax.broadcasted_iota(jnp.int32, (N_HEADS, D_ATTN), 0)
    head_of_lane = lax.broadcasted_iota(jnp.int32, (N_HEADS, D_ATTN), 1) // HEAD_DIM
    head_mask = head_of_row == head_of_lane
    qrows = jnp.concatenate(
        [jnp.where(head_mask, jnp.broadcast_to(q_ref[t, row, :], (N_HEADS, D_ATTN)), 0.0) for t in range(t_new)],
        axis=0)

    lane = lax.broadcasted_iota(jnp.int32, (D_ATTN, page), 1)
    means_t = jnp.zeros((D_ATTN, page), F32)
    for n in range(nb):
        total = sum(k_pages[n * pages_per_block + r][...] for r in range(pages_per_block))
        means_t = jnp.where(lane == n, jnp.sum(total, axis=1, keepdims=True) * (1.0 / MOBA_BLOCK), means_t)
    scores = jnp.dot(qrows, means_t, precision=HIGHEST, preferred_element_type=F32)[:, :nb]
    selected = jnp.where(_select_rows(scores, nb, axis=1), 1.0, 0.0)

    qb = (qrows * (HEAD_DIM ** -0.5)).astype(BF16)
    s_pages = []
    for p in range(n_pages):
        s = jnp.dot(qb, k_pages[p][...].astype(BF16), preferred_element_type=F32)
        keep = selected[:, p // pages_per_block:p // pages_per_block + 1] > 0.5
        s_pages.append(jnp.where(keep, s + bias_ref[:, p * page:(p + 1) * page], NEG_INF))
    s_past = jnp.concatenate(s_pages, axis=1)

    knp_ref[...] = jnp.zeros(knp_ref.shape, F32)
    vnp_ref[...] = jnp.zeros(vnp_ref.shape, F32)
    for t in range(t_new):
        knp_ref[t:t + 1, :] = kn_ref[t, row, :]
        vnp_ref[t:t + 1, :] = vn_ref[t, row, :]
    nt = (((1,), (1,)), ((), ()))
    s_own = (lax.dot_general(qb, knp_ref[...].astype(BF16), nt, preferred_element_type=F32)
             + bias_ref[:, past_len:])

    m = jnp.maximum(jnp.max(s_past, axis=1, keepdims=True), jnp.max(s_own, axis=1, keepdims=True))
    p_past = jnp.exp(s_past - m)
    p_own = jnp.exp(s_own - m)
    l = jnp.sum(p_past, axis=1, keepdims=True) + jnp.sum(p_own, axis=1, keepdims=True)
    o = jnp.dot(p_own.astype(BF16), vnp_ref[...].astype(BF16), preferred_element_type=F32)
    for p in range(n_pages):
        o = o + lax.dot_general(p_past[:, p * page:(p + 1) * page].astype(BF16), v_pages[p][...].astype(BF16), nt,
                                preferred_element_type=F32)
    o = o / l
    for t in range(t_new):
        o_ref[t, row, :] = jnp.sum(jnp.where(head_mask, o[t * N_HEADS:(t + 1) * N_HEADS, :], 0.0),
                                   axis=0, keepdims=True)


def _sample_attention(page_table, q, k_new, v_new, bias, cache_kt, cache_vt):
    t_new, bs, _ = q.shape
    n_pages = page_table.shape[1]
    page = cache_kt.shape[2]
    own_width = bias.shape[1] - n_pages * page
    seqs_per_block = SUBLANES
    assert bs % seqs_per_block == 0
    per_seq = pl.BlockSpec((t_new, seqs_per_block, D_ATTN), lambda s, pt: (0, s // seqs_per_block, 0))
    page_spec = lambda p: pl.BlockSpec((None, D_ATTN, page), lambda s, pt: (pt[s, p], 0, 0))
    grid_spec = pltpu.PrefetchScalarGridSpec(
        num_scalar_prefetch=1,
        grid=(bs,),
        in_specs=[per_seq, per_seq, per_seq, pl.BlockSpec(bias.shape, lambda s, pt: (0, 0))]
                 + [page_spec(p) for p in range(n_pages)] * 2,
        out_specs=per_seq,
        scratch_shapes=[pltpu.VMEM((own_width, D_ATTN), F32), pltpu.VMEM((own_width, D_ATTN), F32)],
    )
    return pl.pallas_call(
        functools.partial(_sample_attn_kernel, n_pages),
        grid_spec=grid_spec,
        out_shape=jax.ShapeDtypeStruct((t_new, bs, D_ATTN), F32),
        compiler_params=_params("arbitrary"),
        name="sample_attention",
    )(page_table, q, k_new, v_new, bias, *([cache_kt] * n_pages), *([cache_vt] * n_pages))


def _rglru_coeffs(xc, wr_ref, br_ref, wi_ref, bi_ref, lam_ref):
    xb = xc.astype(BF16)
    r = jax.nn.sigmoid(jnp.dot(xb, wr_ref[...], preferred_element_type=F32) + br_ref[...])
    i = jax.nn.sigmoid(jnp.dot(xb, wi_ref[...], preferred_element_type=F32) + bi_ref[...])
    log_a = -RG_C * r * jax.nn.softplus(-lam_ref[...])
    a = jnp.exp(log_a)
    b = jnp.sqrt(-jnp.tanh(log_a) * (a * a + 1.0)) * (i * xc)
    return a, b


def _prompt_rnn_kernel(xr_ref, gg_ref, cw_ref, cb_ref, wr_ref, br_ref, wi_ref, bi_ref, lam_ref,
                       o_ref, hl_ref, xbuf_ref, h_ref):
    tt = xr_ref.shape[0]
    halo = SUBLANES

    @pl.when(pl.program_id(1) == 0)
    def _():
        xbuf_ref[0:halo, :] = jnp.zeros((halo, D_RNN), F32)
        h_ref[...] = jnp.zeros(h_ref.shape, F32)

    xbuf_ref[halo:, :] = xr_ref[...]
    xc = cb_ref[...] + sum(xbuf_ref[halo - (CONV_W - 1 - j):halo - (CONV_W - 1 - j) + tt, :] * cw_ref[j:j + 1, :]
                           for j in range(CONV_W))
    xbuf_ref[0:halo, :] = xbuf_ref[tt:tt + halo, :]
    a, b = _rglru_coeffs(xc, wr_ref, br_ref, wi_ref, bi_ref, lam_ref)

    row = lax.broadcasted_iota(jnp.int32, (tt, D_RNN), 0) % SUBLANES
    shift = 1
    while shift < SUBLANES:
        a_prev = pltpu.roll(a, shift, axis=0)
        b_prev = pltpu.roll(b, shift, axis=0)
        ok = row >= shift
        b = jnp.where(ok, a * b_prev + b, b)
        a = jnp.where(ok, a * a_prev, a)
        shift *= 2
    h = h_ref[...]
    for g in range(tt // SUBLANES):
        sl = slice(g * SUBLANES, (g + 1) * SUBLANES)
        hg = a[sl, :] * h + b[sl, :]
        o_ref[sl, :] = hg * gg_ref[sl, :]
        h = hg[SUBLANES - 1:SUBLANES, :]
    h_ref[...] = h
    hl_ref[...] = h


def _prompt_rnn(xr, gg, weights, tt):
    b, t, _ = xr.shape
    assert t % tt == 0 and tt % SUBLANES == 0
    tile = pl.BlockSpec((None, tt, D_RNN), lambda bi, i: (bi, i, 0))
    return pl.pallas_call(
        _prompt_rnn_kernel,
        grid=(b, t // tt),
        in_specs=[tile, tile] + [_resident(w.shape) for w in weights],
        out_specs=[tile, pl.BlockSpec((None, 1, D_RNN), lambda bi, i: (bi, 0, 0))],
        out_shape=[jax.ShapeDtypeStruct((b, t, D_RNN), F32), jax.ShapeDtypeStruct((b, 1, D_RNN), F32)],
        scratch_shapes=[pltpu.VMEM((tt + SUBLANES, D_RNN), F32), pltpu.VMEM((1, D_RNN), F32)],
        compiler_params=_params("parallel", "arbitrary"),
        name="prompt_rnn",
    )(xr, gg, *weights)


def _sample_rnn_kernel(xr_ref, gg_ref, conv_ref, h0_ref, cw_ref, cb_ref, wr_ref, br_ref, wi_ref, bi_ref, lam_ref,
                       o_ref, hl_ref):
    t_new = xr_ref.shape[0]
    past = [conv_ref[j] for j in range(CONV_W - 1)] + [xr_ref[t] for t in range(t_new)]
    h = h0_ref[...]
    for t in range(t_new):
        xc = cb_ref[...] + sum(past[t + j] * cw_ref[j:j + 1, :] for j in range(CONV_W))
        a, b = _rglru_coeffs(xc, wr_ref, br_ref, wi_ref, bi_ref, lam_ref)
        h = a * h + b
        o_ref[t] = h * gg_ref[t]
    hl_ref[...] = h


def _sample_rnn(xr_t, gg_t, conv_t, h0, weights):
    return pl.pallas_call(
        _sample_rnn_kernel,
        out_shape=[jax.ShapeDtypeStruct(xr_t.shape, F32), jax.ShapeDtypeStruct(h0.shape, F32)],
        compiler_params=pltpu.CompilerParams(vmem_limit_bytes=VMEM_LIMIT),
        name="sample_rnn",
    )(xr_t, gg_t, conv_t, h0, *weights)


def _merge_kernel(x_ref, oa_ref, orn_ref, sa_ref, sr_ref, wpa_ref, wpr_ref, wo_ref, g_ref, y_ref):
    pa = jnp.dot(oa_ref[...].astype(BF16), wpa_ref[...], preferred_element_type=F32)
    pr = jnp.dot(orn_ref[...].astype(BF16), wpr_ref[...], preferred_element_type=F32)
    merged = sa_ref[...] * pa + sr_ref[...] * pr
    out = jnp.dot(merged.astype(BF16), wo_ref[...], preferred_element_type=F32)
    y_ref[...] = x_ref[...] + _rms(out, g_ref[...])


def _merge(x2d, oa, orn, sa, sr, wpa, wpr, wo, g, tm):
    n, d_model = x2d.shape
    row = lambda width: pl.BlockSpec((tm, width), lambda i: (i, 0))
    return pl.pallas_call(
        _merge_kernel,
        grid=(n // tm,),
        in_specs=[row(d_model), row(D_ATTN), row(D_RNN), row(d_model), row(d_model),
                  _resident(wpa.shape), _resident(wpr.shape), _resident(wo.shape), _resident(g.shape)],
        out_specs=row(d_model),
        out_shape=jax.ShapeDtypeStruct((n, d_model), F32),
        compiler_params=_params("parallel"),
        name="merge",
    )(x2d, oa, orn, sa, sr, wpa, wpr, wo, g)


def _ffn_kernel(chunk, x_ref, gpre_ref, wgu_ref, wd_ref, gpost_ref, y_ref):
    d_ff = wd_ref.shape[0]
    x = x_ref[...]
    fb = _rms(x, gpre_ref[...]).astype(BF16)
    y = jnp.zeros(x.shape, F32)
    for c in range(0, d_ff, chunk):
        gate = jnp.dot(fb, wgu_ref[:, c:c + chunk], preferred_element_type=F32)
        up = jnp.dot(fb, wgu_ref[:, d_ff + c:d_ff + c + chunk], preferred_element_type=F32)
        act = (jax.nn.silu(gate) * up).astype(BF16)
        y = y + jnp.dot(act, wd_ref[c:c + chunk, :], preferred_element_type=F32)
    y_ref[...] = x + _rms(y, gpost_ref[...])


def _ffn(x2d, gpre, wgu, wd, gpost, tm, chunk):
    n, d_model = x2d.shape
    assert wd.shape[0] % chunk == 0
    row = pl.BlockSpec((tm, d_model), lambda i: (i, 0))
    return pl.pallas_call(
        functools.partial(_ffn_kernel, chunk),
        grid=(n // tm,),
        in_specs=[row, _resident(gpre.shape), _resident(wgu.shape), _resident(wd.shape), _resident(gpost.shape)],
        out_specs=row,
        out_shape=jax.ShapeDtypeStruct((n, d_model), F32),
        compiler_params=_params("parallel"),
        name="ffn",
    )(x2d, gpre, wgu, wd, gpost)


def _block_diag(w):
    nblk, c, _ = w.shape
    eye = jnp.eye(nblk, dtype=w.dtype)
    return (w[:, :, None, :] * eye[:, None, :, None]).reshape(nblk * c, nblk * c)


def _rows_from_feature_major(xt):
    g, _, w = xt.shape
    return jnp.transpose(xt.reshape(g, N_HEADS, HEAD_DIM, w), (3, 0, 1, 2))[None]


def kernel(x_prompt, x_sample, cache_k, cache_v, page_table, state_conv, state_h, norm_mix_pre, norm_mix_post,
           w_in, rel_bias, conv_w, conv_b, rg_w_r, rg_b_r, rg_w_i, rg_b_i, rg_lambda, w_proj_attn, w_proj_rnn,
           w_out, norm_ffn_pre, norm_ffn_post, w_gate_up, w_down):
    depth = w_in.shape[0]
    assert depth == 1
    bp, tp, d_model = x_prompt.shape
    bs, ts, _ = x_sample.shape
    n_phys, page = cache_k.shape[1:3]
    n_pages = page_table.shape[1]
    past_len = n_pages * page
    assert tp % MOBA_BLOCK == 0 and past_len % MOBA_BLOCK == 0 and MOBA_BLOCK % page == 0
    assert past_len // MOBA_BLOCK > MOBA_TOP_K and CONV_W - 1 <= ts <= MOBA_BLOCK
    nbp = tp // MOBA_BLOCK
    l = 0
    row2 = lambda v: v.reshape(1, -1)

    w_in_b = w_in[l].astype(BF16)
    rnn_w = (conv_w[l], row2(conv_b[l]), _block_diag(rg_w_r[l]).astype(BF16), row2(rg_b_r[l]),
             _block_diag(rg_w_i[l]).astype(BF16), row2(rg_b_i[l]), row2(rg_lambda[l]))
    wpa, wpr, wo = w_proj_attn[l].astype(BF16), w_proj_rnn[l].astype(BF16), w_out[l].astype(BF16)
    wgu, wd = w_gate_up[l].astype(BF16), w_down[l].astype(BF16)
    g_pre, g_post = row2(norm_mix_pre[l]), row2(norm_mix_post[l])
    gf_pre, gf_post = row2(norm_ffn_pre[l]), row2(norm_ffn_post[l])

    def tail(x2d, oa, orn, sa, sr, tm):
        x1 = _merge(x2d, oa, orn, sa, sr, wpa, wpr, wo, g_post, tm)
        return _ffn(x1, gf_pre, wgu, wd, gf_post, tm, 256)

    tm = 512
    xp2 = x_prompt.reshape(bp * tp, d_model)
    q, kb, vtb, means, kt, vt, xr, gg, sa, sr = _in_proj(xp2, g_pre, w_in_b, tm, tp, False)
    tables = _prompt_bias_tables(rel_bias)
    oa = _prompt_attention(rel_bias, q.reshape(bp, tp, D_ATTN), kb.reshape(bp, nbp, MOBA_BLOCK, D_ATTN),
                           vtb.reshape(bp, nbp, D_ATTN, MOBA_BLOCK), means.reshape(bp, nbp, D_ATTN), tables)
    xr3 = xr.reshape(bp, tp, D_RNN)
    orn, h_last = _prompt_rnn(xr3, gg.reshape(bp, tp, D_RNN), rnn_w, 256)
    y_prompt = tail(xp2, oa.reshape(bp * tp, D_ATTN), orn.reshape(bp * tp, D_RNN), sa, sr, tm).reshape(bp, tp, d_model)
    new_k_prompt = jnp.swapaxes(_rows_from_feature_major(kt), 1, 2)
    new_v_prompt = jnp.swapaxes(_rows_from_feature_major(vt), 1, 2)
    new_conv_prompt = xr3[:, tp - (CONV_W - 1):, :][None]
    new_h_prompt = h_last.reshape(1, bp, D_RNN).astype(state_h.dtype)

    ns = bs * ts
    xs2 = jnp.swapaxes(x_sample, 0, 1).reshape(ns, d_model)
    qs, ks, vs, kts, vts, xrs, ggs, sas, srs = _in_proj(xs2, g_pre, w_in_b, ns, bs, True)
    own_width = 128
    bias_s = _sample_bias_table(rel_bias, past_len, ts, own_width)
    feature_major_pages = lambda c: jnp.transpose(c, (0, 2, 3, 1)).reshape(n_phys, D_ATTN, page)
    tm3 = lambda a: a.reshape(ts, bs, a.shape[-1])
    oas = _sample_attention(page_table, tm3(qs), tm3(ks), tm3(vs), bias_s,
                            feature_major_pages(cache_k[l]), feature_major_pages(cache_v[l]))
    xrs3 = tm3(xrs)
    orns, hs_last = _sample_rnn(xrs3, tm3(ggs), jnp.swapaxes(state_conv[l], 0, 1), state_h[l], rnn_w)
    y_sample = tail(xs2, oas.reshape(ns, D_ATTN), orns.reshape(ns, D_RNN), sas, srs, ns)
    y_sample = jnp.swapaxes(y_sample.reshape(ts, bs, d_model), 0, 1)
    new_k_sample = _rows_from_feature_major(kts)
    new_v_sample = _rows_from_feature_major(vts)
    new_conv_sample = jnp.swapaxes(xrs3[ts - (CONV_W - 1):], 0, 1)[None]
    new_h_sample = hs_last[None].astype(state_h.dtype)

    return (y_prompt, y_sample, new_k_prompt, new_v_prompt, new_k_sample, new_v_sample,
            new_conv_prompt, new_h_prompt, new_conv_sample, new_h_sample)
```

```python
import functools
import math

import numpy as np
import jax
import jax.numpy as jnp
from jax import lax
from jax.experimental import pallas as pl
from jax.experimental.pallas import tpu as pltpu

N_HEADS = 8
HEAD_DIM = 64
D_ATTN = N_HEADS * HEAD_DIM
D_RNN = 512
N_RG_BLOCKS = 8
CONV_W = 4
RG_C = 8.0
MOBA_BLOCK = 256
MOBA_TOP_K = 3
NUM_BUCKETS = 32
MAX_DISTANCE = 128
RMS_EPS = 1e-6
NEG_INF = -1e30

SUBLANES = 8
VMEM_LIMIT = 56 * 1024 * 1024
F32 = jnp.float32
BF16 = jnp.bfloat16
HIGHEST = lax.Precision.HIGHEST


def _params(*sem):
    return pltpu.CompilerParams(dimension_semantics=sem, vmem_limit_bytes=VMEM_LIMIT)


def _resident(shape):
    nd = len(shape)
    return pl.BlockSpec(shape, lambda *_: (0,) * nd, pipeline_mode=pl.Buffered(1))


def _rms(x, g):
    return x * lax.rsqrt(jnp.mean(x * x, axis=-1, keepdims=True) + RMS_EPS) * g


def _t5_bucket_np(n):
    n = np.maximum(n, 0)
    max_exact = NUM_BUCKETS // 2
    nf = np.maximum(n, 1).astype(np.float32)
    large = max_exact + (np.log(nf / np.float32(max_exact)) / np.float32(math.log(MAX_DISTANCE / max_exact))
                         * np.float32(NUM_BUCKETS - max_exact)).astype(np.int32)
    large = np.minimum(large, NUM_BUCKETS - 1)
    return np.where(n < max_exact, n, large).astype(np.int32)


def _in_proj_kernel(natural_kv, x_ref, g_ref, w_ref, *out_refs):
    if natural_kv:
        q_ref, k_ref, v_ref, kt_ref, vt_ref, xr_ref, gg_ref = out_refs
    else:
        q_ref, kb_ref, vtb_ref, means_ref, kt_ref, vt_ref, xr_ref, gg_ref = out_refs
    hb = _rms(x_ref[...], g_ref[...]).astype(BF16)

    def proj(lo, width):
        return jnp.dot(hb, w_ref[:, lo:lo + width], preferred_element_type=F32)

    q_ref[...] = proj(0, D_ATTN)
    k = proj(D_ATTN, D_ATTN)
    v = proj(2 * D_ATTN, D_ATTN)
    groups, _, width = kt_ref.shape
    for r in range(groups):
        kt_ref[r] = k[r * width:(r + 1) * width, :].T
        vt_ref[r] = v[r * width:(r + 1) * width, :].T
    if natural_kv:
        k_ref[...] = k
        v_ref[...] = v
    else:
        kb_ref[...] = k.astype(BF16)
        for r in range(vtb_ref.shape[0]):
            blk = slice(r * MOBA_BLOCK, (r + 1) * MOBA_BLOCK)
            vtb_ref[r] = v[blk, :].T.astype(BF16)
            means_ref[r] = jnp.sum(k[blk, :], axis=0, keepdims=True) * (1.0 / MOBA_BLOCK)
    xr_ref[...] = proj(3 * D_ATTN, D_RNN)
    gg_ref[...] = jax.nn.gelu(proj(3 * D_ATTN + D_RNN, D_RNN))


def _in_proj(x2d, g, w_in_b, tm, group_len, natural_kv):
    n, d_model = x2d.shape
    assert n % tm == 0 and n % group_len == 0 and (tm % group_len == 0 or group_len % tm == 0)
    tiles_per_group = max(group_len // tm, 1)
    t_block = (max(tm // group_len, 1), D_ATTN, min(tm, group_len))
    t_spec = pl.BlockSpec(t_block, lambda i: (i // tiles_per_group, 0, i % tiles_per_group))
    t_shape = jax.ShapeDtypeStruct((n // group_len, D_ATTN, group_len), F32)
    row = lambda width: pl.BlockSpec((tm, width), lambda i: (i, 0))
    f32 = lambda width: jax.ShapeDtypeStruct((n, width), F32)
    if natural_kv:
        kv_specs = [row(D_ATTN), row(D_ATTN)]
        kv_shapes = [f32(D_ATTN), f32(D_ATTN)]
    else:
        assert tm % MOBA_BLOCK == 0
        blocks = tm // MOBA_BLOCK
        kv_specs = [row(D_ATTN), pl.BlockSpec((blocks, D_ATTN, MOBA_BLOCK), lambda i: (i, 0, 0)),
                    pl.BlockSpec((blocks, 1, D_ATTN), lambda i: (i, 0, 0))]
        kv_shapes = [jax.ShapeDtypeStruct((n, D_ATTN), BF16),
                     jax.ShapeDtypeStruct((n // MOBA_BLOCK, D_ATTN, MOBA_BLOCK), BF16),
                     jax.ShapeDtypeStruct((n // MOBA_BLOCK, 1, D_ATTN), F32)]
    return pl.pallas_call(
        functools.partial(_in_proj_kernel, natural_kv),
        grid=(n // tm,),
        in_specs=[row(d_model), _resident((1, d_model)), _resident(w_in_b.shape)],
        out_specs=[row(D_ATTN)] + kv_specs + [t_spec, t_spec, row(D_RNN), row(D_RNN)],
        out_shape=[f32(D_ATTN)] + kv_shapes + [t_shape, t_shape, f32(D_RNN), f32(D_RNN)],
        compiler_params=_params("parallel"),
        name="in_proj",
    )(x2d, g, w_in_b)


def _bias_table_kernel(rb_ref, bucket_ref, out_ref):
    bucket = bucket_ref[...]
    for h in range(N_HEADS):
        acc = jnp.full(bucket.shape, NEG_INF, F32)
        for b in range(NUM_BUCKETS):
            acc = jnp.where(bucket == b, rb_ref[b, h], acc)
        out_ref[h] = acc


def _prompt_bias_tables(rel_bias):
    kr = np.arange(2 * MOBA_BLOCK)[:, None]
    qr = np.arange(MOBA_BLOCK)[None, :]
    dist = qr + MOBA_BLOCK - kr
    bucket = np.where(dist >= 0, _t5_bucket_np(dist), -1).astype(np.int32)
    return pl.pallas_call(
        _bias_table_kernel,
        in_specs=[pl.BlockSpec(memory_space=pltpu.SMEM), pl.BlockSpec(memory_space=pltpu.VMEM)],
        out_specs=pl.BlockSpec(memory_space=pltpu.VMEM),
        out_shape=jax.ShapeDtypeStruct((N_HEADS,) + bucket.shape, F32),
        name="prompt_bias_tables",
    )(rel_bias, jnp.asarray(bucket))


def _sample_bias_kernel(rbt_ref, bucket_ref, out_ref):
    bucket = bucket_ref[...]
    acc = jnp.full(bucket.shape, NEG_INF, F32)
    for b in range(NUM_BUCKETS):
        acc = jnp.where(bucket == b, rbt_ref[:, b:b + 1], acc)
    out_ref[...] = acc


def _sample_bias_table(rel_bias, past_len, t_new, own_width):
    t = np.repeat(np.arange(t_new), N_HEADS)[:, None]
    kpos = np.arange(past_len + own_width)[None, :]
    dist = past_len + t - kpos
    valid = (dist >= 0) & (kpos < past_len + t_new)
    bucket = np.where(valid, _t5_bucket_np(dist), -1).astype(np.int32)
    rbt = jnp.tile(rel_bias.T, (t_new, 1))
    return pl.pallas_call(
        _sample_bias_kernel,
        out_shape=jax.ShapeDtypeStruct(bucket.shape, F32),
        name="sample_bias_table",
    )(rbt, jnp.asarray(bucket))


def _select_rows(scores, n_valid, axis):
    nb = scores.shape[axis]
    idx = lax.broadcasted_iota(jnp.int32, scores.shape, axis)
    beaten = jnp.zeros(scores.shape, jnp.int32)
    for n in range(nb):
        other = lax.slice_in_dim(scores, n, n + 1, axis=axis)
        beats = (other > scores) | ((other == scores) & (n < idx))
        beaten = beaten + jnp.where(beats, jnp.where(n < n_valid, 1, 0), 0)
    return (beaten < MOBA_TOP_K) & (idx < n_valid)


def _prompt_attn_kernel(rb_ref, q_ref, kb_ref, vtb_ref, means_ref, tab_ref, o_ref,
                        mb_ref, sel_ref, qm_ref, m_ref, l_ref, acc_ref):
    i = pl.program_id(1)
    nb = kb_ref.shape[0]
    prev = jnp.maximum(i - 1, 0)
    pair_rows = 2 * HEAD_DIM
    qt = q_ref[...].T

    means = means_ref[...]
    head_of_lane = lax.broadcasted_iota(jnp.int32, means.shape, 1) // HEAD_DIM
    for h in range(N_HEADS):
        mb_ref[h * nb:(h + 1) * nb, :] = jnp.where(head_of_lane == h, means, 0.0)
    sct = jnp.dot(mb_ref[...], qt, precision=HIGHEST, preferred_element_type=F32)

    row_in_pair = lax.broadcasted_iota(jnp.int32, (pair_rows, MOBA_BLOCK), 0) // HEAD_DIM
    block_id = lax.broadcasted_iota(jnp.int32, (nb, MOBA_BLOCK), 0)
    for h in range(N_HEADS):
        far_bias = rb_ref[NUM_BUCKETS - 1, h]
        selected = _select_rows(sct[h * nb:(h + 1) * nb, :], i, axis=0)
        sel_ref[h * nb:(h + 1) * nb, :] = jnp.where(
            selected, jnp.where(block_id == i - 1, 0.0, far_bias), NEG_INF)
        q_pair = qt[(h // 2) * pair_rows:(h // 2 + 1) * pair_rows, :] * (HEAD_DIM ** -0.5)
        qm_ref[h] = jnp.where(row_in_pair == h % 2, q_pair, 0.0).astype(BF16)

    def scores(item):
        j, table_rows, first, h = item
        pair = h // 2
        k_pair = kb_ref[j, :, pair * pair_rows:(pair + 1) * pair_rows]
        s = jnp.dot(k_pair, qm_ref[h], preferred_element_type=F32)
        if table_rows is not None:
            s = s + tab_ref[h, table_rows, :]
        if not first:
            s = s + sel_ref[pl.ds(h * nb + j, 1), :]
        return s

    def softmax(item, s):
        _, _, first, h = item
        if first:
            m_new = jnp.max(s, axis=0, keepdims=True)
            p = jnp.exp(s - m_new)
            l_ref[h:h + 1, :] = jnp.sum(p, axis=0, keepdims=True)
            alpha = None
        else:
            m_old = m_ref[h:h + 1, :]
            m_new = jnp.maximum(m_old, jnp.max(s, axis=0, keepdims=True))
            alpha = jnp.exp(m_old - m_new)
            p = jnp.exp(s - m_new)
            l_ref[h:h + 1, :] = alpha * l_ref[h:h + 1, :] + jnp.sum(p, axis=0, keepdims=True)
        m_ref[h:h + 1, :] = m_new
        return p.astype(BF16), alpha

    def weighted_values(item, p, alpha):
        j, _, first, h = item
        hs = slice(h * HEAD_DIM, (h + 1) * HEAD_DIM)
        pv = jnp.dot(vtb_ref[j, hs, :], p, preferred_element_type=F32)
        acc_ref[hs, :] = pv if first else alpha * acc_ref[hs, :] + pv

    def run(items, ahead=2):
        pending = {n: scores(items[n]) for n in range(min(ahead, len(items)))}
        for n, item in enumerate(items):
            p, alpha = softmax(item, pending.pop(n))
            if n + ahead < len(items):
                pending[n + ahead] = scores(items[n + ahead])
            weighted_values(item, p, alpha)

    run([(i, slice(MOBA_BLOCK, 2 * MOBA_BLOCK), True, h) for h in range(N_HEADS)]
        + [(prev, slice(0, MOBA_BLOCK), False, h) for h in range(N_HEADS)])

    def far_block(j, carry):
        run([(j, None, False, h) for h in range(N_HEADS)])
        return carry

    lax.fori_loop(0, prev, far_block, 0)
    for h in range(N_HEADS):
        hs = slice(h * HEAD_DIM, (h + 1) * HEAD_DIM)
        acc_ref[hs, :] = acc_ref[hs, :] / l_ref[h:h + 1, :]
    o_ref[...] = acc_ref[...].T.astype(o_ref.dtype)


def _prompt_attention(rel_bias, q, kb4, vtb4, means, tables):
    b, nb = kb4.shape[:2]
    t = nb * MOBA_BLOCK
    return pl.pallas_call(
        _prompt_attn_kernel,
        grid=(b, nb),
        in_specs=[pl.BlockSpec(memory_space=pltpu.SMEM),
                  pl.BlockSpec((None, MOBA_BLOCK, D_ATTN), lambda bi, i: (bi, i, 0)),
                  pl.BlockSpec((None, nb, MOBA_BLOCK, D_ATTN), lambda bi, i: (bi, 0, 0, 0)),
                  pl.BlockSpec((None, nb, D_ATTN, MOBA_BLOCK), lambda bi, i: (bi, 0, 0, 0)),
                  pl.BlockSpec((None, nb, D_ATTN), lambda bi, i: (bi, 0, 0)),
                  _resident(tables.shape)],
        out_specs=pl.BlockSpec((None, MOBA_BLOCK, D_ATTN), lambda bi, i: (bi, i, 0)),
        out_shape=jax.ShapeDtypeStruct((b, t, D_ATTN), BF16),
        scratch_shapes=[pltpu.VMEM((N_HEADS * nb, D_ATTN), F32),
                        pltpu.VMEM((N_HEADS * nb, MOBA_BLOCK), F32),
                        pltpu.VMEM((N_HEADS, 2 * HEAD_DIM, MOBA_BLOCK), BF16),
                        pltpu.VMEM((N_HEADS, MOBA_BLOCK), F32),
                        pltpu.VMEM((N_HEADS, MOBA_BLOCK), F32),
                        pltpu.VMEM((D_ATTN, MOBA_BLOCK), F32)],
        compiler_params=_params("parallel", "arbitrary"),
        name="prompt_attention",
    )(rel_bias, q, kb4, vtb4, means, tables)


def _sample_attn_kernel(n_pages, pt_ref, q_ref, kn_ref, vn_ref, bias_ref, *refs):
    k_pages = refs[:n_pages]
    v_pages = refs[n_pages:2 * n_pages]
    o_ref = refs[2 * n_pages]
    knp_ref, vnp_ref = refs[2 * n_pages + 1:]
    del pt_ref
    t_new, seqs_per_block, _ = q_ref.shape
    page = k_pages[0].shape[1]
    pages_per_block = MOBA_BLOCK // page
    nb = n_pages // pages_per_block
    past_len = n_pages * page
    row = pl.ds(pl.program_id(0) % seqs_per_block, 1)

    head_of_row = lax.broadcasted_iota(jnp.int32, (N_HEADS, D_ATTN), 0)
    head_of_lane = lax.broadcasted_iota(jnp.int32, (N_HEADS, D_ATTN), 1) // HEAD_DIM
    head_mask = head_of_row == head_of_lane
    qrows = jnp.concatenate(
        [jnp.where(head_mask, jnp.broadcast_to(q_ref[t, row, :], (N_HEADS, D_ATTN)), 0.0) for t in range(t_new)],
        axis=0)

    lane = lax.broadcasted_iota(jnp.int32, (D_ATTN, page), 1)
    means_t = jnp.zeros((D_ATTN, page), F32)
    for n in range(nb):
        total = sum(k_pages[n * pages_per_block + r][...] for r in range(pages_per_block))
        means_t = jnp.where(lane == n, jnp.sum(total, axis=1, keepdims=True) * (1.0 / MOBA_BLOCK), means_t)
    scores = jnp.dot(qrows, means_t, precision=HIGHEST, preferred_element_type=F32)[:, :nb]
    selected = jnp.where(_select_rows(scores, nb, axis=1), 1.0, 0.0)

    qb = (qrows * (HEAD_DIM ** -0.5)).astype(BF16)
    s_pages = []
    for p in range(n_pages):
        s = jnp.dot(qb, k_pages[p][...].astype(BF16), preferred_element_type=F32)
        keep = selected[:, p // pages_per_block:p // pages_per_block + 1] > 0.5
        s_pages.append(jnp.where(keep, s + bias_ref[:, p * page:(p + 1) * page], NEG_INF))
    s_past = jnp.concatenate(s_pages, axis=1)

    knp_ref[...] = jnp.zeros(knp_ref.shape, F32)
    vnp_ref[...] = jnp.zeros(vnp_ref.shape, F32)
    for t in range(t_new):
        knp_ref[t:t + 1, :] = kn_ref[t, row, :]
        vnp_ref[t:t + 1, :] = vn_ref[t, row, :]
    nt = (((1,), (1,)), ((), ()))
    s_own = (lax.dot_general(qb, knp_ref[...].astype(BF16), nt, preferred_element_type=F32)
             + bias_ref[:, past_len:])

    m = jnp.maximum(jnp.max(s_past, axis=1, keepdims=True), jnp.max(s_own, axis=1, keepdims=True))
    p_past = jnp.exp(s_past - m)
    p_own = jnp.exp(s_own - m)
    l = jnp.sum(p_past, axis=1, keepdims=True) + jnp.sum(p_own, axis=1, keepdims=True)
    o = jnp.dot(p_own.astype(BF16), vnp_ref[...].astype(BF16), preferred_element_type=F32)
    for p in range(n_pages):
        o = o + lax.dot_general(p_past[:, p * page:(p + 1) * page].astype(BF16), v_pages[p][...].astype(BF16), nt,
                                preferred_element_type=F32)
    o = o / l
    for t in range(t_new):
        o_ref[t, row, :] = jnp.sum(jnp.where(head_mask, o[t * N_HEADS:(t + 1) * N_HEADS, :], 0.0),
                                   axis=0, keepdims=True)


def _sample_attention(page_table, q, k_new, v_new, bias, cache_kt, cache_vt):
    t_new, bs, _ = q.shape
    n_pages = page_table.shape[1]
    page = cache_kt.shape[2]
    own_width = bias.shape[1] - n_pages * page
    seqs_per_block = SUBLANES
    assert bs % seqs_per_block == 0
    per_seq = pl.BlockSpec((t_new, seqs_per_block, D_ATTN), lambda s, pt: (0, s // seqs_per_block, 0))
    page_spec = lambda p: pl.BlockSpec((None, D_ATTN, page), lambda s, pt: (pt[s, p], 0, 0))
    grid_spec = pltpu.PrefetchScalarGridSpec(
        num_scalar_prefetch=1,
        grid=(bs,),
        in_specs=[per_seq, per_seq, per_seq, pl.BlockSpec(bias.shape, lambda s, pt: (0, 0))]
                 + [page_spec(p) for p in range(n_pages)] * 2,
        out_specs=per_seq,
        scratch_shapes=[pltpu.VMEM((own_width, D_ATTN), F32), pltpu.VMEM((own_width, D_ATTN), F32)],
    )
    return pl.pallas_call(
        functools.partial(_sample_attn_kernel, n_pages),
        grid_spec=grid_spec,
        out_shape=jax.ShapeDtypeStruct((t_new, bs, D_ATTN), F32),
        compiler_params=_params("arbitrary"),
        name="sample_attention",
    )(page_table, q, k_new, v_new, bias, *([cache_kt] * n_pages), *([cache_vt] * n_pages))


def _rglru_coeffs(xc, wr_ref, br_ref, wi_ref, bi_ref, lam_ref):
    xb = xc.astype(BF16)
    r = jax.nn.sigmoid(jnp.dot(xb, wr_ref[...], preferred_element_type=F32) + br_ref[...])
    i = jax.nn.sigmoid(jnp.dot(xb, wi_ref[...], preferred_element_type=F32) + bi_ref[...])
    log_a = -RG_C * r * jax.nn.softplus(-lam_ref[...])
    a = jnp.exp(log_a)
    b = jnp.sqrt(-jnp.tanh(log_a) * (a * a + 1.0)) * (i * xc)
    return a, b


def _prompt_rnn_kernel(xr_ref, gg_ref, cw_ref, cb_ref, wr_ref, br_ref, wi_ref, bi_ref, lam_ref,
                       o_ref, hl_ref, xbuf_ref, h_ref, obuf_ref):
    tt = xr_ref.shape[0]
    halo = SUBLANES

    @pl.when(pl.program_id(1) == 0)
    def _():
        xbuf_ref[0:halo, :] = jnp.zeros((halo, D_RNN), F32)
        h_ref[...] = jnp.zeros(h_ref.shape, F32)

    xbuf_ref[halo:, :] = xr_ref[...]
    xc = cb_ref[...] + sum(xbuf_ref[halo - (CONV_W - 1 - j):halo - (CONV_W - 1 - j) + tt, :] * cw_ref[j:j + 1, :]
                           for j in range(CONV_W))
    xbuf_ref[0:halo, :] = xbuf_ref[tt:tt + halo, :]
    a, b = _rglru_coeffs(xc, wr_ref, br_ref, wi_ref, bi_ref, lam_ref)

    row = lax.broadcasted_iota(jnp.int32, (tt, D_RNN), 0) % SUBLANES
    shift = 1
    while shift < SUBLANES:
        a_prev = pltpu.roll(a, shift, axis=0)
        b_prev = pltpu.roll(b, shift, axis=0)
        ok = row >= shift
        b = jnp.where(ok, a * b_prev + b, b)
        a = jnp.where(ok, a * a_prev, a)
        shift *= 2
    h = h_ref[...]
    for g in range(tt // SUBLANES):
        sl = slice(g * SUBLANES, (g + 1) * SUBLANES)
        hg = a[sl, :] * h + b[sl, :]
        obuf_ref[sl, :] = hg * gg_ref[sl, :]
        h = hg[SUBLANES - 1:SUBLANES, :]
    o_ref[...] = obuf_ref[...].astype(o_ref.dtype)
    h_ref[...] = h
    hl_ref[...] = h


def _prompt_rnn(xr, gg, weights, tt):
    b, t, _ = xr.shape
    assert t % tt == 0 and tt % SUBLANES == 0
    tile = pl.BlockSpec((None, tt, D_RNN), lambda bi, i: (bi, i, 0))
    return pl.pallas_call(
        _prompt_rnn_kernel,
        grid=(b, t // tt),
        in_specs=[tile, tile] + [_resident(w.shape) for w in weights],
        out_specs=[tile, pl.BlockSpec((None, 1, D_RNN), lambda bi, i: (bi, 0, 0))],
        out_shape=[jax.ShapeDtypeStruct((b, t, D_RNN), BF16), jax.ShapeDtypeStruct((b, 1, D_RNN), F32)],
        scratch_shapes=[pltpu.VMEM((tt + SUBLANES, D_RNN), F32), pltpu.VMEM((1, D_RNN), F32),
                        pltpu.VMEM((tt, D_RNN), F32)],
        compiler_params=_params("parallel", "arbitrary"),
        name="prompt_rnn",
    )(xr, gg, *weights)


def _sample_rnn_kernel(xr_ref, gg_ref, conv_ref, h0_ref, cw_ref, cb_ref, wr_ref, br_ref, wi_ref, bi_ref, lam_ref,
                       o_ref, hl_ref):
    t_new = xr_ref.shape[0]
    past = [conv_ref[j] for j in range(CONV_W - 1)] + [xr_ref[t] for t in range(t_new)]
    h = h0_ref[...]
    for t in range(t_new):
        xc = cb_ref[...] + sum(past[t + j] * cw_ref[j:j + 1, :] for j in range(CONV_W))
        a, b = _rglru_coeffs(xc, wr_ref, br_ref, wi_ref, bi_ref, lam_ref)
        h = a * h + b
        o_ref[t] = h * gg_ref[t]
    hl_ref[...] = h


def _sample_rnn(xr_t, gg_t, conv_t, h0, weights):
    return pl.pallas_call(
        _sample_rnn_kernel,
        out_shape=[jax.ShapeDtypeStruct(xr_t.shape, F32), jax.ShapeDtypeStruct(h0.shape, F32)],
        compiler_params=pltpu.CompilerParams(vmem_limit_bytes=VMEM_LIMIT),
        name="sample_rnn",
    )(xr_t, gg_t, conv_t, h0, *weights)


def _tail_kernel(chunk, x_ref, oa_ref, orn_ref, gpre_ref, wg_ref, wpa_ref, wpr_ref, wo_ref, gpost_ref,
                 fpre_ref, wgu_ref, wd_ref, fpost_ref, y_ref):
    d_model = x_ref.shape[-1]
    d_ff = wd_ref.shape[0]
    x = x_ref[...]
    hb = _rms(x, gpre_ref[...]).astype(BF16)
    pa = jnp.dot(oa_ref[...].astype(BF16), wpa_ref[...], preferred_element_type=F32)
    merged = jax.nn.sigmoid(jnp.dot(hb, wg_ref[:, :d_model], preferred_element_type=F32)) * pa
    pr = jnp.dot(orn_ref[...].astype(BF16), wpr_ref[...], preferred_element_type=F32)
    merged = merged + jax.nn.sigmoid(jnp.dot(hb, wg_ref[:, d_model:], preferred_element_type=F32)) * pr
    x1 = x + _rms(jnp.dot(merged.astype(BF16), wo_ref[...], preferred_element_type=F32), gpost_ref[...])

    fb = _rms(x1, fpre_ref[...]).astype(BF16)
    y = jnp.zeros(x.shape, F32)
    for c in range(0, d_ff, chunk):
        gate = jnp.dot(fb, wgu_ref[:, c:c + chunk], preferred_element_type=F32)
        up = jnp.dot(fb, wgu_ref[:, d_ff + c:d_ff + c + chunk], preferred_element_type=F32)
        act = (jax.nn.silu(gate) * up).astype(BF16)
        y = y + jnp.dot(act, wd_ref[c:c + chunk, :], preferred_element_type=F32)
    y_ref[...] = x1 + _rms(y, fpost_ref[...])


def _tail(x2d, oa, orn, weights, tm, chunk):
    n, d_model = x2d.shape
    assert n % tm == 0 and weights[8].shape[0] % chunk == 0
    row = lambda width: pl.BlockSpec((tm, width), lambda i: (i, 0))
    return pl.pallas_call(
        functools.partial(_tail_kernel, chunk),
        grid=(n // tm,),
        in_specs=[row(d_model), row(D_ATTN), row(D_RNN)] + [_resident(w.shape) for w in weights],
        out_specs=row(d_model),
        out_shape=jax.ShapeDtypeStruct((n, d_model), F32),
        compiler_params=_params("parallel"),
        name="tail",
    )(x2d, oa, orn, *weights)


def _block_diag(w):
    nblk, c, _ = w.shape
    eye = jnp.eye(nblk, dtype=w.dtype)
    return (w[:, :, None, :] * eye[:, None, :, None]).reshape(nblk * c, nblk * c)


def _rows_from_feature_major(xt):
    g, _, w = xt.shape
    return jnp.transpose(xt.reshape(g, N_HEADS, HEAD_DIM, w), (3, 0, 1, 2))[None]


def kernel(x_prompt, x_sample, cache_k, cache_v, page_table, state_conv, state_h, norm_mix_pre, norm_mix_post,
           w_in, rel_bias, conv_w, conv_b, rg_w_r, rg_b_r, rg_w_i, rg_b_i, rg_lambda, w_proj_attn, w_proj_rnn,
           w_out, norm_ffn_pre, norm_ffn_post, w_gate_up, w_down):
    depth = w_in.shape[0]
    assert depth == 1
    bp, tp, d_model = x_prompt.shape
    bs, ts, _ = x_sample.shape
    n_phys, page = cache_k.shape[1:3]
    n_pages = page_table.shape[1]
    past_len = n_pages * page
    assert tp % MOBA_BLOCK == 0 and past_len % MOBA_BLOCK == 0 and MOBA_BLOCK % page == 0
    assert past_len // MOBA_BLOCK > MOBA_TOP_K and CONV_W - 1 <= ts <= MOBA_BLOCK
    nbp = tp // MOBA_BLOCK
    l = 0
    row2 = lambda v: v.reshape(1, -1)

    n_stream_cols = 3 * D_ATTN + 2 * D_RNN
    w_in_b = w_in[l, :, :n_stream_cols].astype(BF16)
    w_gates = w_in[l, :, n_stream_cols:].astype(BF16)
    rnn_w = (conv_w[l], row2(conv_b[l]), _block_diag(rg_w_r[l]).astype(BF16), row2(rg_b_r[l]),
             _block_diag(rg_w_i[l]).astype(BF16), row2(rg_b_i[l]), row2(rg_lambda[l]))
    wpa, wpr, wo = w_proj_attn[l].astype(BF16), w_proj_rnn[l].astype(BF16), w_out[l].astype(BF16)
    wgu, wd = w_gate_up[l].astype(BF16), w_down[l].astype(BF16)
    g_pre, g_post = row2(norm_mix_pre[l]), row2(norm_mix_post[l])
    gf_pre, gf_post = row2(norm_ffn_pre[l]), row2(norm_ffn_post[l])

    tail_w = (g_pre, w_gates, wpa, wpr, wo, g_post, gf_pre, wgu, wd, gf_post)

    tm = 512
    xp2 = x_prompt.reshape(bp * tp, d_model)
    q, kb, vtb, means, kt, vt, xr, gg = _in_proj(xp2, g_pre, w_in_b, tm, tp, False)
    tables = _prompt_bias_tables(rel_bias)
    oa = _prompt_attention(rel_bias, q.reshape(bp, tp, D_ATTN), kb.reshape(bp, nbp, MOBA_BLOCK, D_ATTN),
                           vtb.reshape(bp, nbp, D_ATTN, MOBA_BLOCK), means.reshape(bp, nbp, D_ATTN), tables)
    xr3 = xr.reshape(bp, tp, D_RNN)
    orn, h_last = _prompt_rnn(xr3, gg.reshape(bp, tp, D_RNN), rnn_w, 256)
    y_prompt = _tail(xp2, oa.reshape(bp * tp, D_ATTN), orn.reshape(bp * tp, D_RNN), tail_w, tm, 256)
    y_prompt = y_prompt.reshape(bp, tp, d_model)
    new_k_prompt = jnp.swapaxes(_rows_from_feature_major(kt), 1, 2)
    new_v_prompt = jnp.swapaxes(_rows_from_feature_major(vt), 1, 2)
    new_conv_prompt = xr3[:, tp - (CONV_W - 1):, :][None]
    new_h_prompt = h_last.reshape(1, bp, D_RNN).astype(state_h.dtype)

    ns = bs * ts
    xs2 = jnp.swapaxes(x_sample, 0, 1).reshape(ns, d_model)
    qs, ks, vs, kts, vts, xrs, ggs = _in_proj(xs2, g_pre, w_in_b, ns, bs, True)
    own_width = 128
    bias_s = _sample_bias_table(rel_bias, past_len, ts, own_width)
    feature_major_pages = lambda c: jnp.transpose(c, (0, 2, 3, 1)).reshape(n_phys, D_ATTN, page)
    tm3 = lambda a: a.reshape(ts, bs, a.shape[-1])
    oas = _sample_attention(page_table, tm3(qs), tm3(ks), tm3(vs), bias_s,
                            feature_major_pages(cache_k[l]), feature_major_pages(cache_v[l]))
    xrs3 = tm3(xrs)
    orns, hs_last = _sample_rnn(xrs3, tm3(ggs), jnp.swapaxes(state_conv[l], 0, 1), state_h[l], rnn_w)
    y_sample = _tail(xs2, oas.reshape(ns, D_ATTN), orns.reshape(ns, D_RNN), tail_w, ns, 256)
    y_sample = jnp.swapaxes(y_sample.reshape(ts, bs, d_model), 0, 1)
    new_k_sample = _rows_from_feature_major(kts)
    new_v_sample = _rows_from_feature_major(vts)
    new_conv_sample = jnp.swapaxes(xrs3[ts - (CONV_W - 1):], 0, 1)[None]
    new_h_sample = hs_last[None].astype(state_h.dtype)

    return (y_prompt, y_sample, new_k_prompt, new_v_prompt, new_k_sample, new_v_sample,
            new_conv_prompt, new_h_prompt, new_conv_sample, new_h_sample)
```

```python
import functools
import math

import numpy as np
import jax
import jax.numpy as jnp
from jax import lax
from jax.experimental import pallas as pl
from jax.experimental.pallas import tpu as pltpu

N_HEADS = 8
HEAD_DIM = 64
D_ATTN = N_HEADS * HEAD_DIM
D_RNN = 512
N_RG_BLOCKS = 8
CONV_W = 4
RG_C = 8.0
MOBA_BLOCK = 256
MOBA_TOP_K = 3
NUM_BUCKETS = 32
MAX_DISTANCE = 128
RMS_EPS = 1e-6
NEG_INF = -1e30

SUBLANES = 8
VMEM_LIMIT = 56 * 1024 * 1024
F32 = jnp.float32
BF16 = jnp.bfloat16
HIGHEST = lax.Precision.HIGHEST


def _params(*sem):
    return pltpu.CompilerParams(dimension_semantics=sem, vmem_limit_bytes=VMEM_LIMIT)


def _resident(shape):
    nd = len(shape)
    return pl.BlockSpec(shape, lambda *_: (0,) * nd, pipeline_mode=pl.Buffered(1))


def _rms(x, g):
    return x * lax.rsqrt(jnp.mean(x * x, axis=-1, keepdims=True) + RMS_EPS) * g


def _t5_bucket_np(n):
    n = np.maximum(n, 0)
    max_exact = NUM_BUCKETS // 2
    nf = np.maximum(n, 1).astype(np.float32)
    large = max_exact + (np.log(nf / np.float32(max_exact)) / np.float32(math.log(MAX_DISTANCE / max_exact))
                         * np.float32(NUM_BUCKETS - max_exact)).astype(np.int32)
    large = np.minimum(large, NUM_BUCKETS - 1)
    return np.where(n < max_exact, n, large).astype(np.int32)


def _in_proj_kernel(natural_kv, x_ref, g_ref, w_ref, *out_refs):
    if natural_kv:
        q_ref, k_ref, v_ref, kt_ref, vt_ref, xr_ref, gg_ref = out_refs
    else:
        q_ref, kb_ref, vtb_ref, means_ref, kt_ref, vt_ref, xr_ref, gg_ref = out_refs
    hb = _rms(x_ref[...], g_ref[...]).astype(BF16)

    def proj(lo, width):
        return jnp.dot(hb, w_ref[:, lo:lo + width], preferred_element_type=F32)

    q_ref[...] = proj(0, D_ATTN)
    k = proj(D_ATTN, D_ATTN)
    v = proj(2 * D_ATTN, D_ATTN)
    groups, _, width = kt_ref.shape
    for r in range(groups):
        kt_ref[r] = k[r * width:(r + 1) * width, :].T
        vt_ref[r] = v[r * width:(r + 1) * width, :].T
    if natural_kv:
        k_ref[...] = k
        v_ref[...] = v
    else:
        kb_ref[...] = k.astype(BF16)
        for r in range(vtb_ref.shape[0]):
            blk = slice(r * MOBA_BLOCK, (r + 1) * MOBA_BLOCK)
            vtb_ref[r] = v[blk, :].T.astype(BF16)
            means_ref[r] = jnp.sum(k[blk, :], axis=0, keepdims=True) * (1.0 / MOBA_BLOCK)
    xr_ref[...] = proj(3 * D_ATTN, D_RNN)
    gg_ref[...] = jax.nn.gelu(proj(3 * D_ATTN + D_RNN, D_RNN))


def _in_proj(x2d, g, w_in_b, tm, group_len, natural_kv):
    n, d_model = x2d.shape
    assert n % tm == 0 and n % group_len == 0 and (tm % group_len == 0 or group_len % tm == 0)
    tiles_per_group = max(group_len // tm, 1)
    t_block = (max(tm // group_len, 1), D_ATTN, min(tm, group_len))
    t_spec = pl.BlockSpec(t_block, lambda i: (i // tiles_per_group, 0, i % tiles_per_group))
    t_shape = jax.ShapeDtypeStruct((n // group_len, D_ATTN, group_len), F32)
    row = lambda width: pl.BlockSpec((tm, width), lambda i: (i, 0))
    f32 = lambda width: jax.ShapeDtypeStruct((n, width), F32)
    if natural_kv:
        kv_specs = [row(D_ATTN), row(D_ATTN)]
        kv_shapes = [f32(D_ATTN), f32(D_ATTN)]
    else:
        assert tm % MOBA_BLOCK == 0
        blocks = tm // MOBA_BLOCK
        kv_specs = [row(D_ATTN), pl.BlockSpec((blocks, D_ATTN, MOBA_BLOCK), lambda i: (i, 0, 0)),
                    pl.BlockSpec((blocks, 1, D_ATTN), lambda i: (i, 0, 0))]
        kv_shapes = [jax.ShapeDtypeStruct((n, D_ATTN), BF16),
                     jax.ShapeDtypeStruct((n // MOBA_BLOCK, D_ATTN, MOBA_BLOCK), BF16),
                     jax.ShapeDtypeStruct((n // MOBA_BLOCK, 1, D_ATTN), F32)]
    return pl.pallas_call(
        functools.partial(_in_proj_kernel, natural_kv),
        grid=(n // tm,),
        in_specs=[row(d_model), _resident((1, d_model)), _resident(w_in_b.shape)],
        out_specs=[row(D_ATTN)] + kv_specs + [t_spec, t_spec, row(D_RNN), row(D_RNN)],
        out_shape=[f32(D_ATTN)] + kv_shapes + [t_shape, t_shape, f32(D_RNN), f32(D_RNN)],
        compiler_params=_params("parallel"),
        name="in_proj",
    )(x2d, g, w_in_b)


def _bias_table_kernel(rb_ref, bucket_ref, out_ref):
    bucket = bucket_ref[...]
    for h in range(N_HEADS):
        acc = jnp.full(bucket.shape, NEG_INF, F32)
        for b in range(NUM_BUCKETS):
            acc = jnp.where(bucket == b, rb_ref[b, h], acc)
        out_ref[h] = acc


def _prompt_bias_tables(rel_bias):
    kr = np.arange(2 * MOBA_BLOCK)[:, None]
    qr = np.arange(MOBA_BLOCK)[None, :]
    dist = qr + MOBA_BLOCK - kr
    bucket = np.where(dist >= 0, _t5_bucket_np(dist), -1).astype(np.int32)
    return pl.pallas_call(
        _bias_table_kernel,
        in_specs=[pl.BlockSpec(memory_space=pltpu.SMEM), pl.BlockSpec(memory_space=pltpu.VMEM)],
        out_specs=pl.BlockSpec(memory_space=pltpu.VMEM),
        out_shape=jax.ShapeDtypeStruct((N_HEADS,) + bucket.shape, F32),
        name="prompt_bias_tables",
    )(rel_bias, jnp.asarray(bucket))


def _sample_bias_kernel(rbt_ref, bucket_ref, out_ref):
    bucket = bucket_ref[...]
    acc = jnp.full(bucket.shape, NEG_INF, F32)
    for b in range(NUM_BUCKETS):
        acc = jnp.where(bucket == b, rbt_ref[:, b:b + 1], acc)
    out_ref[...] = acc


def _sample_bias_table(rel_bias, past_len, t_new, own_width):
    t = np.repeat(np.arange(t_new), N_HEADS)[:, None]
    kpos = np.arange(past_len + own_width)[None, :]
    dist = past_len + t - kpos
    valid = (dist >= 0) & (kpos < past_len + t_new)
    bucket = np.where(valid, _t5_bucket_np(dist), -1).astype(np.int32)
    rbt = jnp.tile(rel_bias.T, (t_new, 1))
    return pl.pallas_call(
        _sample_bias_kernel,
        out_shape=jax.ShapeDtypeStruct(bucket.shape, F32),
        name="sample_bias_table",
    )(rbt, jnp.asarray(bucket))


def _select_rows(scores, n_valid, axis):
    nb = scores.shape[axis]
    idx = lax.broadcasted_iota(jnp.int32, scores.shape, axis)
    beaten = jnp.zeros(scores.shape, jnp.int32)
    for n in range(nb):
        other = lax.slice_in_dim(scores, n, n + 1, axis=axis)
        beats = (other > scores) | ((other == scores) & (n < idx))
        beaten = beaten + jnp.where(beats, jnp.where(n < n_valid, 1, 0), 0)
    return (beaten < MOBA_TOP_K) & (idx < n_valid)


def _prompt_attn_kernel(rb_ref, q_ref, kb_ref, vtb_ref, means_ref, tab_ref, o_ref,
                        mb_ref, sel_ref, qm_ref, m_ref, l_ref, acc_ref, s0_ref, s1_ref):
    i = pl.program_id(1)
    nb = kb_ref.shape[0]
    prev = jnp.maximum(i - 1, 0)
    pair_rows = 2 * HEAD_DIM
    qt = q_ref[...].T

    means = means_ref[...]
    head_of_lane = lax.broadcasted_iota(jnp.int32, means.shape, 1) // HEAD_DIM
    for h in range(N_HEADS):
        mb_ref[h * nb:(h + 1) * nb, :] = jnp.where(head_of_lane == h, means, 0.0)
    sct = jnp.dot(mb_ref[...], qt, precision=HIGHEST, preferred_element_type=F32)

    row_in_pair = lax.broadcasted_iota(jnp.int32, (pair_rows, MOBA_BLOCK), 0) // HEAD_DIM
    block_id = lax.broadcasted_iota(jnp.int32, (nb, MOBA_BLOCK), 0)
    for h in range(N_HEADS):
        far_bias = rb_ref[NUM_BUCKETS - 1, h]
        selected = _select_rows(sct[h * nb:(h + 1) * nb, :], i, axis=0)
        sel_ref[h * nb:(h + 1) * nb, :] = jnp.where(
            selected, jnp.where(block_id == i - 1, 0.0, far_bias), NEG_INF)
        q_pair = qt[(h // 2) * pair_rows:(h // 2 + 1) * pair_rows, :] * (HEAD_DIM ** -0.5)
        qm_ref[h] = jnp.where(row_in_pair == h % 2, q_pair, 0.0).astype(BF16)

    def scores(j, table_rows, use_sel, h, s_ref):
        pair = h // 2
        k_pair = kb_ref[j, :, pair * pair_rows:(pair + 1) * pair_rows]
        s = jnp.dot(k_pair, qm_ref[h], preferred_element_type=F32)
        if table_rows is not None:
            s = s + tab_ref[h, table_rows, :]
        if use_sel:
            s = s + sel_ref[pl.ds(h * nb + j, 1), :]
        s_ref[h] = s

    def consume(j, first, h, s_ref):
        hs = slice(h * HEAD_DIM, (h + 1) * HEAD_DIM)
        s = s_ref[h]
        if first:
            m_new = jnp.max(s, axis=0, keepdims=True)
            p = jnp.exp(s - m_new)
            l_ref[h:h + 1, :] = jnp.sum(p, axis=0, keepdims=True)
        else:
            m_old = m_ref[h:h + 1, :]
            m_new = jnp.maximum(m_old, jnp.max(s, axis=0, keepdims=True))
            alpha = jnp.exp(m_old - m_new)
            p = jnp.exp(s - m_new)
            l_ref[h:h + 1, :] = alpha * l_ref[h:h + 1, :] + jnp.sum(p, axis=0, keepdims=True)
        m_ref[h:h + 1, :] = m_new
        pv = jnp.dot(vtb_ref[j, hs, :], p.astype(BF16), preferred_element_type=F32)
        acc_ref[hs, :] = pv if first else alpha * acc_ref[hs, :] + pv

    own_rows = slice(MOBA_BLOCK, 2 * MOBA_BLOCK)
    prev_rows = slice(0, MOBA_BLOCK)
    n_far = prev
    last_far = jnp.maximum(n_far - 1, 0)
    far = lambda j: jnp.minimum(j, last_far)
    heads = range(N_HEADS)

    def consume_then_refill(j, first, j_refill, s_ref):
        for h in heads:
            consume(j, first, h, s_ref)
            scores(far(j_refill), None, True, h, s_ref)

    for h in heads:
        scores(i, own_rows, False, h, s0_ref)
    for h in heads:
        scores(prev, prev_rows, True, h, s1_ref)
    consume_then_refill(i, True, 0, s0_ref)
    consume_then_refill(prev, False, 1, s1_ref)

    def far_pair(t, carry):
        consume_then_refill(2 * t, False, 2 * t + 2, s0_ref)
        consume_then_refill(2 * t + 1, False, 2 * t + 3, s1_ref)
        return carry

    lax.fori_loop(0, n_far // 2, far_pair, 0)

    @pl.when(n_far % 2 == 1)
    def _():
        for h in heads:
            consume(n_far - 1, False, h, s0_ref)

    for h in range(N_HEADS):
        hs = slice(h * HEAD_DIM, (h + 1) * HEAD_DIM)
        acc_ref[hs, :] = acc_ref[hs, :] / l_ref[h:h + 1, :]
    o_ref[...] = acc_ref[...].T.astype(o_ref.dtype)


def _prompt_attention(rel_bias, q, kb4, vtb4, means, tables):
    b, nb = kb4.shape[:2]
    t = nb * MOBA_BLOCK
    return pl.pallas_call(
        _prompt_attn_kernel,
        grid=(b, nb),
        in_specs=[pl.BlockSpec(memory_space=pltpu.SMEM),
                  pl.BlockSpec((None, MOBA_BLOCK, D_ATTN), lambda bi, i: (bi, i, 0)),
                  pl.BlockSpec((None, nb, MOBA_BLOCK, D_ATTN), lambda bi, i: (bi, 0, 0, 0)),
                  pl.BlockSpec((None, nb, D_ATTN, MOBA_BLOCK), lambda bi, i: (bi, 0, 0, 0)),
                  pl.BlockSpec((None, nb, D_ATTN), lambda bi, i: (bi, 0, 0)),
                  _resident(tables.shape)],
        out_specs=pl.BlockSpec((None, MOBA_BLOCK, D_ATTN), lambda bi, i: (bi, i, 0)),
        out_shape=jax.ShapeDtypeStruct((b, t, D_ATTN), BF16),
        scratch_shapes=[pltpu.VMEM((N_HEADS * nb, D_ATTN), F32),
                        pltpu.VMEM((N_HEADS * nb, MOBA_BLOCK), F32),
                        pltpu.VMEM((N_HEADS, 2 * HEAD_DIM, MOBA_BLOCK), BF16),
                        pltpu.VMEM((N_HEADS, MOBA_BLOCK), F32),
                        pltpu.VMEM((N_HEADS, MOBA_BLOCK), F32),
                        pltpu.VMEM((D_ATTN, MOBA_BLOCK), F32),
                        pltpu.VMEM((N_HEADS, MOBA_BLOCK, MOBA_BLOCK), F32),
                        pltpu.VMEM((N_HEADS, MOBA_BLOCK, MOBA_BLOCK), F32)],
        compiler_params=_params("parallel", "arbitrary"),
        name="prompt_attention",
    )(rel_bias, q, kb4, vtb4, means, tables)


def _sample_attn_kernel(n_pages, pt_ref, q_ref, kn_ref, vn_ref, bias_ref, *refs):
    k_pages = refs[:n_pages]
    v_pages = refs[n_pages:2 * n_pages]
    o_ref = refs[2 * n_pages]
    knp_ref, vnp_ref = refs[2 * n_pages + 1:]
    del pt_ref
    t_new, seqs_per_block, _ = q_ref.shape
    page = k_pages[0].shape[1]
    pages_per_block = MOBA_BLOCK // page
    nb = n_pages // pages_per_block
    past_len = n_pages * page
    row = pl.ds(pl.program_id(0) % seqs_per_block, 1)

    head_of_row = lax.broadcasted_iota(jnp.int32, (N_HEADS, D_ATTN), 0)
    head_of_lane = lax.broadcasted_iota(jnp.int32, (N_HEADS, D_ATTN), 1) // HEAD_DIM
    head_mask = head_of_row == head_of_lane
    qrows = jnp.concatenate(
        [jnp.where(head_mask, jnp.broadcast_to(q_ref[t, row, :], (N_HEADS, D_ATTN)), 0.0) for t in range(t_new)],
        axis=0)

    lane = lax.broadcasted_iota(jnp.int32, (D_ATTN, page), 1)
    means_t = jnp.zeros((D_ATTN, page), F32)
    for n in range(nb):
        total = sum(k_pages[n * pages_per_block + r][...] for r in range(pages_per_block))
        means_t = jnp.where(lane == n, jnp.sum(total, axis=1, keepdims=True) * (1.0 / MOBA_BLOCK), means_t)
    scores = jnp.dot(qrows, means_t, precision=HIGHEST, preferred_element_type=F32)[:, :nb]
    selected = jnp.where(_select_rows(scores, nb, axis=1), 1.0, 0.0)

    qb = (qrows * (HEAD_DIM ** -0.5)).astype(BF16)
    s_pages = []
    for p in range(n_pages):
        s = jnp.dot(qb, k_pages[p][...].astype(BF16), preferred_element_type=F32)
        keep = selected[:, p // pages_per_block:p // pages_per_block + 1] > 0.5
        s_pages.append(jnp.where(keep, s + bias_ref[:, p * page:(p + 1) * page], NEG_INF))
    s_past = jnp.concatenate(s_pages, axis=1)

    knp_ref[...] = jnp.zeros(knp_ref.shape, F32)
    vnp_ref[...] = jnp.zeros(vnp_ref.shape, F32)
    for t in range(t_new):
        knp_ref[t:t + 1, :] = kn_ref[t, row, :]
        vnp_ref[t:t + 1, :] = vn_ref[t, row, :]
    nt = (((1,), (1,)), ((), ()))
    s_own = (lax.dot_general(qb, knp_ref[...].astype(BF16), nt, preferred_element_type=F32)
             + bias_ref[:, past_len:])

    m = jnp.maximum(jnp.max(s_past, axis=1, keepdims=True), jnp.max(s_own, axis=1, keepdims=True))
    p_past = jnp.exp(s_past - m)
    p_own = jnp.exp(s_own - m)
    l = jnp.sum(p_past, axis=1, keepdims=True) + jnp.sum(p_own, axis=1, keepdims=True)
    o = jnp.dot(p_own.astype(BF16), vnp_ref[...].astype(BF16), preferred_element_type=F32)
    for p in range(n_pages):
        o = o + lax.dot_general(p_past[:, p * page:(p + 1) * page].astype(BF16), v_pages[p][...].astype(BF16), nt,
                                preferred_element_type=F32)
    o = o / l
    for t in range(t_new):
        o_ref[t, row, :] = jnp.sum(jnp.where(head_mask, o[t * N_HEADS:(t + 1) * N_HEADS, :], 0.0),
                                   axis=0, keepdims=True)


def _sample_attention(page_table, q, k_new, v_new, bias, cache_kt, cache_vt):
    t_new, bs, _ = q.shape
    n_pages = page_table.shape[1]
    page = cache_kt.shape[2]
    own_width = bias.shape[1] - n_pages * page
    seqs_per_block = SUBLANES
    assert bs % seqs_per_block == 0
    per_seq = pl.BlockSpec((t_new, seqs_per_block, D_ATTN), lambda s, pt: (0, s // seqs_per_block, 0))
    page_spec = lambda p: pl.BlockSpec((None, D_ATTN, page), lambda s, pt: (pt[s, p], 0, 0))
    grid_spec = pltpu.PrefetchScalarGridSpec(
        num_scalar_prefetch=1,
        grid=(bs,),
        in_specs=[per_seq, per_seq, per_seq, pl.BlockSpec(bias.shape, lambda s, pt: (0, 0))]
                 + [page_spec(p) for p in range(n_pages)] * 2,
        out_specs=per_seq,
        scratch_shapes=[pltpu.VMEM((own_width, D_ATTN), F32), pltpu.VMEM((own_width, D_ATTN), F32)],
    )
    return pl.pallas_call(
        functools.partial(_sample_attn_kernel, n_pages),
        grid_spec=grid_spec,
        out_shape=jax.ShapeDtypeStruct((t_new, bs, D_ATTN), F32),
        compiler_params=_params("arbitrary"),
        name="sample_attention",
    )(page_table, q, k_new, v_new, bias, *([cache_kt] * n_pages), *([cache_vt] * n_pages))


def _rglru_coeffs(xc, wr_ref, br_ref, wi_ref, bi_ref, lam_ref):
    xb = xc.astype(BF16)
    r = jax.nn.sigmoid(jnp.dot(xb, wr_ref[...], preferred_element_type=F32) + br_ref[...])
    i = jax.nn.sigmoid(jnp.dot(xb, wi_ref[...], preferred_element_type=F32) + bi_ref[...])
    log_a = -RG_C * r * jax.nn.softplus(-lam_ref[...])
    a = jnp.exp(log_a)
    z = -jnp.tanh(log_a) * (a * a + 1.0)
    b = jnp.where(z > 0.0, z * lax.rsqrt(z), 0.0) * (i * xc)
    return a, b


def _prompt_rnn_kernel(xr_ref, gg_ref, cw_ref, cb_ref, wr_ref, br_ref, wi_ref, bi_ref, lam_ref,
                       o_ref, hl_ref, xbuf_ref, h_ref, obuf_ref):
    tt = xr_ref.shape[0]

    @pl.when(pl.program_id(1) == 0)
    def _():
        xbuf_ref[...] = jnp.zeros(xbuf_ref.shape, F32)
        h_ref[...] = jnp.zeros(h_ref.shape, F32)

    x = xr_ref[...]
    x_ext = jnp.concatenate([xbuf_ref[...], x], axis=0)
    row = lax.broadcasted_iota(jnp.int32, (tt, D_RNN), 0) % SUBLANES
    xc = cb_ref[...] + x * cw_ref[CONV_W - 1:CONV_W, :]
    for s in range(1, CONV_W):
        xc = xc + pltpu.roll(x_ext, s, axis=0)[SUBLANES:] * cw_ref[CONV_W - 1 - s:CONV_W - s, :]
    xbuf_ref[...] = x[tt - SUBLANES:]
    a, b = _rglru_coeffs(xc, wr_ref, br_ref, wi_ref, bi_ref, lam_ref)

    groups = (tt // SUBLANES, SUBLANES, D_RNN)
    a, b, row = a.reshape(groups), b.reshape(groups), row.reshape(groups)
    shift = 1
    while shift < SUBLANES:
        a_prev = pltpu.roll(a, shift, axis=1)
        b_prev = pltpu.roll(b, shift, axis=1)
        ok = row >= shift
        b = jnp.where(ok, a * b_prev + b, b)
        a = jnp.where(ok, a * a_prev, a)
        shift *= 2
    h = h_ref[...]
    for g in range(tt // SUBLANES):
        sl = slice(g * SUBLANES, (g + 1) * SUBLANES)
        hg = a[g] * h + b[g]
        obuf_ref[sl, :] = hg * gg_ref[sl, :]
        h = hg[SUBLANES - 1:SUBLANES, :]
    o_ref[...] = obuf_ref[...].astype(o_ref.dtype)
    h_ref[...] = h
    hl_ref[...] = h


def _prompt_rnn(xr, gg, weights, tt):
    b, t, _ = xr.shape
    assert t % tt == 0 and tt % SUBLANES == 0
    tile = pl.BlockSpec((None, tt, D_RNN), lambda bi, i: (bi, i, 0))
    return pl.pallas_call(
        _prompt_rnn_kernel,
        grid=(b, t // tt),
        in_specs=[tile, tile] + [_resident(w.shape) for w in weights],
        out_specs=[tile, pl.BlockSpec((None, 1, D_RNN), lambda bi, i: (bi, 0, 0))],
        out_shape=[jax.ShapeDtypeStruct((b, t, D_RNN), BF16), jax.ShapeDtypeStruct((b, 1, D_RNN), F32)],
        scratch_shapes=[pltpu.VMEM((SUBLANES, D_RNN), F32), pltpu.VMEM((1, D_RNN), F32),
                        pltpu.VMEM((tt, D_RNN), F32)],
        compiler_params=_params("parallel", "arbitrary"),
        name="prompt_rnn",
    )(xr, gg, *weights)


def _sample_rnn_kernel(xr_ref, gg_ref, conv_ref, h0_ref, cw_ref, cb_ref, wr_ref, br_ref, wi_ref, bi_ref, lam_ref,
                       o_ref, hl_ref):
    t_new = xr_ref.shape[0]
    past = [conv_ref[j] for j in range(CONV_W - 1)] + [xr_ref[t] for t in range(t_new)]
    h = h0_ref[...]
    for t in range(t_new):
        xc = cb_ref[...] + sum(past[t + j] * cw_ref[j:j + 1, :] for j in range(CONV_W))
        a, b = _rglru_coeffs(xc, wr_ref, br_ref, wi_ref, bi_ref, lam_ref)
        h = a * h + b
        o_ref[t] = h * gg_ref[t]
    hl_ref[...] = h


def _sample_rnn(xr_t, gg_t, conv_t, h0, weights):
    return pl.pallas_call(
        _sample_rnn_kernel,
        out_shape=[jax.ShapeDtypeStruct(xr_t.shape, F32), jax.ShapeDtypeStruct(h0.shape, F32)],
        compiler_params=pltpu.CompilerParams(vmem_limit_bytes=VMEM_LIMIT),
        name="sample_rnn",
    )(xr_t, gg_t, conv_t, h0, *weights)


def _tail_kernel(chunk, x_ref, oa_ref, orn_ref, gpre_ref, wg_ref, wpa_ref, wpr_ref, wo_ref, gpost_ref,
                 fpre_ref, wgu_ref, wd_ref, fpost_ref, y_ref):
    d_model = x_ref.shape[-1]
    d_ff = wd_ref.shape[0]
    x = x_ref[...]
    hb = _rms(x, gpre_ref[...]).astype(BF16)
    pa = jnp.dot(oa_ref[...].astype(BF16), wpa_ref[...], preferred_element_type=F32)
    merged = jax.nn.sigmoid(jnp.dot(hb, wg_ref[:, :d_model], preferred_element_type=F32)) * pa
    pr = jnp.dot(orn_ref[...].astype(BF16), wpr_ref[...], preferred_element_type=F32)
    merged = merged + jax.nn.sigmoid(jnp.dot(hb, wg_ref[:, d_model:], preferred_element_type=F32)) * pr
    x1 = x + _rms(jnp.dot(merged.astype(BF16), wo_ref[...], preferred_element_type=F32), gpost_ref[...])

    fb = _rms(x1, fpre_ref[...]).astype(BF16)
    y = jnp.zeros(x.shape, F32)
    for c in range(0, d_ff, chunk):
        gate = jnp.dot(fb, wgu_ref[:, c:c + chunk], preferred_element_type=F32)
        up = jnp.dot(fb, wgu_ref[:, d_ff + c:d_ff + c + chunk], preferred_element_type=F32)
        act = (jax.nn.silu(gate) * up).astype(BF16)
        y = y + jnp.dot(act, wd_ref[c:c + chunk, :], preferred_element_type=F32)
    y_ref[...] = x1 + _rms(y, fpost_ref[...])


def _tail(x2d, oa, orn, weights, tm, chunk):
    n, d_model = x2d.shape
    assert n % tm == 0 and weights[8].shape[0] % chunk == 0
    row = lambda width: pl.BlockSpec((tm, width), lambda i: (i, 0))
    return pl.pallas_call(
        functools.partial(_tail_kernel, chunk),
        grid=(n // tm,),
        in_specs=[row(d_model), row(D_ATTN), row(D_RNN)] + [_resident(w.shape) for w in weights],
        out_specs=row(d_model),
        out_shape=jax.ShapeDtypeStruct((n, d_model), F32),
        compiler_params=_params("parallel"),
        name="tail",
    )(x2d, oa, orn, *weights)


def _block_diag(w):
    nblk, c, _ = w.shape
    eye = jnp.eye(nblk, dtype=w.dtype)
    return (w[:, :, None, :] * eye[:, None, :, None]).reshape(nblk * c, nblk * c)


def _rows_from_feature_major(xt):
    g, _, w = xt.shape
    return jnp.transpose(xt.reshape(g, N_HEADS, HEAD_DIM, w), (3, 0, 1, 2))[None]


def kernel(x_prompt, x_sample, cache_k, cache_v, page_table, state_conv, state_h, norm_mix_pre, norm_mix_post,
           w_in, rel_bias, conv_w, conv_b, rg_w_r, rg_b_r, rg_w_i, rg_b_i, rg_lambda, w_proj_attn, w_proj_rnn,
           w_out, norm_ffn_pre, norm_ffn_post, w_gate_up, w_down):
    depth = w_in.shape[0]
    assert depth == 1
    bp, tp, d_model = x_prompt.shape
    bs, ts, _ = x_sample.shape
    n_phys, page = cache_k.shape[1:3]
    n_pages = page_table.shape[1]
    past_len = n_pages * page
    assert tp % MOBA_BLOCK == 0 and past_len % MOBA_BLOCK == 0 and MOBA_BLOCK % page == 0
    assert past_len // MOBA_BLOCK > MOBA_TOP_K and CONV_W - 1 <= ts <= MOBA_BLOCK
    nbp = tp // MOBA_BLOCK
    l = 0
    row2 = lambda v: v.reshape(1, -1)

    n_stream_cols = 3 * D_ATTN + 2 * D_RNN
    w_in_b = w_in[l, :, :n_stream_cols].astype(BF16)
    w_gates = w_in[l, :, n_stream_cols:].astype(BF16)
    rnn_w = (conv_w[l], row2(conv_b[l]), _block_diag(rg_w_r[l]).astype(BF16), row2(rg_b_r[l]),
             _block_diag(rg_w_i[l]).astype(BF16), row2(rg_b_i[l]), row2(rg_lambda[l]))
    wpa, wpr, wo = w_proj_attn[l].astype(BF16), w_proj_rnn[l].astype(BF16), w_out[l].astype(BF16)
    wgu, wd = w_gate_up[l].astype(BF16), w_down[l].astype(BF16)
    g_pre, g_post = row2(norm_mix_pre[l]), row2(norm_mix_post[l])
    gf_pre, gf_post = row2(norm_ffn_pre[l]), row2(norm_ffn_post[l])

    tail_w = (g_pre, w_gates, wpa, wpr, wo, g_post, gf_pre, wgu, wd, gf_post)

    tm = 512
    xp2 = x_prompt.reshape(bp * tp, d_model)
    q, kb, vtb, means, kt, vt, xr, gg = _in_proj(xp2, g_pre, w_in_b, tm, tp, False)
    tables = _prompt_bias_tables(rel_bias)
    oa = _prompt_attention(rel_bias, q.reshape(bp, tp, D_ATTN), kb.reshape(bp, nbp, MOBA_BLOCK, D_ATTN),
                           vtb.reshape(bp, nbp, D_ATTN, MOBA_BLOCK), means.reshape(bp, nbp, D_ATTN), tables)
    xr3 = xr.reshape(bp, tp, D_RNN)
    orn, h_last = _prompt_rnn(xr3, gg.reshape(bp, tp, D_RNN), rnn_w, 256)
    y_prompt = _tail(xp2, oa.reshape(bp * tp, D_ATTN), orn.reshape(bp * tp, D_RNN), tail_w, tm, 256)
    y_prompt = y_prompt.reshape(bp, tp, d_model)
    new_k_prompt = jnp.swapaxes(_rows_from_feature_major(kt), 1, 2)
    new_v_prompt = jnp.swapaxes(_rows_from_feature_major(vt), 1, 2)
    new_conv_prompt = xr3[:, tp - (CONV_W - 1):, :][None]
    new_h_prompt = h_last.reshape(1, bp, D_RNN).astype(state_h.dtype)

    ns = bs * ts
    xs2 = jnp.swapaxes(x_sample, 0, 1).reshape(ns, d_model)
    qs, ks, vs, kts, vts, xrs, ggs = _in_proj(xs2, g_pre, w_in_b, ns, bs, True)
    own_width = 128
    bias_s = _sample_bias_table(rel_bias, past_len, ts, own_width)
    feature_major_pages = lambda c: jnp.transpose(c, (0, 2, 3, 1)).reshape(n_phys, D_ATTN, page)
    tm3 = lambda a: a.reshape(ts, bs, a.shape[-1])
    oas = _sample_attention(page_table, tm3(qs), tm3(ks), tm3(vs), bias_s,
                            feature_major_pages(cache_k[l]), feature_major_pages(cache_v[l]))
    xrs3 = tm3(xrs)
    orns, hs_last = _sample_rnn(xrs3, tm3(ggs), jnp.swapaxes(state_conv[l], 0, 1), state_h[l], rnn_w)
    y_sample = _tail(xs2, oas.reshape(ns, D_ATTN), orns.reshape(ns, D_RNN), tail_w, ns, 256)
    y_sample = jnp.swapaxes(y_sample.reshape(ts, bs, d_model), 0, 1)
    new_k_sample = _rows_from_feature_major(kts)
    new_v_sample = _rows_from_feature_major(vts)
    new_conv_sample = jnp.swapaxes(xrs3[ts - (CONV_W - 1):], 0, 1)[None]
    new_h_sample = hs_last[None].astype(state_h.dtype)

    return (y_prompt, y_sample, new_k_prompt, new_v_prompt, new_k_sample, new_v_sample,
            new_conv_prompt, new_h_prompt, new_conv_sample, new_h_sample)
```

```python
import functools
import math

import numpy as np
import jax
import jax.numpy as jnp
from jax import lax
from jax.experimental import pallas as pl
from jax.experimental.pallas import tpu as pltpu

N_HEADS = 8
HEAD_DIM = 64
D_ATTN = N_HEADS * HEAD_DIM
D_RNN = 512
N_RG_BLOCKS = 8
CONV_W = 4
RG_C = 8.0
MOBA_BLOCK = 256
MOBA_TOP_K = 3
NUM_BUCKETS = 32
MAX_DISTANCE = 128
RMS_EPS = 1e-6
NEG_INF = -1e30
BF16_SUBLANES = 16
V_ROWS = HEAD_DIM + BF16_SUBLANES
LOG2E = math.log2(math.e)

SUBLANES = 8
VMEM_LIMIT = 56 * 1024 * 1024
F32 = jnp.float32
BF16 = jnp.bfloat16
HIGHEST = lax.Precision.HIGHEST


def _params(*sem):
    return pltpu.CompilerParams(dimension_semantics=sem, vmem_limit_bytes=VMEM_LIMIT)


def _resident(shape):
    nd = len(shape)
    return pl.BlockSpec(shape, lambda *_: (0,) * nd, pipeline_mode=pl.Buffered(1))


def _rms(x, g):
    return x * lax.rsqrt(jnp.mean(x * x, axis=-1, keepdims=True) + RMS_EPS) * g


def _t5_bucket_np(n):
    n = np.maximum(n, 0)
    max_exact = NUM_BUCKETS // 2
    nf = np.maximum(n, 1).astype(np.float32)
    large = max_exact + (np.log(nf / np.float32(max_exact)) / np.float32(math.log(MAX_DISTANCE / max_exact))
                         * np.float32(NUM_BUCKETS - max_exact)).astype(np.int32)
    large = np.minimum(large, NUM_BUCKETS - 1)
    return np.where(n < max_exact, n, large).astype(np.int32)


def _rglru_coeffs(xc, wr_ref, br_ref, wi_ref, bi_ref, lam_ref):
    xb = xc.astype(BF16)
    r = jax.nn.sigmoid(jnp.dot(xb, wr_ref[...], preferred_element_type=F32) + br_ref[...])
    i = jax.nn.sigmoid(jnp.dot(xb, wi_ref[...], preferred_element_type=F32) + bi_ref[...])
    log_a = -RG_C * r * jax.nn.softplus(-lam_ref[...])
    a = jnp.exp(log_a)
    z = -jnp.tanh(log_a) * (a * a + 1.0)
    b = jnp.where(z > 0.0, z * lax.rsqrt(z), 0.0) * (i * xc)
    return a, b


def _rglru_tile(x, gg, first_tile, cw_ref, cb_ref, wr_ref, br_ref, wi_ref, bi_ref, lam_ref,
                o_ref, hl_ref, xbuf_ref, h_ref, obuf_ref):
    tt = x.shape[0]

    @pl.when(first_tile)
    def _():
        xbuf_ref[...] = jnp.zeros(xbuf_ref.shape, F32)
        h_ref[...] = jnp.zeros(h_ref.shape, F32)

    x_ext = jnp.concatenate([xbuf_ref[...], x], axis=0)
    row = lax.broadcasted_iota(jnp.int32, (tt, D_RNN), 0) % SUBLANES
    xc = cb_ref[...] + x * cw_ref[CONV_W - 1:CONV_W, :]
    for s in range(1, CONV_W):
        xc = xc + pltpu.roll(x_ext, s, axis=0)[SUBLANES:] * cw_ref[CONV_W - 1 - s:CONV_W - s, :]
    xbuf_ref[...] = x[tt - SUBLANES:]
    a, b = _rglru_coeffs(xc, wr_ref, br_ref, wi_ref, bi_ref, lam_ref)

    groups = (tt // SUBLANES, SUBLANES, D_RNN)
    a, b, row = a.reshape(groups), b.reshape(groups), row.reshape(groups)
    shift = 1
    while shift < SUBLANES:
        a_prev = pltpu.roll(a, shift, axis=1)
        b_prev = pltpu.roll(b, shift, axis=1)
        ok = row >= shift
        b = jnp.where(ok, a * b_prev + b, b)
        a = jnp.where(ok, a * a_prev, a)
        shift *= 2
    h = h_ref[...]
    for g in range(tt // SUBLANES):
        sl = slice(g * SUBLANES, (g + 1) * SUBLANES)
        hg = a[g] * h + b[g]
        obuf_ref[sl, :] = hg * gg[sl, :]
        h = hg[SUBLANES - 1:SUBLANES, :]
    o_ref[...] = obuf_ref[...].astype(o_ref.dtype)
    h_ref[...] = h
    hl_ref[...] = h


def _sample_rnn_kernel(xr_ref, gg_ref, conv_ref, h0_ref, cw_ref, cb_ref, wr_ref, br_ref, wi_ref, bi_ref, lam_ref,
                       o_ref, hl_ref):
    t_new = xr_ref.shape[0]
    past = [conv_ref[j] for j in range(CONV_W - 1)] + [xr_ref[t] for t in range(t_new)]
    h = h0_ref[...]
    for t in range(t_new):
        xc = cb_ref[...] + sum(past[t + j] * cw_ref[j:j + 1, :] for j in range(CONV_W))
        a, b = _rglru_coeffs(xc, wr_ref, br_ref, wi_ref, bi_ref, lam_ref)
        h = a * h + b
        o_ref[t] = h * gg_ref[t]
    hl_ref[...] = h


def _sample_rnn(xr_t, gg_t, conv_t, h0, weights):
    return pl.pallas_call(
        _sample_rnn_kernel,
        out_shape=[jax.ShapeDtypeStruct(xr_t.shape, F32), jax.ShapeDtypeStruct(h0.shape, F32)],
        compiler_params=pltpu.CompilerParams(vmem_limit_bytes=VMEM_LIMIT),
        name="sample_rnn",
    )(xr_t, gg_t, conv_t, h0, *weights)


def _projector(x_ref, g_ref, w_ref):
    hb = _rms(x_ref[...], g_ref[...]).astype(BF16)
    return lambda lo, width: jnp.dot(hb, w_ref[:, lo:lo + width], preferred_element_type=F32)


def _store_feature_major(kt_ref, vt_ref, k, v):
    groups, _, width = kt_ref.shape
    for r in range(groups):
        kt_ref[r] = k[r * width:(r + 1) * width, :].T
        vt_ref[r] = v[r * width:(r + 1) * width, :].T


def _sample_in_proj_kernel(x_ref, g_ref, w_ref, q_ref, k_ref, v_ref, kt_ref, vt_ref, xr_ref, gg_ref):
    proj = _projector(x_ref, g_ref, w_ref)
    q_ref[...] = proj(0, D_ATTN)
    k = proj(D_ATTN, D_ATTN)
    v = proj(2 * D_ATTN, D_ATTN)
    k_ref[...] = k
    v_ref[...] = v
    _store_feature_major(kt_ref, vt_ref, k, v)
    xr_ref[...] = proj(3 * D_ATTN, D_RNN)
    gg_ref[...] = jax.nn.gelu(proj(3 * D_ATTN + D_RNN, D_RNN))


def _prompt_in_proj_kernel(tiles_per_seq, x_ref, g_ref, w_ref, *refs):
    rnn_w = refs[:7]
    q_ref, kb_ref, vtb_ref, means_ref, kt_ref, vt_ref, orn_ref, xtail_ref, hl_ref = refs[7:16]
    xbuf_ref, h_ref, obuf_ref = refs[16:]
    proj = _projector(x_ref, g_ref, w_ref)
    xr = proj(3 * D_ATTN, D_RNN)
    gg = jax.nn.gelu(proj(3 * D_ATTN + D_RNN, D_RNN))
    xtail_ref[...] = xr[xr.shape[0] - SUBLANES:]
    _rglru_tile(xr, gg, pl.program_id(0) % tiles_per_seq == 0, *rnn_w, orn_ref, hl_ref, xbuf_ref, h_ref, obuf_ref)
    q_ref[...] = proj(0, D_ATTN)
    k = proj(D_ATTN, D_ATTN)
    v = proj(2 * D_ATTN, D_ATTN)
    _store_feature_major(kt_ref, vt_ref, k, v)
    kb_ref[...] = k.astype(BF16)
    pad_row = lax.broadcasted_iota(jnp.int32, (BF16_SUBLANES, MOBA_BLOCK), 0)
    ones_rows = jnp.where(pad_row == 0, 1.0, 0.0).astype(BF16)
    for r in range(vtb_ref.shape[0]):
        blk = slice(r * MOBA_BLOCK, (r + 1) * MOBA_BLOCK)
        v_t = v[blk, :].T.astype(BF16)
        for h in range(N_HEADS):
            vtb_ref[r, h] = jnp.concatenate([v_t[h * HEAD_DIM:(h + 1) * HEAD_DIM], ones_rows], axis=0)
        means_ref[r] = jnp.sum(k[blk, :], axis=0, keepdims=True) * (1.0 / MOBA_BLOCK)


def _feature_major_spec(n, tm, group_len):
    tiles_per_group = max(group_len // tm, 1)
    block = (max(tm // group_len, 1), D_ATTN, min(tm, group_len))
    spec = pl.BlockSpec(block, lambda i: (i // tiles_per_group, 0, i % tiles_per_group))
    return spec, jax.ShapeDtypeStruct((n // group_len, D_ATTN, group_len), F32)


def _sample_in_proj(x2d, g, w_in_b, group_len):
    n, d_model = x2d.shape
    assert n % group_len == 0
    t_spec, t_shape = _feature_major_spec(n, n, group_len)
    row = lambda width: pl.BlockSpec((n, width), lambda i: (i, 0))
    f32 = lambda width: jax.ShapeDtypeStruct((n, width), F32)
    return pl.pallas_call(
        _sample_in_proj_kernel,
        grid=(1,),
        in_specs=[row(d_model), _resident((1, d_model)), _resident(w_in_b.shape)],
        out_specs=[row(D_ATTN)] * 3 + [t_spec, t_spec, row(D_RNN), row(D_RNN)],
        out_shape=[f32(D_ATTN)] * 3 + [t_shape, t_shape, f32(D_RNN), f32(D_RNN)],
        compiler_params=_params("arbitrary"),
        name="sample_in_proj",
    )(x2d, g, w_in_b)


def _prompt_in_proj(x2d, g, w_in_b, rnn_w, tm, seq_len):
    n, d_model = x2d.shape
    assert n % seq_len == 0 and seq_len % tm == 0 and tm % MOBA_BLOCK == 0
    tiles_per_seq = seq_len // tm
    n_seq = n // seq_len
    blocks = tm // MOBA_BLOCK
    t_spec, t_shape = _feature_major_spec(n, tm, seq_len)
    row = lambda width: pl.BlockSpec((tm, width), lambda i: (i, 0))
    per_seq = lambda rows: pl.BlockSpec((None, rows, D_RNN), lambda i: (i // tiles_per_seq, 0, 0))
    return pl.pallas_call(
        functools.partial(_prompt_in_proj_kernel, tiles_per_seq),
        grid=(n // tm,),
        in_specs=[row(d_model), _resident((1, d_model)), _resident(w_in_b.shape)]
                 + [_resident(w.shape) for w in rnn_w],
        out_specs=[row(D_ATTN), row(D_ATTN),
                   pl.BlockSpec((blocks, N_HEADS, V_ROWS, MOBA_BLOCK), lambda i: (i, 0, 0, 0)),
                   pl.BlockSpec((blocks, 1, D_ATTN), lambda i: (i, 0, 0)),
                   t_spec, t_spec, row(D_RNN), per_seq(SUBLANES), per_seq(1)],
        out_shape=[jax.ShapeDtypeStruct((n, D_ATTN), F32), jax.ShapeDtypeStruct((n, D_ATTN), BF16),
                   jax.ShapeDtypeStruct((n // MOBA_BLOCK, N_HEADS, V_ROWS, MOBA_BLOCK), BF16),
                   jax.ShapeDtypeStruct((n // MOBA_BLOCK, 1, D_ATTN), F32),
                   t_shape, t_shape, jax.ShapeDtypeStruct((n, D_RNN), BF16),
                   jax.ShapeDtypeStruct((n_seq, SUBLANES, D_RNN), F32),
                   jax.ShapeDtypeStruct((n_seq, 1, D_RNN), F32)],
        scratch_shapes=[pltpu.VMEM((SUBLANES, D_RNN), F32), pltpu.VMEM((1, D_RNN), F32),
                        pltpu.VMEM((tm, D_RNN), F32)],
        compiler_params=_params("arbitrary"),
        name="prompt_in_proj",
    )(x2d, g, w_in_b, *rnn_w)


def _bias_table_kernel(rb_ref, bucket_ref, out_ref):
    bucket = bucket_ref[...]
    for h in range(N_HEADS):
        acc = jnp.full(bucket.shape, NEG_INF, F32)
        for b in range(NUM_BUCKETS):
            acc = jnp.where(bucket == b, rb_ref[b, h] * LOG2E, acc)
        out_ref[h] = acc


def _prompt_bias_tables(rel_bias):
    kr = np.arange(2 * MOBA_BLOCK)[:, None]
    qr = np.arange(MOBA_BLOCK)[None, :]
    dist = qr + MOBA_BLOCK - kr
    bucket = np.where(dist >= 0, _t5_bucket_np(dist), -1).astype(np.int32)
    return pl.pallas_call(
        _bias_table_kernel,
        in_specs=[pl.BlockSpec(memory_space=pltpu.SMEM), pl.BlockSpec(memory_space=pltpu.VMEM)],
        out_specs=pl.BlockSpec(memory_space=pltpu.VMEM),
        out_shape=jax.ShapeDtypeStruct((N_HEADS,) + bucket.shape, F32),
        name="prompt_bias_tables",
    )(rel_bias, jnp.asarray(bucket))


def _sample_bias_kernel(rbt_ref, bucket_ref, out_ref):
    bucket = bucket_ref[...]
    acc = jnp.full(bucket.shape, NEG_INF, F32)
    for b in range(NUM_BUCKETS):
        acc = jnp.where(bucket == b, rbt_ref[:, b:b + 1], acc)
    out_ref[...] = acc


def _sample_bias_table(rel_bias, past_len, t_new, own_width):
    t = np.repeat(np.arange(t_new), N_HEADS)[:, None]
    kpos = np.arange(past_len + own_width)[None, :]
    dist = past_len + t - kpos
    valid = (dist >= 0) & (kpos < past_len + t_new)
    bucket = np.where(valid, _t5_bucket_np(dist), -1).astype(np.int32)
    rbt = jnp.tile(rel_bias.T, (t_new, 1))
    return pl.pallas_call(
        _sample_bias_kernel,
        out_shape=jax.ShapeDtypeStruct(bucket.shape, F32),
        name="sample_bias_table",
    )(rbt, jnp.asarray(bucket))


def _select_rows(scores, n_valid, axis):
    nb = scores.shape[axis]
    idx = lax.broadcasted_iota(jnp.int32, scores.shape, axis)
    beaten = jnp.zeros(scores.shape, jnp.int32)
    for n in range(nb):
        other = lax.slice_in_dim(scores, n, n + 1, axis=axis)
        beats = (other > scores) | ((other == scores) & (n < idx))
        beaten = beaten + jnp.where(beats, jnp.where(n < n_valid, 1, 0), 0)
    return (beaten < MOBA_TOP_K) & (idx < n_valid)


def _prompt_attn_kernel(rb_ref, q_ref, kb_ref, vtb_ref, means_ref, tab_ref, o_ref,
                        mb_ref, sel_ref, qm_ref, m_ref, acc_ref, s0_ref, s1_ref):
    i = pl.program_id(1)
    nb = kb_ref.shape[0]
    prev = jnp.maximum(i - 1, 0)
    pair_rows = 2 * HEAD_DIM
    qt = q_ref[...].T

    means = means_ref[...]
    head_of_lane = lax.broadcasted_iota(jnp.int32, means.shape, 1) // HEAD_DIM
    for h in range(N_HEADS):
        mb_ref[h * nb:(h + 1) * nb, :] = jnp.where(head_of_lane == h, means, 0.0)
    sct = jnp.dot(mb_ref[...], qt, precision=HIGHEST, preferred_element_type=F32)

    row_in_pair = lax.broadcasted_iota(jnp.int32, (pair_rows, MOBA_BLOCK), 0) // HEAD_DIM
    block_id = lax.broadcasted_iota(jnp.int32, (nb, MOBA_BLOCK), 0)
    for h in range(N_HEADS):
        far_bias = rb_ref[NUM_BUCKETS - 1, h] * LOG2E
        selected = _select_rows(sct[h * nb:(h + 1) * nb, :], i, axis=0)
        sel_ref[h * nb:(h + 1) * nb, :] = jnp.where(
            selected, jnp.where(block_id == i - 1, 0.0, far_bias), NEG_INF)
        q_pair = qt[(h // 2) * pair_rows:(h // 2 + 1) * pair_rows, :] * (HEAD_DIM ** -0.5 * LOG2E)
        qm_ref[h] = jnp.where(row_in_pair == h % 2, q_pair, 0.0).astype(BF16)

    def scores(j, table_rows, h, s_ref):
        pair = h // 2
        k_pair = kb_ref[j, :, pair * pair_rows:(pair + 1) * pair_rows]
        s = jnp.dot(k_pair, qm_ref[h], preferred_element_type=F32)
        if table_rows is not None:
            s = s + tab_ref[h, table_rows, :]
        s_ref[h] = s

    def consume(j, first, h, s_ref):
        s = s_ref[h]
        s_max = jnp.max(s, axis=0, keepdims=True)
        if first:
            m_new = s_max
            shift = m_new
        else:
            sel = sel_ref[pl.ds(h * nb + j, 1), :]
            m_old = m_ref[h:h + 1, :]
            m_new = jnp.maximum(m_old, s_max + sel)
            alpha = jnp.exp2(m_old - m_new)
            shift = m_new - sel
        m_ref[h:h + 1, :] = m_new
        p = jnp.exp2(s - shift).astype(BF16)
        pv = jnp.dot(vtb_ref[j, h], p, preferred_element_type=F32)
        acc_ref[h] = pv if first else alpha * acc_ref[h] + pv

    own_rows = slice(MOBA_BLOCK, 2 * MOBA_BLOCK)
    prev_rows = slice(0, MOBA_BLOCK)
    n_far = prev
    last_far = jnp.maximum(n_far - 1, 0)
    far = lambda j: jnp.minimum(j, last_far)
    heads = range(N_HEADS)

    def consume_then_refill(j, first, j_refill, s_ref):
        for h in heads:
            consume(j, first, h, s_ref)
            scores(far(j_refill), None, h, s_ref)

    for h in heads:
        scores(i, own_rows, h, s0_ref)
    for h in heads:
        scores(prev, prev_rows, h, s1_ref)
    consume_then_refill(i, True, 0, s0_ref)
    consume_then_refill(prev, False, 1, s1_ref)

    def far_pair(t, carry):
        consume_then_refill(2 * t, False, 2 * t + 2, s0_ref)
        consume_then_refill(2 * t + 1, False, 2 * t + 3, s1_ref)
        return carry

    lax.fori_loop(0, n_far // 2, far_pair, 0)

    @pl.when(n_far % 2 == 1)
    def _():
        for h in heads:
            consume(n_far - 1, False, h, s0_ref)

    out_t = jnp.concatenate(
        [acc_ref[h, :HEAD_DIM, :] / acc_ref[h, HEAD_DIM:HEAD_DIM + 1, :] for h in heads], axis=0)
    o_ref[...] = out_t.T.astype(o_ref.dtype)


def _prompt_attention(rel_bias, q, kb4, vtb4, means, tables):
    b, nb = kb4.shape[:2]
    t = nb * MOBA_BLOCK
    return pl.pallas_call(
        _prompt_attn_kernel,
        grid=(b, nb),
        in_specs=[pl.BlockSpec(memory_space=pltpu.SMEM),
                  pl.BlockSpec((None, MOBA_BLOCK, D_ATTN), lambda bi, i: (bi, i, 0)),
                  pl.BlockSpec((None, nb, MOBA_BLOCK, D_ATTN), lambda bi, i: (bi, 0, 0, 0)),
                  pl.BlockSpec((None, nb, N_HEADS, V_ROWS, MOBA_BLOCK), lambda bi, i: (bi, 0, 0, 0, 0)),
                  pl.BlockSpec((None, nb, D_ATTN), lambda bi, i: (bi, 0, 0)),
                  _resident(tables.shape)],
        out_specs=pl.BlockSpec((None, MOBA_BLOCK, D_ATTN), lambda bi, i: (bi, i, 0)),
        out_shape=jax.ShapeDtypeStruct((b, t, D_ATTN), BF16),
        scratch_shapes=[pltpu.VMEM((N_HEADS * nb, D_ATTN), F32),
                        pltpu.VMEM((N_HEADS * nb, MOBA_BLOCK), F32),
                        pltpu.VMEM((N_HEADS, 2 * HEAD_DIM, MOBA_BLOCK), BF16),
                        pltpu.VMEM((N_HEADS, MOBA_BLOCK), F32),
                        pltpu.VMEM((N_HEADS, V_ROWS, MOBA_BLOCK), F32),
                        pltpu.VMEM((N_HEADS, MOBA_BLOCK, MOBA_BLOCK), F32),
                        pltpu.VMEM((N_HEADS, MOBA_BLOCK, MOBA_BLOCK), F32)],
        compiler_params=_params("parallel", "arbitrary"),
        name="prompt_attention",
    )(rel_bias, q, kb4, vtb4, means, tables)


def _sample_attn_kernel(n_pages, pt_ref, q_ref, kn_ref, vn_ref, bias_ref, *refs):
    k_pages = refs[:n_pages]
    v_pages = refs[n_pages:2 * n_pages]
    o_ref = refs[2 * n_pages]
    knp_ref, vnp_ref = refs[2 * n_pages + 1:]
    del pt_ref
    t_new, seqs_per_block, _ = q_ref.shape
    page = k_pages[0].shape[1]
    pages_per_block = MOBA_BLOCK // page
    nb = n_pages // pages_per_block
    past_len = n_pages * page
    row = pl.ds(pl.program_id(0) % seqs_per_block, 1)

    head_of_row = lax.broadcasted_iota(jnp.int32, (N_HEADS, D_ATTN), 0)
    head_of_lane = lax.broadcasted_iota(jnp.int32, (N_HEADS, D_ATTN), 1) // HEAD_DIM
    head_mask = head_of_row == head_of_lane
    qrows = jnp.concatenate(
        [jnp.where(head_mask, jnp.broadcast_to(q_ref[t, row, :], (N_HEADS, D_ATTN)), 0.0) for t in range(t_new)],
        axis=0)

    lane = lax.broadcasted_iota(jnp.int32, (D_ATTN, page), 1)
    means_t = jnp.zeros((D_ATTN, page), F32)
    for n in range(nb):
        total = sum(k_pages[n * pages_per_block + r][...] for r in range(pages_per_block))
        means_t = jnp.where(lane == n, jnp.sum(total, axis=1, keepdims=True) * (1.0 / MOBA_BLOCK), means_t)
    scores = jnp.dot(qrows, means_t, precision=HIGHEST, preferred_element_type=F32)[:, :nb]
    selected = jnp.where(_select_rows(scores, nb, axis=1), 1.0, 0.0)

    qb = (qrows * (HEAD_DIM ** -0.5)).astype(BF16)
    s_pages = []
    for p in range(n_pages):
        s = jnp.dot(qb, k_pages[p][...].astype(BF16), preferred_element_type=F32)
        keep = selected[:, p // pages_per_block:p // pages_per_block + 1] > 0.5
        s_pages.append(jnp.where(keep, s + bias_ref[:, p * page:(p + 1) * page], NEG_INF))
    s_past = jnp.concatenate(s_pages, axis=1)

    knp_ref[...] = jnp.zeros(knp_ref.shape, F32)
    vnp_ref[...] = jnp.zeros(vnp_ref.shape, F32)
    for t in range(t_new):
        knp_ref[t:t + 1, :] = kn_ref[t, row, :]
        vnp_ref[t:t + 1, :] = vn_ref[t, row, :]
    nt = (((1,), (1,)), ((), ()))
    s_own = (lax.dot_general(qb, knp_ref[...].astype(BF16), nt, preferred_element_type=F32)
             + bias_ref[:, past_len:])

    m = jnp.maximum(jnp.max(s_past, axis=1, keepdims=True), jnp.max(s_own, axis=1, keepdims=True))
    p_past = jnp.exp(s_past - m)
    p_own = jnp.exp(s_own - m)
    l = jnp.sum(p_past, axis=1, keepdims=True) + jnp.sum(p_own, axis=1, keepdims=True)
    o = jnp.dot(p_own.astype(BF16), vnp_ref[...].astype(BF16), preferred_element_type=F32)
    for p in range(n_pages):
        o = o + lax.dot_general(p_past[:, p * page:(p + 1) * page].astype(BF16), v_pages[p][...].astype(BF16), nt,
                                preferred_element_type=F32)
    o = o / l
    for t in range(t_new):
        o_ref[t, row, :] = jnp.sum(jnp.where(head_mask, o[t * N_HEADS:(t + 1) * N_HEADS, :], 0.0),
                                   axis=0, keepdims=True)


def _sample_attention(page_table, q, k_new, v_new, bias, cache_kt, cache_vt):
    t_new, bs, _ = q.shape
    n_pages = page_table.shape[1]
    page = cache_kt.shape[2]
    own_width = bias.shape[1] - n_pages * page
    seqs_per_block = SUBLANES
    assert bs % seqs_per_block == 0
    per_seq = pl.BlockSpec((t_new, seqs_per_block, D_ATTN), lambda s, pt: (0, s // seqs_per_block, 0))
    page_spec = lambda p: pl.BlockSpec((None, D_ATTN, page), lambda s, pt: (pt[s, p], 0, 0))
    grid_spec = pltpu.PrefetchScalarGridSpec(
        num_scalar_prefetch=1,
        grid=(bs,),
        in_specs=[per_seq, per_seq, per_seq, pl.BlockSpec(bias.shape, lambda s, pt: (0, 0))]
                 + [page_spec(p) for p in range(n_pages)] * 2,
        out_specs=per_seq,
        scratch_shapes=[pltpu.VMEM((own_width, D_ATTN), F32), pltpu.VMEM((own_width, D_ATTN), F32)],
    )
    return pl.pallas_call(
        functools.partial(_sample_attn_kernel, n_pages),
        grid_spec=grid_spec,
        out_shape=jax.ShapeDtypeStruct((t_new, bs, D_ATTN), F32),
        compiler_params=_params("arbitrary"),
        name="sample_attention",
    )(page_table, q, k_new, v_new, bias, *([cache_kt] * n_pages), *([cache_vt] * n_pages))


def _tail_kernel(chunk, x_ref, oa_ref, orn_ref, gpre_ref, wg_ref, wpa_ref, wpr_ref, wo_ref, gpost_ref,
                 fpre_ref, wgu_ref, wd_ref, fpost_ref, y_ref):
    d_model = x_ref.shape[-1]
    d_ff = wd_ref.shape[0]
    x = x_ref[...]
    hb = _rms(x, gpre_ref[...]).astype(BF16)
    pa = jnp.dot(oa_ref[...].astype(BF16), wpa_ref[...], preferred_element_type=F32)
    merged = jax.nn.sigmoid(jnp.dot(hb, wg_ref[:, :d_model], preferred_element_type=F32)) * pa
    pr = jnp.dot(orn_ref[...].astype(BF16), wpr_ref[...], preferred_element_type=F32)
    merged = merged + jax.nn.sigmoid(jnp.dot(hb, wg_ref[:, d_model:], preferred_element_type=F32)) * pr
    x1 = x + _rms(jnp.dot(merged.astype(BF16), wo_ref[...], preferred_element_type=F32), gpost_ref[...])

    fb = _rms(x1, fpre_ref[...]).astype(BF16)
    y = jnp.zeros(x.shape, F32)
    for c in range(0, d_ff, chunk):
        gate = jnp.dot(fb, wgu_ref[:, c:c + chunk], preferred_element_type=F32)
        up = jnp.dot(fb, wgu_ref[:, d_ff + c:d_ff + c + chunk], preferred_element_type=F32)
        act = (jax.nn.silu(gate) * up).astype(BF16)
        y = y + jnp.dot(act, wd_ref[c:c + chunk, :], preferred_element_type=F32)
    y_ref[...] = x1 + _rms(y, fpost_ref[...])


def _tail(x2d, oa, orn, weights, tm, chunk):
    n, d_model = x2d.shape
    assert n % tm == 0 and weights[8].shape[0] % chunk == 0
    row = lambda width: pl.BlockSpec((tm, width), lambda i: (i, 0))
    return pl.pallas_call(
        functools.partial(_tail_kernel, chunk),
        grid=(n // tm,),
        in_specs=[row(d_model), row(D_ATTN), row(D_RNN)] + [_resident(w.shape) for w in weights],
        out_specs=row(d_model),
        out_shape=jax.ShapeDtypeStruct((n, d_model), F32),
        compiler_params=_params("parallel"),
        name="tail",
    )(x2d, oa, orn, *weights)


def _block_diag(w):
    nblk, c, _ = w.shape
    eye = jnp.eye(nblk, dtype=w.dtype)
    return (w[:, :, None, :] * eye[:, None, :, None]).reshape(nblk * c, nblk * c)


def _rows_from_feature_major(xt):
    g, _, w = xt.shape
    return jnp.transpose(xt.reshape(g, N_HEADS, HEAD_DIM, w), (3, 0, 1, 2))[None]


def kernel(x_prompt, x_sample, cache_k, cache_v, page_table, state_conv, state_h, norm_mix_pre, norm_mix_post,
           w_in, rel_bias, conv_w, conv_b, rg_w_r, rg_b_r, rg_w_i, rg_b_i, rg_lambda, w_proj_attn, w_proj_rnn,
           w_out, norm_ffn_pre, norm_ffn_post, w_gate_up, w_down):
    depth = w_in.shape[0]
    assert depth == 1
    bp, tp, d_model = x_prompt.shape
    bs, ts, _ = x_sample.shape
    n_phys, page = cache_k.shape[1:3]
    n_pages = page_table.shape[1]
    past_len = n_pages * page
    assert tp % MOBA_BLOCK == 0 and past_len % MOBA_BLOCK == 0 and MOBA_BLOCK % page == 0
    assert past_len // MOBA_BLOCK > MOBA_TOP_K and CONV_W - 1 <= ts <= MOBA_BLOCK
    nbp = tp // MOBA_BLOCK
    l = 0
    row2 = lambda v: v.reshape(1, -1)

    n_stream_cols = 3 * D_ATTN + 2 * D_RNN
    w_in_b = w_in[l, :, :n_stream_cols].astype(BF16)
    w_gates = w_in[l, :, n_stream_cols:].astype(BF16)
    rnn_w = (conv_w[l], row2(conv_b[l]), _block_diag(rg_w_r[l]).astype(BF16), row2(rg_b_r[l]),
             _block_diag(rg_w_i[l]).astype(BF16), row2(rg_b_i[l]), row2(rg_lambda[l]))
    wpa, wpr, wo = w_proj_attn[l].astype(BF16), w_proj_rnn[l].astype(BF16), w_out[l].astype(BF16)
    wgu, wd = w_gate_up[l].astype(BF16), w_down[l].astype(BF16)
    g_pre, g_post = row2(norm_mix_pre[l]), row2(norm_mix_post[l])
    gf_pre, gf_post = row2(norm_ffn_pre[l]), row2(norm_ffn_post[l])

    tail_w = (g_pre, w_gates, wpa, wpr, wo, g_post, gf_pre, wgu, wd, gf_post)

    tm = 512
    xp2 = x_prompt.reshape(bp * tp, d_model)
    q, kb, vtb, means, kt, vt, orn, xtail, h_last = _prompt_in_proj(xp2, g_pre, w_in_b, rnn_w, tm, tp)
    tables = _prompt_bias_tables(rel_bias)
    oa = _prompt_attention(rel_bias, q.reshape(bp, tp, D_ATTN), kb.reshape(bp, nbp, MOBA_BLOCK, D_ATTN),
                           vtb.reshape(bp, nbp, N_HEADS, V_ROWS, MOBA_BLOCK), means.reshape(bp, nbp, D_ATTN), tables)
    y_prompt = _tail(xp2, oa.reshape(bp * tp, D_ATTN), orn, tail_w, tm, 256)
    y_prompt = y_prompt.reshape(bp, tp, d_model)
    new_k_prompt = jnp.swapaxes(_rows_from_feature_major(kt), 1, 2)
    new_v_prompt = jnp.swapaxes(_rows_from_feature_major(vt), 1, 2)
    new_conv_prompt = xtail[:, SUBLANES - (CONV_W - 1):, :][None]
    new_h_prompt = h_last.reshape(1, bp, D_RNN).astype(state_h.dtype)

    ns = bs * ts
    xs2 = jnp.swapaxes(x_sample, 0, 1).reshape(ns, d_model)
    qs, ks, vs, kts, vts, xrs, ggs = _sample_in_proj(xs2, g_pre, w_in_b, bs)
    own_width = 128
    bias_s = _sample_bias_table(rel_bias, past_len, ts, own_width)
    feature_major_pages = lambda c: jnp.transpose(c, (0, 2, 3, 1)).reshape(n_phys, D_ATTN, page)
    tm3 = lambda a: a.reshape(ts, bs, a.shape[-1])
    oas = _sample_attention(page_table, tm3(qs), tm3(ks), tm3(vs), bias_s,
                            feature_major_pages(cache_k[l]), feature_major_pages(cache_v[l]))
    xrs3 = tm3(xrs)
    orns, hs_last = _sample_rnn(xrs3, tm3(ggs), jnp.swapaxes(state_conv[l], 0, 1), state_h[l], rnn_w)
    y_sample = _tail(xs2, oas.reshape(ns, D_ATTN), orns.reshape(ns, D_RNN), tail_w, ns, 256)
    y_sample = jnp.swapaxes(y_sample.reshape(ts, bs, d_model), 0, 1)
    new_k_sample = _rows_from_feature_major(kts)
    new_v_sample = _rows_from_feature_major(vts)
    new_conv_sample = jnp.swapaxes(xrs3[ts - (CONV_W - 1):], 0, 1)[None]
    new_h_sample = hs_last[None].astype(state_h.dtype)

    return (y_prompt, y_sample, new_k_prompt, new_v_prompt, new_k_sample, new_v_sample,
            new_conv_prompt, new_h_prompt, new_conv_sample, new_h_sample)
```

```python
import functools
import math

import numpy as np
import jax
import jax.numpy as jnp
from jax import lax
from jax.experimental import pallas as pl
from jax.experimental.pallas import tpu as pltpu

N_HEADS = 8
HEAD_DIM = 64
D_ATTN = N_HEADS * HEAD_DIM
D_RNN = 512
N_RG_BLOCKS = 8
CONV_W = 4
RG_C = 8.0
MOBA_BLOCK = 256
MOBA_TOP_K = 3
NUM_BUCKETS = 32
MAX_DISTANCE = 128
RMS_EPS = 1e-6
NEG_INF = -1e30
BF16_SUBLANES = 16
V_ROWS = HEAD_DIM + BF16_SUBLANES
LOG2E = math.log2(math.e)

SUBLANES = 8
VMEM_LIMIT = 56 * 1024 * 1024
F32 = jnp.float32
BF16 = jnp.bfloat16
HIGHEST = lax.Precision.HIGHEST


def _params(*sem):
    return pltpu.CompilerParams(dimension_semantics=sem, vmem_limit_bytes=VMEM_LIMIT)


def _resident(shape):
    nd = len(shape)
    return pl.BlockSpec(shape, lambda *_: (0,) * nd, pipeline_mode=pl.Buffered(1))


def _rms(x, g):
    return x * lax.rsqrt(jnp.mean(x * x, axis=-1, keepdims=True) + RMS_EPS) * g


def _t5_bucket_np(n):
    n = np.maximum(n, 0)
    max_exact = NUM_BUCKETS // 2
    nf = np.maximum(n, 1).astype(np.float32)
    large = max_exact + (np.log(nf / np.float32(max_exact)) / np.float32(math.log(MAX_DISTANCE / max_exact))
                         * np.float32(NUM_BUCKETS - max_exact)).astype(np.int32)
    large = np.minimum(large, NUM_BUCKETS - 1)
    return np.where(n < max_exact, n, large).astype(np.int32)


def _rglru_coeffs(xc, wr_ref, br_ref, wi_ref, bi_ref, lam_ref):
    xb = xc.astype(BF16)
    r = jax.nn.sigmoid(jnp.dot(xb, wr_ref[...], preferred_element_type=F32) + br_ref[...])
    i = jax.nn.sigmoid(jnp.dot(xb, wi_ref[...], preferred_element_type=F32) + bi_ref[...])
    log_a = -RG_C * r * jax.nn.softplus(-lam_ref[...])
    a = jnp.exp(log_a)
    z = -jnp.tanh(log_a) * (a * a + 1.0)
    b = jnp.where(z > 0.0, z * lax.rsqrt(z), 0.0) * (i * xc)
    return a, b


def _rglru_tile(x, gg, first_tile, cw_ref, cb_ref, wr_ref, br_ref, wi_ref, bi_ref, lam_ref,
                o_ref, hl_ref, xbuf_ref, h_ref, obuf_ref):
    tt = x.shape[0]

    @pl.when(first_tile)
    def _():
        xbuf_ref[...] = jnp.zeros(xbuf_ref.shape, F32)
        h_ref[...] = jnp.zeros(h_ref.shape, F32)

    x_ext = jnp.concatenate([xbuf_ref[...], x], axis=0)
    row = lax.broadcasted_iota(jnp.int32, (tt, D_RNN), 0) % SUBLANES
    xc = cb_ref[...] + x * cw_ref[CONV_W - 1:CONV_W, :]
    for s in range(1, CONV_W):
        xc = xc + pltpu.roll(x_ext, s, axis=0)[SUBLANES:] * cw_ref[CONV_W - 1 - s:CONV_W - s, :]
    xbuf_ref[...] = x[tt - SUBLANES:]
    a, b = _rglru_coeffs(xc, wr_ref, br_ref, wi_ref, bi_ref, lam_ref)

    groups = (tt // SUBLANES, SUBLANES, D_RNN)
    a, b, row = a.reshape(groups), b.reshape(groups), row.reshape(groups)
    shift = 1
    while shift < SUBLANES:
        a_prev = pltpu.roll(a, shift, axis=1)
        b_prev = pltpu.roll(b, shift, axis=1)
        ok = row >= shift
        b = jnp.where(ok, a * b_prev + b, b)
        a = jnp.where(ok, a * a_prev, a)
        shift *= 2
    h = h_ref[...]
    for g in range(tt // SUBLANES):
        sl = slice(g * SUBLANES, (g + 1) * SUBLANES)
        hg = a[g] * h + b[g]
        obuf_ref[sl, :] = hg * gg[sl, :]
        h = hg[SUBLANES - 1:SUBLANES, :]
    o_ref[...] = obuf_ref[...].astype(o_ref.dtype)
    h_ref[...] = h
    hl_ref[...] = h


def _sample_rnn_kernel(xr_ref, gg_ref, conv_ref, h0_ref, cw_ref, cb_ref, wr_ref, br_ref, wi_ref, bi_ref, lam_ref,
                       o_ref, hl_ref):
    t_new = xr_ref.shape[0]
    past = [conv_ref[j] for j in range(CONV_W - 1)] + [xr_ref[t] for t in range(t_new)]
    h = h0_ref[...]
    for t in range(t_new):
        xc = cb_ref[...] + sum(past[t + j] * cw_ref[j:j + 1, :] for j in range(CONV_W))
        a, b = _rglru_coeffs(xc, wr_ref, br_ref, wi_ref, bi_ref, lam_ref)
        h = a * h + b
        o_ref[t] = h * gg_ref[t]
    hl_ref[...] = h


def _sample_rnn(xr_t, gg_t, conv_t, h0, weights):
    return pl.pallas_call(
        _sample_rnn_kernel,
        out_shape=[jax.ShapeDtypeStruct(xr_t.shape, F32), jax.ShapeDtypeStruct(h0.shape, F32)],
        compiler_params=pltpu.CompilerParams(vmem_limit_bytes=VMEM_LIMIT),
        name="sample_rnn",
    )(xr_t, gg_t, conv_t, h0, *weights)


def _projector(x_ref, g_ref, w_ref):
    hb = _rms(x_ref[...], g_ref[...]).astype(BF16)
    return lambda lo, width: jnp.dot(hb, w_ref[:, lo:lo + width], preferred_element_type=F32)


def _store_feature_major(kt_ref, vt_ref, k, v):
    groups, _, width = kt_ref.shape
    for r in range(groups):
        kt_ref[r] = k[r * width:(r + 1) * width, :].T
        vt_ref[r] = v[r * width:(r + 1) * width, :].T


def _sample_in_proj_kernel(x_ref, g_ref, w_ref, q_ref, k_ref, v_ref, kt_ref, vt_ref, xr_ref, gg_ref):
    proj = _projector(x_ref, g_ref, w_ref)
    q_ref[...] = proj(0, D_ATTN)
    k = proj(D_ATTN, D_ATTN)
    v = proj(2 * D_ATTN, D_ATTN)
    k_ref[...] = k
    v_ref[...] = v
    _store_feature_major(kt_ref, vt_ref, k, v)
    xr_ref[...] = proj(3 * D_ATTN, D_RNN)
    gg_ref[...] = jax.nn.gelu(proj(3 * D_ATTN + D_RNN, D_RNN))


def _prompt_in_proj_kernel(tiles_per_seq, x_ref, g_ref, w_ref, *refs):
    rnn_w = refs[:7]
    q_ref, kb_ref, vtb_ref, means_ref, kt_ref, vt_ref, orn_ref, xtail_ref, hl_ref = refs[7:16]
    xbuf_ref, h_ref, obuf_ref = refs[16:]
    proj = _projector(x_ref, g_ref, w_ref)
    xr = proj(3 * D_ATTN, D_RNN)
    gg = jax.nn.gelu(proj(3 * D_ATTN + D_RNN, D_RNN))
    xtail_ref[...] = xr[xr.shape[0] - SUBLANES:]
    _rglru_tile(xr, gg, pl.program_id(0) % tiles_per_seq == 0, *rnn_w, orn_ref, hl_ref, xbuf_ref, h_ref, obuf_ref)
    q_ref[...] = proj(0, D_ATTN)
    k = proj(D_ATTN, D_ATTN)
    v = proj(2 * D_ATTN, D_ATTN)
    _store_feature_major(kt_ref, vt_ref, k, v)
    kb_ref[...] = k.astype(BF16)
    pad_row = lax.broadcasted_iota(jnp.int32, (BF16_SUBLANES, MOBA_BLOCK), 0)
    ones_rows = jnp.where(pad_row == 0, 1.0, 0.0).astype(BF16)
    for r in range(vtb_ref.shape[0]):
        blk = slice(r * MOBA_BLOCK, (r + 1) * MOBA_BLOCK)
        v_t = v[blk, :].T.astype(BF16)
        for h in range(N_HEADS):
            vtb_ref[r, h] = jnp.concatenate([v_t[h * HEAD_DIM:(h + 1) * HEAD_DIM], ones_rows], axis=0)
        means_ref[r] = jnp.sum(k[blk, :], axis=0, keepdims=True) * (1.0 / MOBA_BLOCK)


def _feature_major_spec(n, tm, group_len):
    tiles_per_group = max(group_len // tm, 1)
    block = (max(tm // group_len, 1), D_ATTN, min(tm, group_len))
    spec = pl.BlockSpec(block, lambda i: (i // tiles_per_group, 0, i % tiles_per_group))
    return spec, jax.ShapeDtypeStruct((n // group_len, D_ATTN, group_len), F32)


def _sample_in_proj(x2d, g, w_in_b, group_len):
    n, d_model = x2d.shape
    assert n % group_len == 0
    t_spec, t_shape = _feature_major_spec(n, n, group_len)
    row = lambda width: pl.BlockSpec((n, width), lambda i: (i, 0))
    f32 = lambda width: jax.ShapeDtypeStruct((n, width), F32)
    return pl.pallas_call(
        _sample_in_proj_kernel,
        grid=(1,),
        in_specs=[row(d_model), _resident((1, d_model)), _resident(w_in_b.shape)],
        out_specs=[row(D_ATTN)] * 3 + [t_spec, t_spec, row(D_RNN), row(D_RNN)],
        out_shape=[f32(D_ATTN)] * 3 + [t_shape, t_shape, f32(D_RNN), f32(D_RNN)],
        compiler_params=_params("arbitrary"),
        name="sample_in_proj",
    )(x2d, g, w_in_b)


def _prompt_in_proj(x2d, g, w_in_b, rnn_w, tm, seq_len):
    n, d_model = x2d.shape
    assert n % seq_len == 0 and seq_len % tm == 0 and tm % MOBA_BLOCK == 0
    tiles_per_seq = seq_len // tm
    n_seq = n // seq_len
    blocks = tm // MOBA_BLOCK
    t_spec, t_shape = _feature_major_spec(n, tm, seq_len)
    row = lambda width: pl.BlockSpec((tm, width), lambda i: (i, 0))
    per_seq = lambda rows: pl.BlockSpec((None, rows, D_RNN), lambda i: (i // tiles_per_seq, 0, 0))
    return pl.pallas_call(
        functools.partial(_prompt_in_proj_kernel, tiles_per_seq),
        grid=(n // tm,),
        in_specs=[row(d_model), _resident((1, d_model)), _resident(w_in_b.shape)]
                 + [_resident(w.shape) for w in rnn_w],
        out_specs=[row(D_ATTN), row(D_ATTN),
                   pl.BlockSpec((blocks, N_HEADS, V_ROWS, MOBA_BLOCK), lambda i: (i, 0, 0, 0)),
                   pl.BlockSpec((blocks, 1, D_ATTN), lambda i: (i, 0, 0)),
                   t_spec, t_spec, row(D_RNN), per_seq(SUBLANES), per_seq(1)],
        out_shape=[jax.ShapeDtypeStruct((n, D_ATTN), F32), jax.ShapeDtypeStruct((n, D_ATTN), BF16),
                   jax.ShapeDtypeStruct((n // MOBA_BLOCK, N_HEADS, V_ROWS, MOBA_BLOCK), BF16),
                   jax.ShapeDtypeStruct((n // MOBA_BLOCK, 1, D_ATTN), F32),
                   t_shape, t_shape, jax.ShapeDtypeStruct((n, D_RNN), BF16),
                   jax.ShapeDtypeStruct((n_seq, SUBLANES, D_RNN), F32),
                   jax.ShapeDtypeStruct((n_seq, 1, D_RNN), F32)],
        scratch_shapes=[pltpu.VMEM((SUBLANES, D_RNN), F32), pltpu.VMEM((1, D_RNN), F32),
                        pltpu.VMEM((tm, D_RNN), F32)],
        compiler_params=_params("arbitrary"),
        name="prompt_in_proj",
    )(x2d, g, w_in_b, *rnn_w)


def _bias_table_kernel(rb_ref, bucket_ref, out_ref):
    bucket = bucket_ref[...]
    for h in range(N_HEADS):
        acc = jnp.full(bucket.shape, NEG_INF, F32)
        for b in range(NUM_BUCKETS):
            acc = jnp.where(bucket == b, rb_ref[b, h] * LOG2E, acc)
        out_ref[h] = acc


def _prompt_bias_tables(rel_bias):
    kr = np.arange(2 * MOBA_BLOCK)[:, None]
    qr = np.arange(MOBA_BLOCK)[None, :]
    dist = qr + MOBA_BLOCK - kr
    bucket = np.where(dist >= 0, _t5_bucket_np(dist), -1).astype(np.int32)
    return pl.pallas_call(
        _bias_table_kernel,
        in_specs=[pl.BlockSpec(memory_space=pltpu.SMEM), pl.BlockSpec(memory_space=pltpu.VMEM)],
        out_specs=pl.BlockSpec(memory_space=pltpu.VMEM),
        out_shape=jax.ShapeDtypeStruct((N_HEADS,) + bucket.shape, F32),
        name="prompt_bias_tables",
    )(rel_bias, jnp.asarray(bucket))


def _sample_bias_kernel(rbt_ref, bucket_ref, out_ref):
    bucket = bucket_ref[...]
    acc = jnp.full(bucket.shape, NEG_INF, F32)
    for b in range(NUM_BUCKETS):
        acc = jnp.where(bucket == b, rbt_ref[:, b:b + 1], acc)
    out_ref[...] = acc


def _sample_bias_table(rel_bias, past_len, t_new, own_width):
    t = np.repeat(np.arange(t_new), N_HEADS)[:, None]
    kpos = np.arange(past_len + own_width)[None, :]
    dist = past_len + t - kpos
    valid = (dist >= 0) & (kpos < past_len + t_new)
    bucket = np.where(valid, _t5_bucket_np(dist), -1).astype(np.int32)
    rbt = jnp.tile(rel_bias.T, (t_new, 1))
    return pl.pallas_call(
        _sample_bias_kernel,
        out_shape=jax.ShapeDtypeStruct(bucket.shape, F32),
        name="sample_bias_table",
    )(rbt, jnp.asarray(bucket))


def _select_rows(scores, n_valid, axis):
    nb = scores.shape[axis]
    idx = lax.broadcasted_iota(jnp.int32, scores.shape, axis)
    beaten = jnp.zeros(scores.shape, jnp.int32)
    for n in range(nb):
        other = lax.slice_in_dim(scores, n, n + 1, axis=axis)
        beats = (other > scores) | ((other == scores) & (n < idx))
        beaten = beaten + jnp.where(beats, jnp.where(n < n_valid, 1, 0), 0)
    return (beaten < MOBA_TOP_K) & (idx < n_valid)


def _prompt_attn_kernel(rb_ref, q_ref, kb_ref, vtb_ref, means_ref, tab_ref, o_ref,
                        mb_ref, sel_ref, qm_ref, m_ref, acc_ref, s0_ref, s1_ref):
    i = pl.program_id(1)
    nb = kb_ref.shape[0]
    prev = jnp.maximum(i - 1, 0)
    pair_rows = 2 * HEAD_DIM
    qt = q_ref[...].T

    means = means_ref[...]
    head_of_lane = lax.broadcasted_iota(jnp.int32, means.shape, 1) // HEAD_DIM
    for h in range(N_HEADS):
        mb_ref[h * nb:(h + 1) * nb, :] = jnp.where(head_of_lane == h, means, 0.0)
    sct = jnp.dot(mb_ref[...], qt, precision=HIGHEST, preferred_element_type=F32)

    row_in_pair = lax.broadcasted_iota(jnp.int32, (pair_rows, MOBA_BLOCK), 0) // HEAD_DIM
    block_id = lax.broadcasted_iota(jnp.int32, (nb, MOBA_BLOCK), 0)
    for h in range(N_HEADS):
        far_bias = rb_ref[NUM_BUCKETS - 1, h] * LOG2E
        selected = _select_rows(sct[h * nb:(h + 1) * nb, :], i, axis=0)
        sel_ref[h * nb:(h + 1) * nb, :] = jnp.where(
            selected, jnp.where(block_id == i - 1, 0.0, far_bias), NEG_INF)
        q_pair = qt[(h // 2) * pair_rows:(h // 2 + 1) * pair_rows, :] * (HEAD_DIM ** -0.5 * LOG2E)
        qm_ref[h] = jnp.where(row_in_pair == h % 2, q_pair, 0.0).astype(BF16)

    def scores(j, table_rows, h, s_ref):
        pair = h // 2
        k_pair = kb_ref[j, :, pair * pair_rows:(pair + 1) * pair_rows]
        s = jnp.dot(k_pair, qm_ref[h], preferred_element_type=F32)
        if table_rows is not None:
            s = s + tab_ref[h, table_rows, :]
        s_ref[h] = s

    def consume(j, first, h, s_ref):
        s = s_ref[h]
        s_max = jnp.max(s, axis=0, keepdims=True)
        if first:
            m_new = s_max
            shift = m_new
        else:
            sel = sel_ref[pl.ds(h * nb + j, 1), :]
            m_old = m_ref[h:h + 1, :]
            m_new = jnp.maximum(m_old, s_max + sel)
            alpha = jnp.exp2(m_old - m_new)
            shift = m_new - sel
        m_ref[h:h + 1, :] = m_new
        p = jnp.exp2(s - shift).astype(BF16)
        pv = jnp.dot(vtb_ref[j, h], p, preferred_element_type=F32)
        acc_ref[h] = pv if first else alpha * acc_ref[h] + pv

    own_rows = slice(MOBA_BLOCK, 2 * MOBA_BLOCK)
    prev_rows = slice(0, MOBA_BLOCK)
    n_far = prev
    last_far = jnp.maximum(n_far - 1, 0)
    far = lambda j: jnp.minimum(j, last_far)
    heads = range(N_HEADS)

    def consume_then_refill(j, first, j_refill, s_ref):
        for h in heads:
            consume(j, first, h, s_ref)
            scores(far(j_refill), None, h, s_ref)

    for h in heads:
        scores(i, own_rows, h, s0_ref)
    for h in heads:
        scores(prev, prev_rows, h, s1_ref)
    consume_then_refill(i, True, 0, s0_ref)
    consume_then_refill(prev, False, 1, s1_ref)

    def far_pair(t, carry):
        consume_then_refill(2 * t, False, 2 * t + 2, s0_ref)
        consume_then_refill(2 * t + 1, False, 2 * t + 3, s1_ref)
        return carry

    lax.fori_loop(0, n_far // 2, far_pair, 0)

    @pl.when(n_far % 2 == 1)
    def _():
        for h in heads:
            consume(n_far - 1, False, h, s0_ref)

    out_t = jnp.concatenate(
        [acc_ref[h, :HEAD_DIM, :] / acc_ref[h, HEAD_DIM:HEAD_DIM + 1, :] for h in heads], axis=0)
    o_ref[...] = out_t.T.astype(o_ref.dtype)


def _prompt_attention(rel_bias, q, kb4, vtb4, means, tables):
    b, nb = kb4.shape[:2]
    t = nb * MOBA_BLOCK
    return pl.pallas_call(
        _prompt_attn_kernel,
        grid=(b, nb),
        in_specs=[pl.BlockSpec(memory_space=pltpu.SMEM),
                  pl.BlockSpec((None, MOBA_BLOCK, D_ATTN), lambda bi, i: (bi, i, 0)),
                  pl.BlockSpec((None, nb, MOBA_BLOCK, D_ATTN), lambda bi, i: (bi, 0, 0, 0)),
                  pl.BlockSpec((None, nb, N_HEADS, V_ROWS, MOBA_BLOCK), lambda bi, i: (bi, 0, 0, 0, 0)),
                  pl.BlockSpec((None, nb, D_ATTN), lambda bi, i: (bi, 0, 0)),
                  _resident(tables.shape)],
        out_specs=pl.BlockSpec((None, MOBA_BLOCK, D_ATTN), lambda bi, i: (bi, i, 0)),
        out_shape=jax.ShapeDtypeStruct((b, t, D_ATTN), BF16),
        scratch_shapes=[pltpu.VMEM((N_HEADS * nb, D_ATTN), F32),
                        pltpu.VMEM((N_HEADS * nb, MOBA_BLOCK), F32),
                        pltpu.VMEM((N_HEADS, 2 * HEAD_DIM, MOBA_BLOCK), BF16),
                        pltpu.VMEM((N_HEADS, MOBA_BLOCK), F32),
                        pltpu.VMEM((N_HEADS, V_ROWS, MOBA_BLOCK), F32),
                        pltpu.VMEM((N_HEADS, MOBA_BLOCK, MOBA_BLOCK), F32),
                        pltpu.VMEM((N_HEADS, MOBA_BLOCK, MOBA_BLOCK), F32)],
        compiler_params=_params("parallel", "arbitrary"),
        name="prompt_attention",
    )(rel_bias, q, kb4, vtb4, means, tables)


def _sample_attn_kernel(pt_ref, q_ref, kn_ref, vn_ref, bias_ref, kt_hbm, vt_hbm, o_ref,
                        kbuf_ref, vbuf_ref, knp_ref, vnp_ref, sem):
    step = pl.program_id(0)
    t_new, seqs_per_step, _ = q_ref.shape
    n_slots, n_pages, _, page = kbuf_ref.shape
    n_seq = pl.num_programs(0) * seqs_per_step
    pages_per_block = MOBA_BLOCK // page
    nb = n_pages // pages_per_block
    past_len = n_pages * page

    def page_copies(seq):
        slot = seq % n_slots
        copies = []
        for p in range(n_pages):
            phys = pt_ref[seq, p]
            copies.append(pltpu.make_async_copy(kt_hbm.at[phys], kbuf_ref.at[slot, p], sem.at[slot, 0]))
            copies.append(pltpu.make_async_copy(vt_hbm.at[phys], vbuf_ref.at[slot, p], sem.at[slot, 1]))
        return copies

    @pl.when(step == 0)
    def _():
        for copy in page_copies(0):
            copy.start()

    head_of_row = lax.broadcasted_iota(jnp.int32, (N_HEADS, D_ATTN), 0)
    head_of_lane = lax.broadcasted_iota(jnp.int32, (N_HEADS, D_ATTN), 1) // HEAD_DIM
    head_mask = head_of_row == head_of_lane
    lane = lax.broadcasted_iota(jnp.int32, (D_ATTN, page), 1)
    nt = (((1,), (1,)), ((), ()))

    def one_sequence(r, carry):
        seq = step * seqs_per_step + r
        slot = seq % n_slots
        row = pl.ds(r, 1)

        @pl.when(seq + 1 < n_seq)
        def _():
            for copy in page_copies(seq + 1):
                copy.start()

        for copy in page_copies(seq):
            copy.wait()

        qrows = jnp.concatenate(
            [jnp.where(head_mask, jnp.broadcast_to(q_ref[t, row, :], (N_HEADS, D_ATTN)), 0.0)
             for t in range(t_new)], axis=0)

        means_t = jnp.zeros((D_ATTN, page), F32)
        for n in range(nb):
            total = sum(kbuf_ref[slot, n * pages_per_block + i] for i in range(pages_per_block))
            means_t = jnp.where(lane == n, jnp.sum(total, axis=1, keepdims=True) * (1.0 / MOBA_BLOCK), means_t)
        scores = jnp.dot(qrows, means_t, precision=HIGHEST, preferred_element_type=F32)[:, :nb]
        selected = jnp.where(_select_rows(scores, nb, axis=1), 1.0, 0.0)

        qb = (qrows * (HEAD_DIM ** -0.5)).astype(BF16)
        s_pages = []
        for p in range(n_pages):
            s = jnp.dot(qb, kbuf_ref[slot, p].astype(BF16), preferred_element_type=F32)
            keep = selected[:, p // pages_per_block:p // pages_per_block + 1] > 0.5
            s_pages.append(jnp.where(keep, s + bias_ref[:, p * page:(p + 1) * page], NEG_INF))
        s_past = jnp.concatenate(s_pages, axis=1)

        knp_ref[...] = jnp.zeros(knp_ref.shape, F32)
        vnp_ref[...] = jnp.zeros(vnp_ref.shape, F32)
        for t in range(t_new):
            knp_ref[t:t + 1, :] = kn_ref[t, row, :]
            vnp_ref[t:t + 1, :] = vn_ref[t, row, :]
        s_own = (lax.dot_general(qb, knp_ref[...].astype(BF16), nt, preferred_element_type=F32)
                 + bias_ref[:, past_len:])

        m = jnp.maximum(jnp.max(s_past, axis=1, keepdims=True), jnp.max(s_own, axis=1, keepdims=True))
        p_past = jnp.exp(s_past - m)
        p_own = jnp.exp(s_own - m)
        l = jnp.sum(p_past, axis=1, keepdims=True) + jnp.sum(p_own, axis=1, keepdims=True)
        o = jnp.dot(p_own.astype(BF16), vnp_ref[...].astype(BF16), preferred_element_type=F32)
        for p in range(n_pages):
            o = o + lax.dot_general(p_past[:, p * page:(p + 1) * page].astype(BF16),
                                    vbuf_ref[slot, p].astype(BF16), nt, preferred_element_type=F32)
        o = o / l
        for t in range(t_new):
            o_ref[t, row, :] = jnp.sum(jnp.where(head_mask, o[t * N_HEADS:(t + 1) * N_HEADS, :], 0.0),
                                       axis=0, keepdims=True)
        return carry

    lax.fori_loop(0, seqs_per_step, one_sequence, 0)


def _sample_attention(page_table, q, k_new, v_new, bias, cache_kt, cache_vt):
    t_new, bs, _ = q.shape
    n_pages = page_table.shape[1]
    page = cache_kt.shape[2]
    own_width = bias.shape[1] - n_pages * page
    seqs_per_step = SUBLANES
    n_slots = 2
    assert bs % seqs_per_step == 0
    per_step = pl.BlockSpec((t_new, seqs_per_step, D_ATTN), lambda s, pt: (0, s, 0))
    in_hbm = pl.BlockSpec(memory_space=pl.ANY)
    grid_spec = pltpu.PrefetchScalarGridSpec(
        num_scalar_prefetch=1,
        grid=(bs // seqs_per_step,),
        in_specs=[per_step, per_step, per_step, pl.BlockSpec(bias.shape, lambda s, pt: (0, 0)), in_hbm, in_hbm],
        out_specs=per_step,
        scratch_shapes=[pltpu.VMEM((n_slots, n_pages, D_ATTN, page), F32),
                        pltpu.VMEM((n_slots, n_pages, D_ATTN, page), F32),
                        pltpu.VMEM((own_width, D_ATTN), F32), pltpu.VMEM((own_width, D_ATTN), F32),
                        pltpu.SemaphoreType.DMA((n_slots, 2))],
    )
    return pl.pallas_call(
        _sample_attn_kernel,
        grid_spec=grid_spec,
        out_shape=jax.ShapeDtypeStruct((t_new, bs, D_ATTN), F32),
        compiler_params=_params("arbitrary"),
        name="sample_attention",
    )(page_table, q, k_new, v_new, bias, cache_kt, cache_vt)


def _tail_kernel(chunk, x_ref, oa_ref, orn_ref, gpre_ref, wg_ref, wpa_ref, wpr_ref, wo_ref, gpost_ref,
                 fpre_ref, wgu_ref, wd_ref, fpost_ref, y_ref):
    d_model = x_ref.shape[-1]
    d_ff = wd_ref.shape[0]
    x = x_ref[...]
    hb = _rms(x, gpre_ref[...]).astype(BF16)
    pa = jnp.dot(oa_ref[...].astype(BF16), wpa_ref[...], preferred_element_type=F32)
    merged = jax.nn.sigmoid(jnp.dot(hb, wg_ref[:, :d_model], preferred_element_type=F32)) * pa
    pr = jnp.dot(orn_ref[...].astype(BF16), wpr_ref[...], preferred_element_type=F32)
    merged = merged + jax.nn.sigmoid(jnp.dot(hb, wg_ref[:, d_model:], preferred_element_type=F32)) * pr
    x1 = x + _rms(jnp.dot(merged.astype(BF16), wo_ref[...], preferred_element_type=F32), gpost_ref[...])

    fb = _rms(x1, fpre_ref[...]).astype(BF16)
    y = jnp.zeros(x.shape, F32)
    for c in range(0, d_ff, chunk):
        gate = jnp.dot(fb, wgu_ref[:, c:c + chunk], preferred_element_type=F32)
        up = jnp.dot(fb, wgu_ref[:, d_ff + c:d_ff + c + chunk], preferred_element_type=F32)
        act = (jax.nn.silu(gate) * up).astype(BF16)
        y = y + jnp.dot(act, wd_ref[c:c + chunk, :], preferred_element_type=F32)
    y_ref[...] = x1 + _rms(y, fpost_ref[...])


def _tail(x2d, oa, orn, weights, tm, chunk):
    n, d_model = x2d.shape
    assert n % tm == 0 and weights[8].shape[0] % chunk == 0
    row = lambda width: pl.BlockSpec((tm, width), lambda i: (i, 0))
    return pl.pallas_call(
        functools.partial(_tail_kernel, chunk),
        grid=(n // tm,),
        in_specs=[row(d_model), row(D_ATTN), row(D_RNN)] + [_resident(w.shape) for w in weights],
        out_specs=row(d_model),
        out_shape=jax.ShapeDtypeStruct((n, d_model), F32),
        compiler_params=_params("parallel"),
        name="tail",
    )(x2d, oa, orn, *weights)


def _block_diag(w):
    nblk, c, _ = w.shape
    eye = jnp.eye(nblk, dtype=w.dtype)
    return (w[:, :, None, :] * eye[:, None, :, None]).reshape(nblk * c, nblk * c)


def _rows_from_feature_major(xt):
    g, _, w = xt.shape
    return jnp.transpose(xt.reshape(g, N_HEADS, HEAD_DIM, w), (3, 0, 1, 2))[None]


def kernel(x_prompt, x_sample, cache_k, cache_v, page_table, state_conv, state_h, norm_mix_pre, norm_mix_post,
           w_in, rel_bias, conv_w, conv_b, rg_w_r, rg_b_r, rg_w_i, rg_b_i, rg_lambda, w_proj_attn, w_proj_rnn,
           w_out, norm_ffn_pre, norm_ffn_post, w_gate_up, w_down):
    depth = w_in.shape[0]
    assert depth == 1
    bp, tp, d_model = x_prompt.shape
    bs, ts, _ = x_sample.shape
    n_phys, page = cache_k.shape[1:3]
    n_pages = page_table.shape[1]
    past_len = n_pages * page
    assert tp % MOBA_BLOCK == 0 and past_len % MOBA_BLOCK == 0 and MOBA_BLOCK % page == 0
    assert past_len // MOBA_BLOCK > MOBA_TOP_K and CONV_W - 1 <= ts <= MOBA_BLOCK
    nbp = tp // MOBA_BLOCK
    l = 0
    row2 = lambda v: v.reshape(1, -1)

    n_stream_cols = 3 * D_ATTN + 2 * D_RNN
    w_in_b = w_in[l, :, :n_stream_cols].astype(BF16)
    w_gates = w_in[l, :, n_stream_cols:].astype(BF16)
    rnn_w = (conv_w[l], row2(conv_b[l]), _block_diag(rg_w_r[l]).astype(BF16), row2(rg_b_r[l]),
             _block_diag(rg_w_i[l]).astype(BF16), row2(rg_b_i[l]), row2(rg_lambda[l]))
    wpa, wpr, wo = w_proj_attn[l].astype(BF16), w_proj_rnn[l].astype(BF16), w_out[l].astype(BF16)
    wgu, wd = w_gate_up[l].astype(BF16), w_down[l].astype(BF16)
    g_pre, g_post = row2(norm_mix_pre[l]), row2(norm_mix_post[l])
    gf_pre, gf_post = row2(norm_ffn_pre[l]), row2(norm_ffn_post[l])

    tail_w = (g_pre, w_gates, wpa, wpr, wo, g_post, gf_pre, wgu, wd, gf_post)

    tm = 512
    xp2 = x_prompt.reshape(bp * tp, d_model)
    q, kb, vtb, means, kt, vt, orn, xtail, h_last = _prompt_in_proj(xp2, g_pre, w_in_b, rnn_w, tm, tp)
    tables = _prompt_bias_tables(rel_bias)
    oa = _prompt_attention(rel_bias, q.reshape(bp, tp, D_ATTN), kb.reshape(bp, nbp, MOBA_BLOCK, D_ATTN),
                           vtb.reshape(bp, nbp, N_HEADS, V_ROWS, MOBA_BLOCK), means.reshape(bp, nbp, D_ATTN), tables)
    y_prompt = _tail(xp2, oa.reshape(bp * tp, D_ATTN), orn, tail_w, tm, 256)
    y_prompt = y_prompt.reshape(bp, tp, d_model)
    new_k_prompt = jnp.swapaxes(_rows_from_feature_major(kt), 1, 2)
    new_v_prompt = jnp.swapaxes(_rows_from_feature_major(vt), 1, 2)
    new_conv_prompt = xtail[:, SUBLANES - (CONV_W - 1):, :][None]
    new_h_prompt = h_last.reshape(1, bp, D_RNN).astype(state_h.dtype)

    ns = bs * ts
    xs2 = jnp.swapaxes(x_sample, 0, 1).reshape(ns, d_model)
    qs, ks, vs, kts, vts, xrs, ggs = _sample_in_proj(xs2, g_pre, w_in_b, bs)
    own_width = 128
    bias_s = _sample_bias_table(rel_bias, past_len, ts, own_width)
    feature_major_pages = lambda c: jnp.transpose(c, (0, 2, 3, 1)).reshape(n_phys, D_ATTN, page)
    tm3 = lambda a: a.reshape(ts, bs, a.shape[-1])
    oas = _sample_attention(page_table, tm3(qs), tm3(ks), tm3(vs), bias_s,
                            feature_major_pages(cache_k[l]), feature_major_pages(cache_v[l]))
    xrs3 = tm3(xrs)
    orns, hs_last = _sample_rnn(xrs3, tm3(ggs), jnp.swapaxes(state_conv[l], 0, 1), state_h[l], rnn_w)
    y_sample = _tail(xs2, oas.reshape(ns, D_ATTN), orns.reshape(ns, D_RNN), tail_w, ns, 256)
    y_sample = jnp.swapaxes(y_sample.reshape(ts, bs, d_model), 0, 1)
    new_k_sample = _rows_from_feature_major(kts)
    new_v_sample = _rows_from_feature_major(vts)
    new_conv_sample = jnp.swapaxes(xrs3[ts - (CONV_W - 1):], 0, 1)[None]
    new_h_sample = hs_last[None].astype(state_h.dtype)

    return (y_prompt, y_sample, new_k_prompt, new_v_prompt, new_k_sample, new_v_sample,
            new_conv_prompt, new_h_prompt, new_conv_sample, new_h_sample)
```

```python
import functools
import math

import numpy as np
import jax
import jax.numpy as jnp
from jax import lax
from jax.experimental import pallas as pl
from jax.experimental.pallas import tpu as pltpu

N_HEADS = 8
HEAD_DIM = 64
D_ATTN = N_HEADS * HEAD_DIM
D_RNN = 512
N_RG_BLOCKS = 8
CONV_W = 4
RG_C = 8.0
MOBA_BLOCK = 256
MOBA_TOP_K = 3
NUM_BUCKETS = 32
MAX_DISTANCE = 128
RMS_EPS = 1e-6
NEG_INF = -1e30
BF16_SUBLANES = 16
V_ROWS = HEAD_DIM + BF16_SUBLANES
LOG2E = math.log2(math.e)

SUBLANES = 8
VMEM_LIMIT = 56 * 1024 * 1024
F32 = jnp.float32
BF16 = jnp.bfloat16
HIGHEST = lax.Precision.HIGHEST


def _params(*sem):
    return pltpu.CompilerParams(dimension_semantics=sem, vmem_limit_bytes=VMEM_LIMIT)


def _resident(shape):
    nd = len(shape)
    return pl.BlockSpec(shape, lambda *_: (0,) * nd, pipeline_mode=pl.Buffered(1))


def _rms(x, g):
    return x * lax.rsqrt(jnp.mean(x * x, axis=-1, keepdims=True) + RMS_EPS) * g


def _t5_bucket_np(n):
    n = np.maximum(n, 0)
    max_exact = NUM_BUCKETS // 2
    nf = np.maximum(n, 1).astype(np.float32)
    large = max_exact + (np.log(nf / np.float32(max_exact)) / np.float32(math.log(MAX_DISTANCE / max_exact))
                         * np.float32(NUM_BUCKETS - max_exact)).astype(np.int32)
    large = np.minimum(large, NUM_BUCKETS - 1)
    return np.where(n < max_exact, n, large).astype(np.int32)


def _rglru_coeffs(xc, wr_ref, br_ref, wi_ref, bi_ref, lam_ref):
    xb = xc.astype(BF16)
    r = jax.nn.sigmoid(jnp.dot(xb, wr_ref[...], preferred_element_type=F32) + br_ref[...])
    i = jax.nn.sigmoid(jnp.dot(xb, wi_ref[...], preferred_element_type=F32) + bi_ref[...])
    log_a = -RG_C * r * jax.nn.softplus(-lam_ref[...])
    a = jnp.exp(log_a)
    z = -jnp.tanh(log_a) * (a * a + 1.0)
    b = jnp.where(z > 0.0, z * lax.rsqrt(z), 0.0) * (i * xc)
    return a, b


def _rglru_tile(x, gg, first_tile, cw_ref, cb_ref, wr_ref, br_ref, wi_ref, bi_ref, lam_ref,
                o_ref, hl_ref, xbuf_ref, h_ref, obuf_ref):
    tt = x.shape[0]

    @pl.when(first_tile)
    def _():
        xbuf_ref[...] = jnp.zeros(xbuf_ref.shape, F32)
        h_ref[...] = jnp.zeros(h_ref.shape, F32)

    x_ext = jnp.concatenate([xbuf_ref[...], x], axis=0)
    row = lax.broadcasted_iota(jnp.int32, (tt, D_RNN), 0) % SUBLANES
    xc = cb_ref[...] + x * cw_ref[CONV_W - 1:CONV_W, :]
    for s in range(1, CONV_W):
        xc = xc + pltpu.roll(x_ext, s, axis=0)[SUBLANES:] * cw_ref[CONV_W - 1 - s:CONV_W - s, :]
    xbuf_ref[...] = x[tt - SUBLANES:]
    a, b = _rglru_coeffs(xc, wr_ref, br_ref, wi_ref, bi_ref, lam_ref)

    groups = (tt // SUBLANES, SUBLANES, D_RNN)
    a, b, row = a.reshape(groups), b.reshape(groups), row.reshape(groups)
    shift = 1
    while shift < SUBLANES:
        a_prev = pltpu.roll(a, shift, axis=1)
        b_prev = pltpu.roll(b, shift, axis=1)
        ok = row >= shift
        b = jnp.where(ok, a * b_prev + b, b)
        a = jnp.where(ok, a * a_prev, a)
        shift *= 2
    h = h_ref[...]
    for g in range(tt // SUBLANES):
        sl = slice(g * SUBLANES, (g + 1) * SUBLANES)
        hg = a[g] * h + b[g]
        obuf_ref[sl, :] = hg * gg[sl, :]
        h = hg[SUBLANES - 1:SUBLANES, :]
    o_ref[...] = obuf_ref[...].astype(o_ref.dtype)
    h_ref[...] = h
    hl_ref[...] = h


def _sample_rnn_kernel(xr_ref, gg_ref, conv_ref, h0_ref, cw_ref, cb_ref, wr_ref, br_ref, wi_ref, bi_ref, lam_ref,
                       o_ref, hl_ref):
    t_new = xr_ref.shape[0]
    past = [conv_ref[j] for j in range(CONV_W - 1)] + [xr_ref[t] for t in range(t_new)]
    h = h0_ref[...]
    for t in range(t_new):
        xc = cb_ref[...] + sum(past[t + j] * cw_ref[j:j + 1, :] for j in range(CONV_W))
        a, b = _rglru_coeffs(xc, wr_ref, br_ref, wi_ref, bi_ref, lam_ref)
        h = a * h + b
        o_ref[t] = h * gg_ref[t]
    hl_ref[...] = h


def _sample_rnn(xr_t, gg_t, conv_t, h0, weights):
    return pl.pallas_call(
        _sample_rnn_kernel,
        out_shape=[jax.ShapeDtypeStruct(xr_t.shape, F32), jax.ShapeDtypeStruct(h0.shape, F32)],
        compiler_params=pltpu.CompilerParams(vmem_limit_bytes=VMEM_LIMIT),
        name="sample_rnn",
    )(xr_t, gg_t, conv_t, h0, *weights)


def _projector(x_ref, g_ref, w_ref):
    hb = _rms(x_ref[...], g_ref[...]).astype(BF16)
    return lambda lo, width: jnp.dot(hb, w_ref[:, lo:lo + width], preferred_element_type=F32)


def _store_feature_major(kt_ref, vt_ref, k, v):
    groups, _, width = kt_ref.shape
    for r in range(groups):
        kt_ref[r] = k[r * width:(r + 1) * width, :].T
        vt_ref[r] = v[r * width:(r + 1) * width, :].T


def _sample_in_proj_kernel(x_ref, g_ref, w_ref, q_ref, k_ref, v_ref, kt_ref, vt_ref, xr_ref, gg_ref):
    proj = _projector(x_ref, g_ref, w_ref)
    q_ref[...] = proj(0, D_ATTN)
    k = proj(D_ATTN, D_ATTN)
    v = proj(2 * D_ATTN, D_ATTN)
    k_ref[...] = k
    v_ref[...] = v
    _store_feature_major(kt_ref, vt_ref, k, v)
    xr_ref[...] = proj(3 * D_ATTN, D_RNN)
    gg_ref[...] = jax.nn.gelu(proj(3 * D_ATTN + D_RNN, D_RNN))


def _prompt_in_proj_kernel(tiles_per_seq, x_ref, g_ref, w_ref, *refs):
    rnn_w = refs[:7]
    q_ref, kb_ref, vtb_ref, means_ref, kt_ref, vt_ref, orn_ref, xtail_ref, hl_ref = refs[7:16]
    xbuf_ref, h_ref, obuf_ref = refs[16:]
    proj = _projector(x_ref, g_ref, w_ref)
    xr = proj(3 * D_ATTN, D_RNN)
    gg = jax.nn.gelu(proj(3 * D_ATTN + D_RNN, D_RNN))
    xtail_ref[...] = xr[xr.shape[0] - SUBLANES:]
    _rglru_tile(xr, gg, pl.program_id(0) % tiles_per_seq == 0, *rnn_w, orn_ref, hl_ref, xbuf_ref, h_ref, obuf_ref)
    q_ref[...] = proj(0, D_ATTN)
    k = proj(D_ATTN, D_ATTN)
    v = proj(2 * D_ATTN, D_ATTN)
    _store_feature_major(kt_ref, vt_ref, k, v)
    kb_ref[...] = k.astype(BF16)
    pad_row = lax.broadcasted_iota(jnp.int32, (BF16_SUBLANES, MOBA_BLOCK), 0)
    ones_rows = jnp.where(pad_row == 0, 1.0, 0.0).astype(BF16)
    for r in range(vtb_ref.shape[0]):
        blk = slice(r * MOBA_BLOCK, (r + 1) * MOBA_BLOCK)
        v_t = v[blk, :].T.astype(BF16)
        for h in range(N_HEADS):
            vtb_ref[r, h] = jnp.concatenate([v_t[h * HEAD_DIM:(h + 1) * HEAD_DIM], ones_rows], axis=0)
        means_ref[r] = jnp.sum(k[blk, :], axis=0, keepdims=True) * (1.0 / MOBA_BLOCK)


def _feature_major_spec(n, tm, group_len):
    tiles_per_group = max(group_len // tm, 1)
    block = (max(tm // group_len, 1), D_ATTN, min(tm, group_len))
    spec = pl.BlockSpec(block, lambda i: (i // tiles_per_group, 0, i % tiles_per_group))
    return spec, jax.ShapeDtypeStruct((n // group_len, D_ATTN, group_len), F32)


def _sample_in_proj(x2d, g, w_in_b, group_len):
    n, d_model = x2d.shape
    assert n % group_len == 0
    t_spec, t_shape = _feature_major_spec(n, n, group_len)
    row = lambda width: pl.BlockSpec((n, width), lambda i: (i, 0))
    f32 = lambda width: jax.ShapeDtypeStruct((n, width), F32)
    return pl.pallas_call(
        _sample_in_proj_kernel,
        grid=(1,),
        in_specs=[row(d_model), _resident((1, d_model)), _resident(w_in_b.shape)],
        out_specs=[row(D_ATTN)] * 3 + [t_spec, t_spec, row(D_RNN), row(D_RNN)],
        out_shape=[f32(D_ATTN)] * 3 + [t_shape, t_shape, f32(D_RNN), f32(D_RNN)],
        compiler_params=_params("arbitrary"),
        name="sample_in_proj",
    )(x2d, g, w_in_b)


def _prompt_in_proj(x2d, g, w_in_b, rnn_w, tm, seq_len):
    n, d_model = x2d.shape
    assert n % seq_len == 0 and seq_len % tm == 0 and tm % MOBA_BLOCK == 0
    tiles_per_seq = seq_len // tm
    n_seq = n // seq_len
    blocks = tm // MOBA_BLOCK
    t_spec, t_shape = _feature_major_spec(n, tm, seq_len)
    row = lambda width: pl.BlockSpec((tm, width), lambda i: (i, 0))
    per_seq = lambda rows: pl.BlockSpec((None, rows, D_RNN), lambda i: (i // tiles_per_seq, 0, 0))
    return pl.pallas_call(
        functools.partial(_prompt_in_proj_kernel, tiles_per_seq),
        grid=(n // tm,),
        in_specs=[row(d_model), _resident((1, d_model)), _resident(w_in_b.shape)]
                 + [_resident(w.shape) for w in rnn_w],
        out_specs=[row(D_ATTN), row(D_ATTN),
                   pl.BlockSpec((blocks, N_HEADS, V_ROWS, MOBA_BLOCK), lambda i: (i, 0, 0, 0)),
                   pl.BlockSpec((blocks, 1, D_ATTN), lambda i: (i, 0, 0)),
                   t_spec, t_spec, row(D_RNN), per_seq(SUBLANES), per_seq(1)],
        out_shape=[jax.ShapeDtypeStruct((n, D_ATTN), F32), jax.ShapeDtypeStruct((n, D_ATTN), BF16),
                   jax.ShapeDtypeStruct((n // MOBA_BLOCK, N_HEADS, V_ROWS, MOBA_BLOCK), BF16),
                   jax.ShapeDtypeStruct((n // MOBA_BLOCK, 1, D_ATTN), F32),
                   t_shape, t_shape, jax.ShapeDtypeStruct((n, D_RNN), BF16),
                   jax.ShapeDtypeStruct((n_seq, SUBLANES, D_RNN), F32),
                   jax.ShapeDtypeStruct((n_seq, 1, D_RNN), F32)],
        scratch_shapes=[pltpu.VMEM((SUBLANES, D_RNN), F32), pltpu.VMEM((1, D_RNN), F32),
                        pltpu.VMEM((tm, D_RNN), F32)],
        compiler_params=_params("arbitrary"),
        name="prompt_in_proj",
    )(x2d, g, w_in_b, *rnn_w)


def _bias_table_kernel(rb_ref, bucket_ref, out_ref):
    bucket = bucket_ref[...]
    for h in range(N_HEADS):
        acc = jnp.full(bucket.shape, NEG_INF, F32)
        for b in range(NUM_BUCKETS):
            acc = jnp.where(bucket == b, rb_ref[b, h] * LOG2E, acc)
        out_ref[h] = acc


def _prompt_bias_tables(rel_bias):
    kr = np.arange(2 * MOBA_BLOCK)[:, None]
    qr = np.arange(MOBA_BLOCK)[None, :]
    dist = qr + MOBA_BLOCK - kr
    bucket = np.where(dist >= 0, _t5_bucket_np(dist), -1).astype(np.int32)
    return pl.pallas_call(
        _bias_table_kernel,
        in_specs=[pl.BlockSpec(memory_space=pltpu.SMEM), pl.BlockSpec(memory_space=pltpu.VMEM)],
        out_specs=pl.BlockSpec(memory_space=pltpu.VMEM),
        out_shape=jax.ShapeDtypeStruct((N_HEADS,) + bucket.shape, F32),
        name="prompt_bias_tables",
    )(rel_bias, jnp.asarray(bucket))


def _sample_bias_kernel(rbt_ref, bucket_ref, out_ref):
    bucket = bucket_ref[...]
    acc = jnp.full(bucket.shape, NEG_INF, F32)
    for b in range(NUM_BUCKETS):
        acc = jnp.where(bucket == b, rbt_ref[:, b:b + 1], acc)
    out_ref[...] = acc


def _sample_bias_table(rel_bias, past_len, t_new, own_width):
    t = np.repeat(np.arange(t_new), N_HEADS)[:, None]
    kpos = np.arange(past_len + own_width)[None, :]
    dist = past_len + t - kpos
    valid = (dist >= 0) & (kpos < past_len + t_new)
    bucket = np.where(valid, _t5_bucket_np(dist), -1).astype(np.int32)
    rbt = jnp.tile(rel_bias.T, (t_new, 1))
    return pl.pallas_call(
        _sample_bias_kernel,
        out_shape=jax.ShapeDtypeStruct(bucket.shape, F32),
        name="sample_bias_table",
    )(rbt, jnp.asarray(bucket))


def _select_rows(scores, n_valid, axis):
    nb = scores.shape[axis]
    idx = lax.broadcasted_iota(jnp.int32, scores.shape, axis)
    beaten = jnp.zeros(scores.shape, jnp.int32)
    for n in range(nb):
        other = lax.slice_in_dim(scores, n, n + 1, axis=axis)
        beats = (other > scores) | ((other == scores) & (n < idx))
        beaten = beaten + jnp.where(beats, jnp.where(n < n_valid, 1, 0), 0)
    return (beaten < MOBA_TOP_K) & (idx < n_valid)


def _prompt_attn_kernel(rb_ref, q_ref, kb_ref, vtb_ref, means_ref, tab_ref, o_ref,
                        sel_ref, qm_ref, m_ref, acc_ref, s0_ref, s1_ref):
    i = pl.program_id(1)
    nb = kb_ref.shape[0]
    prev = jnp.maximum(i - 1, 0)
    pair_rows = 2 * HEAD_DIM
    qt = q_ref[...].T

    means = means_ref[...]
    row_in_pair = lax.broadcasted_iota(jnp.int32, (pair_rows, MOBA_BLOCK), 0) // HEAD_DIM
    block_id = lax.broadcasted_iota(jnp.int32, (nb, MOBA_BLOCK), 0)
    for h in range(N_HEADS):
        hs = slice(h * HEAD_DIM, (h + 1) * HEAD_DIM)
        far_bias = rb_ref[NUM_BUCKETS - 1, h] * LOG2E
        block_scores = jnp.dot(means[:, hs], qt[hs, :], precision=HIGHEST, preferred_element_type=F32)
        selected = _select_rows(block_scores, i, axis=0)
        sel_ref[h * nb:(h + 1) * nb, :] = jnp.where(
            selected, jnp.where(block_id == i - 1, 0.0, far_bias), NEG_INF)
        q_pair = qt[(h // 2) * pair_rows:(h // 2 + 1) * pair_rows, :] * (HEAD_DIM ** -0.5 * LOG2E)
        qm_ref[h] = jnp.where(row_in_pair == h % 2, q_pair, 0.0).astype(BF16)

    def scores(j, table_rows, h, s_ref):
        pair = h // 2
        k_pair = kb_ref[j, :, pair * pair_rows:(pair + 1) * pair_rows]
        s = jnp.dot(k_pair, qm_ref[h], preferred_element_type=F32)
        if table_rows is not None:
            s = s + tab_ref[h, table_rows, :]
        s_ref[h] = s

    def consume(j, first, h, s_ref):
        s = s_ref[h]
        s_max = jnp.max(s, axis=0, keepdims=True)
        if first:
            m_new = s_max
            shift = m_new
        else:
            sel = sel_ref[pl.ds(h * nb + j, 1), :]
            m_old = m_ref[h:h + 1, :]
            m_new = jnp.maximum(m_old, s_max + sel)
            alpha = jnp.exp2(m_old - m_new)
            shift = m_new - sel
        m_ref[h:h + 1, :] = m_new
        p = jnp.exp2(s - shift).astype(BF16)
        pv = jnp.dot(vtb_ref[j, h], p, preferred_element_type=F32)
        acc_ref[h] = pv if first else alpha * acc_ref[h] + pv

    own_rows = slice(MOBA_BLOCK, 2 * MOBA_BLOCK)
    prev_rows = slice(0, MOBA_BLOCK)
    n_far = prev
    last_far = jnp.maximum(n_far - 1, 0)
    far = lambda j: jnp.minimum(j, last_far)
    heads = range(N_HEADS)

    def consume_then_refill(j, first, j_refill, s_ref):
        for h in heads:
            consume(j, first, h, s_ref)
            scores(far(j_refill), None, h, s_ref)

    for h in heads:
        scores(i, own_rows, h, s0_ref)
    for h in heads:
        scores(prev, prev_rows, h, s1_ref)
    consume_then_refill(i, True, 0, s0_ref)
    consume_then_refill(prev, False, 1, s1_ref)

    def far_pair(t, carry):
        consume_then_refill(2 * t, False, 2 * t + 2, s0_ref)
        consume_then_refill(2 * t + 1, False, 2 * t + 3, s1_ref)
        return carry

    lax.fori_loop(0, n_far // 2, far_pair, 0)

    @pl.when(n_far % 2 == 1)
    def _():
        for h in heads:
            consume(n_far - 1, False, h, s0_ref)

    out_t = jnp.concatenate(
        [acc_ref[h, :HEAD_DIM, :] / acc_ref[h, HEAD_DIM:HEAD_DIM + 1, :] for h in heads], axis=0)
    o_ref[...] = out_t.T.astype(o_ref.dtype)


def _prompt_attention(rel_bias, q, kb4, vtb4, means, tables):
    b, nb = kb4.shape[:2]
    t = nb * MOBA_BLOCK
    return pl.pallas_call(
        _prompt_attn_kernel,
        grid=(b, nb),
        in_specs=[pl.BlockSpec(memory_space=pltpu.SMEM),
                  pl.BlockSpec((None, MOBA_BLOCK, D_ATTN), lambda bi, i: (bi, i, 0)),
                  pl.BlockSpec((None, nb, MOBA_BLOCK, D_ATTN), lambda bi, i: (bi, 0, 0, 0)),
                  pl.BlockSpec((None, nb, N_HEADS, V_ROWS, MOBA_BLOCK), lambda bi, i: (bi, 0, 0, 0, 0)),
                  pl.BlockSpec((None, nb, D_ATTN), lambda bi, i: (bi, 0, 0)),
                  _resident(tables.shape)],
        out_specs=pl.BlockSpec((None, MOBA_BLOCK, D_ATTN), lambda bi, i: (bi, i, 0)),
        out_shape=jax.ShapeDtypeStruct((b, t, D_ATTN), BF16),
        scratch_shapes=[pltpu.VMEM((N_HEADS * nb, MOBA_BLOCK), F32),
                        pltpu.VMEM((N_HEADS, 2 * HEAD_DIM, MOBA_BLOCK), BF16),
                        pltpu.VMEM((N_HEADS, MOBA_BLOCK), F32),
                        pltpu.VMEM((N_HEADS, V_ROWS, MOBA_BLOCK), F32),
                        pltpu.VMEM((N_HEADS, MOBA_BLOCK, MOBA_BLOCK), F32),
                        pltpu.VMEM((N_HEADS, MOBA_BLOCK, MOBA_BLOCK), F32)],
        compiler_params=_params("parallel", "arbitrary"),
        name="prompt_attention",
    )(rel_bias, q, kb4, vtb4, means, tables)


def _sample_attn_kernel(pt_ref, q_ref, kn_ref, vn_ref, bias_ref, kt_hbm, vt_hbm, o_ref,
                        kbuf_ref, vbuf_ref, knp_ref, vnp_ref, sem):
    step = pl.program_id(0)
    t_new, seqs_per_step, _ = q_ref.shape
    n_slots, n_pages, _, page = kbuf_ref.shape
    n_seq = pl.num_programs(0) * seqs_per_step
    pages_per_block = MOBA_BLOCK // page
    nb = n_pages // pages_per_block
    past_len = n_pages * page

    def page_copies(seq):
        slot = seq % n_slots
        copies = []
        for p in range(n_pages):
            phys = pt_ref[seq, p]
            copies.append(pltpu.make_async_copy(kt_hbm.at[phys], kbuf_ref.at[slot, p], sem.at[slot, 0]))
            copies.append(pltpu.make_async_copy(vt_hbm.at[phys], vbuf_ref.at[slot, p], sem.at[slot, 1]))
        return copies

    lookahead = n_slots - 1

    @pl.when(step == 0)
    def _():
        for first in range(lookahead):
            for copy in page_copies(first):
                copy.start()

    head_of_row = lax.broadcasted_iota(jnp.int32, (N_HEADS, D_ATTN), 0)
    head_of_lane = lax.broadcasted_iota(jnp.int32, (N_HEADS, D_ATTN), 1) // HEAD_DIM
    head_mask = head_of_row == head_of_lane
    lane = lax.broadcasted_iota(jnp.int32, (D_ATTN, page), 1)
    nt = (((1,), (1,)), ((), ()))

    def one_sequence(r, carry):
        seq = step * seqs_per_step + r
        slot = seq % n_slots
        row = pl.ds(r, 1)

        @pl.when(seq + lookahead < n_seq)
        def _():
            for copy in page_copies(seq + lookahead):
                copy.start()

        for copy in page_copies(seq):
            copy.wait()

        qrows = jnp.concatenate(
            [jnp.where(head_mask, jnp.broadcast_to(q_ref[t, row, :], (N_HEADS, D_ATTN)), 0.0)
             for t in range(t_new)], axis=0)

        means_t = jnp.zeros((D_ATTN, page), F32)
        for n in range(nb):
            total = sum(kbuf_ref[slot, n * pages_per_block + i] for i in range(pages_per_block))
            means_t = jnp.where(lane == n, jnp.sum(total, axis=1, keepdims=True) * (1.0 / MOBA_BLOCK), means_t)
        scores = jnp.dot(qrows, means_t, precision=HIGHEST, preferred_element_type=F32)[:, :nb]
        selected = jnp.where(_select_rows(scores, nb, axis=1), 1.0, 0.0)

        qb = (qrows * (HEAD_DIM ** -0.5)).astype(BF16)
        s_pages = []
        for p in range(n_pages):
            s = jnp.dot(qb, kbuf_ref[slot, p].astype(BF16), preferred_element_type=F32)
            keep = selected[:, p // pages_per_block:p // pages_per_block + 1] > 0.5
            s_pages.append(jnp.where(keep, s + bias_ref[:, p * page:(p + 1) * page], NEG_INF))
        s_past = jnp.concatenate(s_pages, axis=1)

        knp_ref[...] = jnp.zeros(knp_ref.shape, F32)
        vnp_ref[...] = jnp.zeros(vnp_ref.shape, F32)
        for t in range(t_new):
            knp_ref[t:t + 1, :] = kn_ref[t, row, :]
            vnp_ref[t:t + 1, :] = vn_ref[t, row, :]
        s_own = (lax.dot_general(qb, knp_ref[...].astype(BF16), nt, preferred_element_type=F32)
                 + bias_ref[:, past_len:])

        m = jnp.maximum(jnp.max(s_past, axis=1, keepdims=True), jnp.max(s_own, axis=1, keepdims=True))
        p_past = jnp.exp(s_past - m)
        p_own = jnp.exp(s_own - m)
        l = jnp.sum(p_past, axis=1, keepdims=True) + jnp.sum(p_own, axis=1, keepdims=True)
        o = jnp.dot(p_own.astype(BF16), vnp_ref[...].astype(BF16), preferred_element_type=F32)
        for p in range(n_pages):
            o = o + lax.dot_general(p_past[:, p * page:(p + 1) * page].astype(BF16),
                                    vbuf_ref[slot, p].astype(BF16), nt, preferred_element_type=F32)
        o = o / l
        for t in range(t_new):
            o_ref[t, row, :] = jnp.sum(jnp.where(head_mask, o[t * N_HEADS:(t + 1) * N_HEADS, :], 0.0),
                                       axis=0, keepdims=True)
        return carry

    lax.fori_loop(0, seqs_per_step, one_sequence, 0)


def _sample_attention(page_table, q, k_new, v_new, bias, cache_kt, cache_vt):
    t_new, bs, _ = q.shape
    n_pages = page_table.shape[1]
    page = cache_kt.shape[2]
    own_width = bias.shape[1] - n_pages * page
    seqs_per_step = SUBLANES
    n_slots = 3
    assert bs % seqs_per_step == 0 and bs >= n_slots
    per_step = pl.BlockSpec((t_new, seqs_per_step, D_ATTN), lambda s, pt: (0, s, 0))
    in_hbm = pl.BlockSpec(memory_space=pl.ANY)
    grid_spec = pltpu.PrefetchScalarGridSpec(
        num_scalar_prefetch=1,
        grid=(bs // seqs_per_step,),
        in_specs=[per_step, per_step, per_step, pl.BlockSpec(bias.shape, lambda s, pt: (0, 0)), in_hbm, in_hbm],
        out_specs=per_step,
        scratch_shapes=[pltpu.VMEM((n_slots, n_pages, D_ATTN, page), F32),
                        pltpu.VMEM((n_slots, n_pages, D_ATTN, page), F32),
                        pltpu.VMEM((own_width, D_ATTN), F32), pltpu.VMEM((own_width, D_ATTN), F32),
                        pltpu.SemaphoreType.DMA((n_slots, 2))],
    )
    return pl.pallas_call(
        _sample_attn_kernel,
        grid_spec=grid_spec,
        out_shape=jax.ShapeDtypeStruct((t_new, bs, D_ATTN), F32),
        compiler_params=_params("arbitrary"),
        name="sample_attention",
    )(page_table, q, k_new, v_new, bias, cache_kt, cache_vt)


def _tail_kernel(chunk, x_ref, oa_ref, orn_ref, gpre_ref, wg_ref, wpa_ref, wpr_ref, wo_ref, gpost_ref,
                 fpre_ref, wgu_ref, wd_ref, fpost_ref, y_ref):
    d_model = x_ref.shape[-1]
    d_ff = wd_ref.shape[0]
    x = x_ref[...]
    hb = _rms(x, gpre_ref[...]).astype(BF16)
    pa = jnp.dot(oa_ref[...].astype(BF16), wpa_ref[...], preferred_element_type=F32)
    merged = jax.nn.sigmoid(jnp.dot(hb, wg_ref[:, :d_model], preferred_element_type=F32)) * pa
    pr = jnp.dot(orn_ref[...].astype(BF16), wpr_ref[...], preferred_element_type=F32)
    merged = merged + jax.nn.sigmoid(jnp.dot(hb, wg_ref[:, d_model:], preferred_element_type=F32)) * pr
    x1 = x + _rms(jnp.dot(merged.astype(BF16), wo_ref[...], preferred_element_type=F32), gpost_ref[...])

    fb = _rms(x1, fpre_ref[...]).astype(BF16)
    y = jnp.zeros(x.shape, F32)
    for c in range(0, d_ff, chunk):
        gate = jnp.dot(fb, wgu_ref[:, c:c + chunk], preferred_element_type=F32)
        up = jnp.dot(fb, wgu_ref[:, d_ff + c:d_ff + c + chunk], preferred_element_type=F32)
        act = (jax.nn.silu(gate) * up).astype(BF16)
        y = y + jnp.dot(act, wd_ref[c:c + chunk, :], preferred_element_type=F32)
    y_ref[...] = x1 + _rms(y, fpost_ref[...])


def _tail(x2d, oa, orn, weights, tm, chunk):
    n, d_model = x2d.shape
    assert n % tm == 0 and weights[8].shape[0] % chunk == 0
    row = lambda width: pl.BlockSpec((tm, width), lambda i: (i, 0))
    return pl.pallas_call(
        functools.partial(_tail_kernel, chunk),
        grid=(n // tm,),
        in_specs=[row(d_model), row(D_ATTN), row(D_RNN)] + [_resident(w.shape) for w in weights],
        out_specs=row(d_model),
        out_shape=jax.ShapeDtypeStruct((n, d_model), F32),
        compiler_params=_params("parallel"),
        name="tail",
    )(x2d, oa, orn, *weights)


def _block_diag(w):
    nblk, c, _ = w.shape
    eye = jnp.eye(nblk, dtype=w.dtype)
    return (w[:, :, None, :] * eye[:, None, :, None]).reshape(nblk * c, nblk * c)


def _rows_from_feature_major(xt):
    g, _, w = xt.shape
    return jnp.transpose(xt.reshape(g, N_HEADS, HEAD_DIM, w), (3, 0, 1, 2))[None]


def kernel(x_prompt, x_sample, cache_k, cache_v, page_table, state_conv, state_h, norm_mix_pre, norm_mix_post,
           w_in, rel_bias, conv_w, conv_b, rg_w_r, rg_b_r, rg_w_i, rg_b_i, rg_lambda, w_proj_attn, w_proj_rnn,
           w_out, norm_ffn_pre, norm_ffn_post, w_gate_up, w_down):
    depth = w_in.shape[0]
    assert depth == 1
    bp, tp, d_model = x_prompt.shape
    bs, ts, _ = x_sample.shape
    n_phys, page = cache_k.shape[1:3]
    n_pages = page_table.shape[1]
    past_len = n_pages * page
    assert tp % MOBA_BLOCK == 0 and past_len % MOBA_BLOCK == 0 and MOBA_BLOCK % page == 0
    assert past_len // MOBA_BLOCK > MOBA_TOP_K and CONV_W - 1 <= ts <= MOBA_BLOCK
    nbp = tp // MOBA_BLOCK
    l = 0
    row2 = lambda v: v.reshape(1, -1)

    n_stream_cols = 3 * D_ATTN + 2 * D_RNN
    w_in_b = w_in[l, :, :n_stream_cols].astype(BF16)
    w_gates = w_in[l, :, n_stream_cols:].astype(BF16)
    rnn_w = (conv_w[l], row2(conv_b[l]), _block_diag(rg_w_r[l]).astype(BF16), row2(rg_b_r[l]),
             _block_diag(rg_w_i[l]).astype(BF16), row2(rg_b_i[l]), row2(rg_lambda[l]))
    wpa, wpr, wo = w_proj_attn[l].astype(BF16), w_proj_rnn[l].astype(BF16), w_out[l].astype(BF16)
    wgu, wd = w_gate_up[l].astype(BF16), w_down[l].astype(BF16)
    g_pre, g_post = row2(norm_mix_pre[l]), row2(norm_mix_post[l])
    gf_pre, gf_post = row2(norm_ffn_pre[l]), row2(norm_ffn_post[l])

    tail_w = (g_pre, w_gates, wpa, wpr, wo, g_post, gf_pre, wgu, wd, gf_post)

    tm = 512
    xp2 = x_prompt.reshape(bp * tp, d_model)
    q, kb, vtb, means, kt, vt, orn, xtail, h_last = _prompt_in_proj(xp2, g_pre, w_in_b, rnn_w, tm, tp)
    tables = _prompt_bias_tables(rel_bias)
    oa = _prompt_attention(rel_bias, q.reshape(bp, tp, D_ATTN), kb.reshape(bp, nbp, MOBA_BLOCK, D_ATTN),
                           vtb.reshape(bp, nbp, N_HEADS, V_ROWS, MOBA_BLOCK), means.reshape(bp, nbp, D_ATTN), tables)
    y_prompt = _tail(xp2, oa.reshape(bp * tp, D_ATTN), orn, tail_w, tm, 256)
    y_prompt = y_prompt.reshape(bp, tp, d_model)
    new_k_prompt = jnp.swapaxes(_rows_from_feature_major(kt), 1, 2)
    new_v_prompt = jnp.swapaxes(_rows_from_feature_major(vt), 1, 2)
    new_conv_prompt = xtail[:, SUBLANES - (CONV_W - 1):, :][None]
    new_h_prompt = h_last.reshape(1, bp, D_RNN).astype(state_h.dtype)

    ns = bs * ts
    xs2 = jnp.swapaxes(x_sample, 0, 1).reshape(ns, d_model)
    qs, ks, vs, kts, vts, xrs, ggs = _sample_in_proj(xs2, g_pre, w_in_b, bs)
    own_width = 128
    bias_s = _sample_bias_table(rel_bias, past_len, ts, own_width)
    feature_major_pages = lambda c: jnp.transpose(c, (0, 2, 3, 1)).reshape(n_phys, D_ATTN, page)
    tm3 = lambda a: a.reshape(ts, bs, a.shape[-1])
    oas = _sample_attention(page_table, tm3(qs), tm3(ks), tm3(vs), bias_s,
                            feature_major_pages(cache_k[l]), feature_major_pages(cache_v[l]))
    xrs3 = tm3(xrs)
    orns, hs_last = _sample_rnn(xrs3, tm3(ggs), jnp.swapaxes(state_conv[l], 0, 1), state_h[l], rnn_w)
    y_sample = _tail(xs2, oas.reshape(ns, D_ATTN), orns.reshape(ns, D_RNN), tail_w, ns, 256)
    y_sample = jnp.swapaxes(y_sample.reshape(ts, bs, d_model), 0, 1)
    new_k_sample = _rows_from_feature_major(kts)
    new_v_sample = _rows_from_feature_major(vts)
    new_conv_sample = jnp.swapaxes(xrs3[ts - (CONV_W - 1):], 0, 1)[None]
    new_h_sample = hs_last[None].astype(state_h.dtype)

    return (y_prompt, y_sample, new_k_prompt, new_v_prompt, new_k_sample, new_v_sample,
            new_conv_prompt, new_h_prompt, new_conv_sample, new_h_sample)
```

```python
import functools
import math

import numpy as np
import jax
import jax.numpy as jnp
from jax import lax
from jax.experimental import pallas as pl
from jax.experimental.pallas import tpu as pltpu

N_HEADS = 8
HEAD_DIM = 64
D_ATTN = N_HEADS * HEAD_DIM
D_RNN = 512
N_RG_BLOCKS = 8
CONV_W = 4
RG_C = 8.0
MOBA_BLOCK = 256
MOBA_TOP_K = 3
NUM_BUCKETS = 32
MAX_DISTANCE = 128
RMS_EPS = 1e-6
NEG_INF = -1e30
BF16_SUBLANES = 16
V_ROWS = HEAD_DIM + BF16_SUBLANES
LOG2E = math.log2(math.e)

SUBLANES = 8
VMEM_LIMIT = 56 * 1024 * 1024
F32 = jnp.float32
BF16 = jnp.bfloat16
HIGHEST = lax.Precision.HIGHEST


def _params(*sem):
    return pltpu.CompilerParams(dimension_semantics=sem, vmem_limit_bytes=VMEM_LIMIT)


def _resident(shape):
    nd = len(shape)
    return pl.BlockSpec(shape, lambda *_: (0,) * nd, pipeline_mode=pl.Buffered(1))


def _rms(x, g):
    return x * lax.rsqrt(jnp.mean(x * x, axis=-1, keepdims=True) + RMS_EPS) * g


def _t5_bucket_np(n):
    n = np.maximum(n, 0)
    max_exact = NUM_BUCKETS // 2
    nf = np.maximum(n, 1).astype(np.float32)
    large = max_exact + (np.log(nf / np.float32(max_exact)) / np.float32(math.log(MAX_DISTANCE / max_exact))
                         * np.float32(NUM_BUCKETS - max_exact)).astype(np.int32)
    large = np.minimum(large, NUM_BUCKETS - 1)
    return np.where(n < max_exact, n, large).astype(np.int32)


def _rglru_coeffs(xc, wr_ref, br_ref, wi_ref, bi_ref, lam_ref):
    xb = xc.astype(BF16)
    r_plus = jnp.tanh(jnp.dot(xb, wr_ref[...], preferred_element_type=F32) + br_ref[...]) + 1.0
    i_plus = jnp.tanh(jnp.dot(xb, wi_ref[...], preferred_element_type=F32) + bi_ref[...]) + 1.0
    log_a = (-0.5 * RG_C * jax.nn.softplus(-lam_ref[...])) * r_plus
    a = jnp.exp(log_a)
    z = -jnp.tanh(log_a) * (a * a + 1.0)
    b = jnp.where(z > 0.0, z * lax.rsqrt(z), 0.0) * (i_plus * (0.5 * xc))
    return a, b


def _rglru_tile(x, gg, first_tile, cw_ref, cb_ref, wr_ref, br_ref, wi_ref, bi_ref, lam_ref,
                o_ref, hl_ref, xbuf_ref, h_ref, obuf_ref):
    tt = x.shape[0]

    @pl.when(first_tile)
    def _():
        xbuf_ref[...] = jnp.zeros(xbuf_ref.shape, F32)
        h_ref[...] = jnp.zeros(h_ref.shape, F32)

    x_ext = jnp.concatenate([xbuf_ref[...], x], axis=0)
    row = lax.broadcasted_iota(jnp.int32, (tt, D_RNN), 0) % SUBLANES
    xc = cb_ref[...] + x * cw_ref[CONV_W - 1:CONV_W, :]
    for s in range(1, CONV_W):
        xc = xc + pltpu.roll(x_ext, s, axis=0)[SUBLANES:] * cw_ref[CONV_W - 1 - s:CONV_W - s, :]
    xbuf_ref[...] = x[tt - SUBLANES:]
    a, b = _rglru_coeffs(xc, wr_ref, br_ref, wi_ref, bi_ref, lam_ref)

    groups = (tt // SUBLANES, SUBLANES, D_RNN)
    a, b, row = a.reshape(groups), b.reshape(groups), row.reshape(groups)
    shift = 1
    while shift < SUBLANES:
        a_prev = pltpu.roll(a, shift, axis=1)
        b_prev = pltpu.roll(b, shift, axis=1)
        ok = row >= shift
        b = jnp.where(ok, a * b_prev + b, b)
        a = jnp.where(ok, a * a_prev, a)
        shift *= 2
    h = h_ref[...]
    for g in range(tt // SUBLANES):
        sl = slice(g * SUBLANES, (g + 1) * SUBLANES)
        hg = a[g] * h + b[g]
        obuf_ref[sl, :] = hg * gg[sl, :]
        h = hg[SUBLANES - 1:SUBLANES, :]
    o_ref[...] = obuf_ref[...].astype(o_ref.dtype)
    h_ref[...] = h
    hl_ref[...] = h


def _sample_rnn_kernel(xr_ref, gg_ref, conv_ref, h0_ref, cw_ref, cb_ref, wr_ref, br_ref, wi_ref, bi_ref, lam_ref,
                       o_ref, hl_ref):
    t_new = xr_ref.shape[0]
    past = [conv_ref[j] for j in range(CONV_W - 1)] + [xr_ref[t] for t in range(t_new)]
    h = h0_ref[...]
    for t in range(t_new):
        xc = cb_ref[...] + sum(past[t + j] * cw_ref[j:j + 1, :] for j in range(CONV_W))
        a, b = _rglru_coeffs(xc, wr_ref, br_ref, wi_ref, bi_ref, lam_ref)
        h = a * h + b
        o_ref[t] = h * gg_ref[t]
    hl_ref[...] = h


def _sample_rnn(xr_t, gg_t, conv_t, h0, weights):
    return pl.pallas_call(
        _sample_rnn_kernel,
        out_shape=[jax.ShapeDtypeStruct(xr_t.shape, F32), jax.ShapeDtypeStruct(h0.shape, F32)],
        compiler_params=pltpu.CompilerParams(vmem_limit_bytes=VMEM_LIMIT),
        name="sample_rnn",
    )(xr_t, gg_t, conv_t, h0, *weights)


def _projector(x_ref, g_ref, w_ref):
    hb = _rms(x_ref[...], g_ref[...]).astype(BF16)
    return lambda lo, width: jnp.dot(hb, w_ref[:, lo:lo + width], preferred_element_type=F32)


def _store_feature_major(kt_ref, vt_ref, k, v):
    groups, _, width = kt_ref.shape
    for r in range(groups):
        kt_ref[r] = k[r * width:(r + 1) * width, :].T
        vt_ref[r] = v[r * width:(r + 1) * width, :].T


def _sample_in_proj_kernel(x_ref, g_ref, w_ref, q_ref, k_ref, v_ref, kt_ref, vt_ref, xr_ref, gg_ref):
    proj = _projector(x_ref, g_ref, w_ref)
    q_ref[...] = proj(0, D_ATTN)
    k = proj(D_ATTN, D_ATTN)
    v = proj(2 * D_ATTN, D_ATTN)
    k_ref[...] = k
    v_ref[...] = v
    _store_feature_major(kt_ref, vt_ref, k, v)
    xr_ref[...] = proj(3 * D_ATTN, D_RNN)
    gg_ref[...] = jax.nn.gelu(proj(3 * D_ATTN + D_RNN, D_RNN))


def _prompt_in_proj_kernel(tiles_per_seq, x_ref, g_ref, w_ref, *refs):
    rnn_w = refs[:7]
    q_ref, kb_ref, vtb_ref, means_ref, kt_ref, vt_ref, orn_ref, xtail_ref, hl_ref = refs[7:16]
    xbuf_ref, h_ref, obuf_ref = refs[16:]
    proj = _projector(x_ref, g_ref, w_ref)
    xr = proj(3 * D_ATTN, D_RNN)
    gg = jax.nn.gelu(proj(3 * D_ATTN + D_RNN, D_RNN))
    xtail_ref[...] = xr[xr.shape[0] - SUBLANES:]
    _rglru_tile(xr, gg, pl.program_id(0) % tiles_per_seq == 0, *rnn_w, orn_ref, hl_ref, xbuf_ref, h_ref, obuf_ref)
    q_ref[...] = proj(0, D_ATTN)
    k = proj(D_ATTN, D_ATTN)
    v = proj(2 * D_ATTN, D_ATTN)
    _store_feature_major(kt_ref, vt_ref, k, v)
    kb_ref[...] = k.astype(BF16)
    pad_row = lax.broadcasted_iota(jnp.int32, (BF16_SUBLANES, MOBA_BLOCK), 0)
    ones_rows = jnp.where(pad_row == 0, 1.0, 0.0).astype(BF16)
    for r in range(vtb_ref.shape[0]):
        blk = slice(r * MOBA_BLOCK, (r + 1) * MOBA_BLOCK)
        v_t = v[blk, :].T.astype(BF16)
        for h in range(N_HEADS):
            vtb_ref[r, h] = jnp.concatenate([v_t[h * HEAD_DIM:(h + 1) * HEAD_DIM], ones_rows], axis=0)
        means_ref[r] = jnp.sum(k[blk, :], axis=0, keepdims=True) * (1.0 / MOBA_BLOCK)


def _feature_major_spec(n, tm, group_len):
    tiles_per_group = max(group_len // tm, 1)
    block = (max(tm // group_len, 1), D_ATTN, min(tm, group_len))
    spec = pl.BlockSpec(block, lambda i: (i // tiles_per_group, 0, i % tiles_per_group))
    return spec, jax.ShapeDtypeStruct((n // group_len, D_ATTN, group_len), F32)


def _sample_in_proj(x2d, g, w_in_b, group_len):
    n, d_model = x2d.shape
    assert n % group_len == 0
    t_spec, t_shape = _feature_major_spec(n, n, group_len)
    row = lambda width: pl.BlockSpec((n, width), lambda i: (i, 0))
    f32 = lambda width: jax.ShapeDtypeStruct((n, width), F32)
    return pl.pallas_call(
        _sample_in_proj_kernel,
        grid=(1,),
        in_specs=[row(d_model), _resident((1, d_model)), _resident(w_in_b.shape)],
        out_specs=[row(D_ATTN)] * 3 + [t_spec, t_spec, row(D_RNN), row(D_RNN)],
        out_shape=[f32(D_ATTN)] * 3 + [t_shape, t_shape, f32(D_RNN), f32(D_RNN)],
        compiler_params=_params("arbitrary"),
        name="sample_in_proj",
    )(x2d, g, w_in_b)


def _prompt_in_proj(x2d, g, w_in_b, rnn_w, tm, seq_len):
    n, d_model = x2d.shape
    assert n % seq_len == 0 and seq_len % tm == 0 and tm % MOBA_BLOCK == 0
    tiles_per_seq = seq_len // tm
    n_seq = n // seq_len
    blocks = tm // MOBA_BLOCK
    t_spec, t_shape = _feature_major_spec(n, tm, seq_len)
    row = lambda width: pl.BlockSpec((tm, width), lambda i: (i, 0))
    per_seq = lambda rows: pl.BlockSpec((None, rows, D_RNN), lambda i: (i // tiles_per_seq, 0, 0))
    return pl.pallas_call(
        functools.partial(_prompt_in_proj_kernel, tiles_per_seq),
        grid=(n // tm,),
        in_specs=[row(d_model), _resident((1, d_model)), _resident(w_in_b.shape)]
                 + [_resident(w.shape) for w in rnn_w],
        out_specs=[row(D_ATTN), row(D_ATTN),
                   pl.BlockSpec((blocks, N_HEADS, V_ROWS, MOBA_BLOCK), lambda i: (i, 0, 0, 0)),
                   pl.BlockSpec((blocks, 1, D_ATTN), lambda i: (i, 0, 0)),
                   t_spec, t_spec, row(D_RNN), per_seq(SUBLANES), per_seq(1)],
        out_shape=[jax.ShapeDtypeStruct((n, D_ATTN), F32), jax.ShapeDtypeStruct((n, D_ATTN), BF16),
                   jax.ShapeDtypeStruct((n // MOBA_BLOCK, N_HEADS, V_ROWS, MOBA_BLOCK), BF16),
                   jax.ShapeDtypeStruct((n // MOBA_BLOCK, 1, D_ATTN), F32),
                   t_shape, t_shape, jax.ShapeDtypeStruct((n, D_RNN), BF16),
                   jax.ShapeDtypeStruct((n_seq, SUBLANES, D_RNN), F32),
                   jax.ShapeDtypeStruct((n_seq, 1, D_RNN), F32)],
        scratch_shapes=[pltpu.VMEM((SUBLANES, D_RNN), F32), pltpu.VMEM((1, D_RNN), F32),
                        pltpu.VMEM((tm, D_RNN), F32)],
        compiler_params=_params("arbitrary"),
        name="prompt_in_proj",
    )(x2d, g, w_in_b, *rnn_w)


def _bias_table_kernel(rb_ref, bucket_ref, out_ref):
    bucket = bucket_ref[...]
    for h in range(N_HEADS):
        acc = jnp.full(bucket.shape, NEG_INF, F32)
        for b in range(NUM_BUCKETS):
            acc = jnp.where(bucket == b, rb_ref[b, h] * LOG2E, acc)
        out_ref[h] = acc


def _prompt_bias_tables(rel_bias):
    kr = np.arange(2 * MOBA_BLOCK)[:, None]
    qr = np.arange(MOBA_BLOCK)[None, :]
    dist = qr + MOBA_BLOCK - kr
    bucket = np.where(dist >= 0, _t5_bucket_np(dist), -1).astype(np.int32)
    return pl.pallas_call(
        _bias_table_kernel,
        in_specs=[pl.BlockSpec(memory_space=pltpu.SMEM), pl.BlockSpec(memory_space=pltpu.VMEM)],
        out_specs=pl.BlockSpec(memory_space=pltpu.VMEM),
        out_shape=jax.ShapeDtypeStruct((N_HEADS,) + bucket.shape, F32),
        name="prompt_bias_tables",
    )(rel_bias, jnp.asarray(bucket))


def _sample_bias_kernel(rbt_ref, bucket_ref, out_ref):
    bucket = bucket_ref[...]
    acc = jnp.full(bucket.shape, NEG_INF, F32)
    for b in range(NUM_BUCKETS):
        acc = jnp.where(bucket == b, rbt_ref[:, b:b + 1], acc)
    out_ref[...] = acc


def _sample_bias_table(rel_bias, past_len, t_new, own_width):
    t = np.repeat(np.arange(t_new), N_HEADS)[:, None]
    kpos = np.arange(past_len + own_width)[None, :]
    dist = past_len + t - kpos
    valid = (dist >= 0) & (kpos < past_len + t_new)
    bucket = np.where(valid, _t5_bucket_np(dist), -1).astype(np.int32)
    rbt = jnp.tile(rel_bias.T, (t_new, 1))
    return pl.pallas_call(
        _sample_bias_kernel,
        out_shape=jax.ShapeDtypeStruct(bucket.shape, F32),
        name="sample_bias_table",
    )(rbt, jnp.asarray(bucket))


def _select_rows(scores, n_valid, axis):
    nb = scores.shape[axis]
    idx = lax.broadcasted_iota(jnp.int32, scores.shape, axis)
    valid = idx < n_valid
    left = jnp.where(valid, scores, -jnp.inf)
    selected = jnp.zeros(scores.shape, jnp.bool_)
    for _ in range(MOBA_TOP_K):
        best = jnp.max(left, axis=axis, keepdims=True)
        first = jnp.min(jnp.where(left == best, idx, nb), axis=axis, keepdims=True)
        pick = idx == first
        selected = selected | pick
        left = jnp.where(pick, -jnp.inf, left)
    return selected & valid


def _prompt_attn_kernel(rb_ref, q_ref, kb_ref, vtb_ref, means_ref, tab_ref, o_ref,
                        sel_ref, qm_ref, m_ref, acc_ref, s0_ref, s1_ref):
    i = pl.program_id(1)
    nb = kb_ref.shape[0]
    prev = jnp.maximum(i - 1, 0)
    pair_rows = 2 * HEAD_DIM
    qt = q_ref[...].T

    means = means_ref[...]
    row_in_pair = lax.broadcasted_iota(jnp.int32, (pair_rows, MOBA_BLOCK), 0) // HEAD_DIM
    block_id = lax.broadcasted_iota(jnp.int32, (nb, MOBA_BLOCK), 0)
    for h in range(N_HEADS):
        hs = slice(h * HEAD_DIM, (h + 1) * HEAD_DIM)
        far_bias = rb_ref[NUM_BUCKETS - 1, h] * LOG2E
        block_scores = jnp.dot(means[:, hs], qt[hs, :], precision=HIGHEST, preferred_element_type=F32)
        selected = _select_rows(block_scores, i, axis=0)
        sel_ref[h * nb:(h + 1) * nb, :] = jnp.where(
            selected, jnp.where(block_id == i - 1, 0.0, far_bias), NEG_INF)
        q_pair = qt[(h // 2) * pair_rows:(h // 2 + 1) * pair_rows, :] * (HEAD_DIM ** -0.5 * LOG2E)
        qm_ref[h] = jnp.where(row_in_pair == h % 2, q_pair, 0.0).astype(BF16)

    def scores(j, table_rows, h, s_ref):
        pair = h // 2
        k_pair = kb_ref[j, :, pair * pair_rows:(pair + 1) * pair_rows]
        s = jnp.dot(k_pair, qm_ref[h], preferred_element_type=F32)
        if table_rows is not None:
            s = s + tab_ref[h, table_rows, :]
        s_ref[h] = s

    def consume(j, first, h, s_ref):
        s = s_ref[h]
        s_max = jnp.max(s, axis=0, keepdims=True)
        if first:
            m_new = s_max
            shift = m_new
        else:
            sel = sel_ref[pl.ds(h * nb + j, 1), :]
            m_old = m_ref[h:h + 1, :]
            m_new = jnp.maximum(m_old, s_max + sel)
            alpha = jnp.exp2(m_old - m_new)
            shift = m_new - sel
        m_ref[h:h + 1, :] = m_new
        p = jnp.exp2(s - shift).astype(BF16)
        pv = jnp.dot(vtb_ref[j, h], p, preferred_element_type=F32)
        acc_ref[h] = pv if first else alpha * acc_ref[h] + pv

    own_rows = slice(MOBA_BLOCK, 2 * MOBA_BLOCK)
    prev_rows = slice(0, MOBA_BLOCK)
    n_far = prev
    last_far = jnp.maximum(n_far - 1, 0)
    far = lambda j: jnp.minimum(j, last_far)
    heads = range(N_HEADS)

    def consume_then_refill(j, first, j_refill, s_ref):
        for h in heads:
            consume(j, first, h, s_ref)
            scores(far(j_refill), None, h, s_ref)

    for h in heads:
        scores(i, own_rows, h, s0_ref)
    for h in heads:
        scores(prev, prev_rows, h, s1_ref)
    consume_then_refill(i, True, 0, s0_ref)
    consume_then_refill(prev, False, 1, s1_ref)

    def far_pair(t, carry):
        consume_then_refill(2 * t, False, 2 * t + 2, s0_ref)
        consume_then_refill(2 * t + 1, False, 2 * t + 3, s1_ref)
        return carry

    lax.fori_loop(0, n_far // 2, far_pair, 0)

    @pl.when(n_far % 2 == 1)
    def _():
        for h in heads:
            consume(n_far - 1, False, h, s0_ref)

    out_t = jnp.concatenate(
        [acc_ref[h, :HEAD_DIM, :] / acc_ref[h, HEAD_DIM:HEAD_DIM + 1, :] for h in heads], axis=0)
    o_ref[...] = out_t.T.astype(o_ref.dtype)


def _prompt_attention(rel_bias, q, kb4, vtb4, means, tables):
    b, nb = kb4.shape[:2]
    t = nb * MOBA_BLOCK
    return pl.pallas_call(
        _prompt_attn_kernel,
        grid=(b, nb),
        in_specs=[pl.BlockSpec(memory_space=pltpu.SMEM),
                  pl.BlockSpec((None, MOBA_BLOCK, D_ATTN), lambda bi, i: (bi, i, 0)),
                  pl.BlockSpec((None, nb, MOBA_BLOCK, D_ATTN), lambda bi, i: (bi, 0, 0, 0)),
                  pl.BlockSpec((None, nb, N_HEADS, V_ROWS, MOBA_BLOCK), lambda bi, i: (bi, 0, 0, 0, 0)),
                  pl.BlockSpec((None, nb, D_ATTN), lambda bi, i: (bi, 0, 0)),
                  _resident(tables.shape)],
        out_specs=pl.BlockSpec((None, MOBA_BLOCK, D_ATTN), lambda bi, i: (bi, i, 0)),
        out_shape=jax.ShapeDtypeStruct((b, t, D_ATTN), BF16),
        scratch_shapes=[pltpu.VMEM((N_HEADS * nb, MOBA_BLOCK), F32),
                        pltpu.VMEM((N_HEADS, 2 * HEAD_DIM, MOBA_BLOCK), BF16),
                        pltpu.VMEM((N_HEADS, MOBA_BLOCK), F32),
                        pltpu.VMEM((N_HEADS, V_ROWS, MOBA_BLOCK), F32),
                        pltpu.VMEM((N_HEADS, MOBA_BLOCK, MOBA_BLOCK), F32),
                        pltpu.VMEM((N_HEADS, MOBA_BLOCK, MOBA_BLOCK), F32)],
        compiler_params=_params("parallel", "arbitrary"),
        name="prompt_attention",
    )(rel_bias, q, kb4, vtb4, means, tables)


def _sample_attn_kernel(pt_ref, q_ref, kn_ref, vn_ref, bias_ref, kt_hbm, vt_hbm, o_ref,
                        kbuf_ref, vbuf_ref, knp_ref, vnp_ref, sem):
    step = pl.program_id(0)
    t_new, seqs_per_step, _ = q_ref.shape
    n_slots, n_pages, _, page = kbuf_ref.shape
    n_seq = pl.num_programs(0) * seqs_per_step
    pages_per_block = MOBA_BLOCK // page
    nb = n_pages // pages_per_block
    past_len = n_pages * page

    def page_copies(seq):
        slot = seq % n_slots
        copies = []
        for p in range(n_pages):
            phys = pt_ref[seq, p]
            copies.append(pltpu.make_async_copy(kt_hbm.at[phys], kbuf_ref.at[slot, p], sem.at[slot, 0]))
            copies.append(pltpu.make_async_copy(vt_hbm.at[phys], vbuf_ref.at[slot, p], sem.at[slot, 1]))
        return copies

    lookahead = n_slots - 1

    @pl.when(step == 0)
    def _():
        for first in range(lookahead):
            for copy in page_copies(first):
                copy.start()

    head_of_row = lax.broadcasted_iota(jnp.int32, (N_HEADS, D_ATTN), 0)
    head_of_lane = lax.broadcasted_iota(jnp.int32, (N_HEADS, D_ATTN), 1) // HEAD_DIM
    head_mask = head_of_row == head_of_lane
    lane = lax.broadcasted_iota(jnp.int32, (D_ATTN, page), 1)
    nt = (((1,), (1,)), ((), ()))

    def one_sequence(r, carry):
        seq = step * seqs_per_step + r
        slot = seq % n_slots
        row = pl.ds(r, 1)

        @pl.when(seq + lookahead < n_seq)
        def _():
            for copy in page_copies(seq + lookahead):
                copy.start()

        for copy in page_copies(seq):
            copy.wait()

        qrows = jnp.concatenate(
            [jnp.where(head_mask, jnp.broadcast_to(q_ref[t, row, :], (N_HEADS, D_ATTN)), 0.0)
             for t in range(t_new)], axis=0)

        means_t = jnp.zeros((D_ATTN, page), F32)
        for n in range(nb):
            total = sum(kbuf_ref[slot, n * pages_per_block + i] for i in range(pages_per_block))
            means_t = jnp.where(lane == n, jnp.sum(total, axis=1, keepdims=True) * (1.0 / MOBA_BLOCK), means_t)
        scores = jnp.dot(qrows, means_t, precision=HIGHEST, preferred_element_type=F32)[:, :nb]
        selected = jnp.where(_select_rows(scores, nb, axis=1), 1.0, 0.0)

        qb = (qrows * (HEAD_DIM ** -0.5)).astype(BF16)
        s_pages = []
        for p in range(n_pages):
            s = jnp.dot(qb, kbuf_ref[slot, p].astype(BF16), preferred_element_type=F32)
            keep = selected[:, p // pages_per_block:p // pages_per_block + 1] > 0.5
            s_pages.append(jnp.where(keep, s + bias_ref[:, p * page:(p + 1) * page], NEG_INF))
        s_past = jnp.concatenate(s_pages, axis=1)

        knp_ref[...] = jnp.zeros(knp_ref.shape, F32)
        vnp_ref[...] = jnp.zeros(vnp_ref.shape, F32)
        for t in range(t_new):
            knp_ref[t:t + 1, :] = kn_ref[t, row, :]
            vnp_ref[t:t + 1, :] = vn_ref[t, row, :]
        s_own = (lax.dot_general(qb, knp_ref[...].astype(BF16), nt, preferred_element_type=F32)
                 + bias_ref[:, past_len:])

        m = jnp.maximum(jnp.max(s_past, axis=1, keepdims=True), jnp.max(s_own, axis=1, keepdims=True))
        p_past = jnp.exp(s_past - m)
        p_own = jnp.exp(s_own - m)
        l = jnp.sum(p_past, axis=1, keepdims=True) + jnp.sum(p_own, axis=1, keepdims=True)
        o = jnp.dot(p_own.astype(BF16), vnp_ref[...].astype(BF16), preferred_element_type=F32)
        for p in range(n_pages):
            o = o + lax.dot_general(p_past[:, p * page:(p + 1) * page].astype(BF16),
                                    vbuf_ref[slot, p].astype(BF16), nt, preferred_element_type=F32)
        o = o / l
        for t in range(t_new):
            o_ref[t, row, :] = jnp.sum(jnp.where(head_mask, o[t * N_HEADS:(t + 1) * N_HEADS, :], 0.0),
                                       axis=0, keepdims=True)
        return carry

    lax.fori_loop(0, seqs_per_step, one_sequence, 0)


def _sample_attention(page_table, q, k_new, v_new, bias, cache_kt, cache_vt):
    t_new, bs, _ = q.shape
    n_pages = page_table.shape[1]
    page = cache_kt.shape[2]
    own_width = bias.shape[1] - n_pages * page
    seqs_per_step = SUBLANES
    n_slots = 3
    assert bs % seqs_per_step == 0 and bs >= n_slots
    per_step = pl.BlockSpec((t_new, seqs_per_step, D_ATTN), lambda s, pt: (0, s, 0))
    in_hbm = pl.BlockSpec(memory_space=pl.ANY)
    grid_spec = pltpu.PrefetchScalarGridSpec(
        num_scalar_prefetch=1,
        grid=(bs // seqs_per_step,),
        in_specs=[per_step, per_step, per_step, pl.BlockSpec(bias.shape, lambda s, pt: (0, 0)), in_hbm, in_hbm],
        out_specs=per_step,
        scratch_shapes=[pltpu.VMEM((n_slots, n_pages, D_ATTN, page), F32),
                        pltpu.VMEM((n_slots, n_pages, D_ATTN, page), F32),
                        pltpu.VMEM((own_width, D_ATTN), F32), pltpu.VMEM((own_width, D_ATTN), F32),
                        pltpu.SemaphoreType.DMA((n_slots, 2))],
    )
    return pl.pallas_call(
        _sample_attn_kernel,
        grid_spec=grid_spec,
        out_shape=jax.ShapeDtypeStruct((t_new, bs, D_ATTN), F32),
        compiler_params=_params("arbitrary"),
        name="sample_attention",
    )(page_table, q, k_new, v_new, bias, cache_kt, cache_vt)


def _tail_kernel(chunk, x_ref, oa_ref, orn_ref, gpre_ref, wg_ref, wpa_ref, wpr_ref, wo_ref, gpost_ref,
                 fpre_ref, wgu_ref, wd_ref, fpost_ref, y_ref):
    d_model = x_ref.shape[-1]
    d_ff = wd_ref.shape[0]
    x = x_ref[...]
    hb = _rms(x, gpre_ref[...]).astype(BF16)
    pa = jnp.dot(oa_ref[...].astype(BF16), wpa_ref[...], preferred_element_type=F32)
    merged = jax.nn.sigmoid(jnp.dot(hb, wg_ref[:, :d_model], preferred_element_type=F32)) * pa
    pr = jnp.dot(orn_ref[...].astype(BF16), wpr_ref[...], preferred_element_type=F32)
    merged = merged + jax.nn.sigmoid(jnp.dot(hb, wg_ref[:, d_model:], preferred_element_type=F32)) * pr
    x1 = x + _rms(jnp.dot(merged.astype(BF16), wo_ref[...], preferred_element_type=F32), gpost_ref[...])

    fb = _rms(x1, fpre_ref[...]).astype(BF16)
    y = jnp.zeros(x.shape, F32)
    for c in range(0, d_ff, chunk):
        gate = jnp.dot(fb, wgu_ref[:, c:c + chunk], preferred_element_type=F32)
        up = jnp.dot(fb, wgu_ref[:, d_ff + c:d_ff + c + chunk], preferred_element_type=F32)
        act = (jax.nn.silu(gate) * up).astype(BF16)
        y = y + jnp.dot(act, wd_ref[c:c + chunk, :], preferred_element_type=F32)
    y_ref[...] = x1 + _rms(y, fpost_ref[...])


def _tail(x2d, oa, orn, weights, tm, chunk):
    n, d_model = x2d.shape
    assert n % tm == 0 and weights[8].shape[0] % chunk == 0
    row = lambda width: pl.BlockSpec((tm, width), lambda i: (i, 0))
    return pl.pallas_call(
        functools.partial(_tail_kernel, chunk),
        grid=(n // tm,),
        in_specs=[row(d_model), row(D_ATTN), row(D_RNN)] + [_resident(w.shape) for w in weights],
        out_specs=row(d_model),
        out_shape=jax.ShapeDtypeStruct((n, d_model), F32),
        compiler_params=_params("parallel"),
        name="tail",
    )(x2d, oa, orn, *weights)


def _block_diag(w):
    nblk, c, _ = w.shape
    eye = jnp.eye(nblk, dtype=w.dtype)
    return (w[:, :, None, :] * eye[:, None, :, None]).reshape(nblk * c, nblk * c)


def _rows_from_feature_major(xt):
    g, _, w = xt.shape
    return jnp.transpose(xt.reshape(g, N_HEADS, HEAD_DIM, w), (3, 0, 1, 2))[None]


def kernel(x_prompt, x_sample, cache_k, cache_v, page_table, state_conv, state_h, norm_mix_pre, norm_mix_post,
           w_in, rel_bias, conv_w, conv_b, rg_w_r, rg_b_r, rg_w_i, rg_b_i, rg_lambda, w_proj_attn, w_proj_rnn,
           w_out, norm_ffn_pre, norm_ffn_post, w_gate_up, w_down):
    depth = w_in.shape[0]
    assert depth == 1
    bp, tp, d_model = x_prompt.shape
    bs, ts, _ = x_sample.shape
    n_phys, page = cache_k.shape[1:3]
    n_pages = page_table.shape[1]
    past_len = n_pages * page
    assert tp % MOBA_BLOCK == 0 and past_len % MOBA_BLOCK == 0 and MOBA_BLOCK % page == 0
    assert past_len // MOBA_BLOCK > MOBA_TOP_K and CONV_W - 1 <= ts <= MOBA_BLOCK
    nbp = tp // MOBA_BLOCK
    l = 0
    row2 = lambda v: v.reshape(1, -1)

    n_stream_cols = 3 * D_ATTN + 2 * D_RNN
    w_in_b = w_in[l, :, :n_stream_cols].astype(BF16)
    w_gates = w_in[l, :, n_stream_cols:].astype(BF16)
    rnn_w = (conv_w[l], row2(conv_b[l]), (0.5 * _block_diag(rg_w_r[l])).astype(BF16), row2(0.5 * rg_b_r[l]),
             (0.5 * _block_diag(rg_w_i[l])).astype(BF16), row2(0.5 * rg_b_i[l]), row2(rg_lambda[l]))
    wpa, wpr, wo = w_proj_attn[l].astype(BF16), w_proj_rnn[l].astype(BF16), w_out[l].astype(BF16)
    wgu, wd = w_gate_up[l].astype(BF16), w_down[l].astype(BF16)
    g_pre, g_post = row2(norm_mix_pre[l]), row2(norm_mix_post[l])
    gf_pre, gf_post = row2(norm_ffn_pre[l]), row2(norm_ffn_post[l])

    tail_w = (g_pre, w_gates, wpa, wpr, wo, g_post, gf_pre, wgu, wd, gf_post)

    tm = 512
    xp2 = x_prompt.reshape(bp * tp, d_model)
    q, kb, vtb, means, kt, vt, orn, xtail, h_last = _prompt_in_proj(xp2, g_pre, w_in_b, rnn_w, tm, tp)
    tables = _prompt_bias_tables(rel_bias)
    oa = _prompt_attention(rel_bias, q.reshape(bp, tp, D_ATTN), kb.reshape(bp, nbp, MOBA_BLOCK, D_ATTN),
                           vtb.reshape(bp, nbp, N_HEADS, V_ROWS, MOBA_BLOCK), means.reshape(bp, nbp, D_ATTN), tables)
    y_prompt = _tail(xp2, oa.reshape(bp * tp, D_ATTN), orn, tail_w, tm, 256)
    y_prompt = y_prompt.reshape(bp, tp, d_model)
    new_k_prompt = jnp.swapaxes(_rows_from_feature_major(kt), 1, 2)
    new_v_prompt = jnp.swapaxes(_rows_from_feature_major(vt), 1, 2)
    new_conv_prompt = xtail[:, SUBLANES - (CONV_W - 1):, :][None]
    new_h_prompt = h_last.reshape(1, bp, D_RNN).astype(state_h.dtype)

    ns = bs * ts
    xs2 = jnp.swapaxes(x_sample, 0, 1).reshape(ns, d_model)
    qs, ks, vs, kts, vts, xrs, ggs = _sample_in_proj(xs2, g_pre, w_in_b, bs)
    own_width = 128
    bias_s = _sample_bias_table(rel_bias, past_len, ts, own_width)
    feature_major_pages = lambda c: jnp.transpose(c, (0, 2, 3, 1)).reshape(n_phys, D_ATTN, page)
    tm3 = lambda a: a.reshape(ts, bs, a.shape[-1])
    oas = _sample_attention(page_table, tm3(qs), tm3(ks), tm3(vs), bias_s,
                            feature_major_pages(cache_k[l]), feature_major_pages(cache_v[l]))
    xrs3 = tm3(xrs)
    orns, hs_last = _sample_rnn(xrs3, tm3(ggs), jnp.swapaxes(state_conv[l], 0, 1), state_h[l], rnn_w)
    y_sample = _tail(xs2, oas.reshape(ns, D_ATTN), orns.reshape(ns, D_RNN), tail_w, ns, 256)
    y_sample = jnp.swapaxes(y_sample.reshape(ts, bs, d_model), 0, 1)
    new_k_sample = _rows_from_feature_major(kts)
    new_v_sample = _rows_from_feature_major(vts)
    new_conv_sample = jnp.swapaxes(xrs3[ts - (CONV_W - 1):], 0, 1)[None]
    new_h_sample = hs_last[None].astype(state_h.dtype)

    return (y_prompt, y_sample, new_k_prompt, new_v_prompt, new_k_sample, new_v_sample,
            new_conv_prompt, new_h_prompt, new_conv_sample, new_h_sample)
```

```python
import functools
import math

import numpy as np
import jax
import jax.numpy as jnp
from jax import lax
from jax.experimental import pallas as pl
from jax.experimental.pallas import tpu as pltpu

N_HEADS = 8
HEAD_DIM = 64
D_ATTN = N_HEADS * HEAD_DIM
D_RNN = 512
N_RG_BLOCKS = 8
CONV_W = 4
RG_C = 8.0
MOBA_BLOCK = 256
MOBA_TOP_K = 3
NUM_BUCKETS = 32
MAX_DISTANCE = 128
RMS_EPS = 1e-6
NEG_INF = -1e30
BF16_SUBLANES = 16
V_ROWS = HEAD_DIM + BF16_SUBLANES
LOG2E = math.log2(math.e)

SUBLANES = 8
VMEM_LIMIT = 56 * 1024 * 1024
F32 = jnp.float32
BF16 = jnp.bfloat16
HIGHEST = lax.Precision.HIGHEST


def _params(*sem):
    return pltpu.CompilerParams(dimension_semantics=sem, vmem_limit_bytes=VMEM_LIMIT)


def _resident(shape):
    nd = len(shape)
    return pl.BlockSpec(shape, lambda *_: (0,) * nd, pipeline_mode=pl.Buffered(1))


def _rms(x, g):
    return x * lax.rsqrt(jnp.mean(x * x, axis=-1, keepdims=True) + RMS_EPS) * g


def _t5_bucket_np(n):
    n = np.maximum(n, 0)
    max_exact = NUM_BUCKETS // 2
    nf = np.maximum(n, 1).astype(np.float32)
    large = max_exact + (np.log(nf / np.float32(max_exact)) / np.float32(math.log(MAX_DISTANCE / max_exact))
                         * np.float32(NUM_BUCKETS - max_exact)).astype(np.int32)
    large = np.minimum(large, NUM_BUCKETS - 1)
    return np.where(n < max_exact, n, large).astype(np.int32)


def _rglru_coeffs(xc, wr_ref, br_ref, wi_ref, bi_ref, lam_ref):
    xb = xc.astype(BF16)
    r_plus = jnp.tanh(jnp.dot(xb, wr_ref[...], preferred_element_type=F32) + br_ref[...]) + 1.0
    i_plus = jnp.tanh(jnp.dot(xb, wi_ref[...], preferred_element_type=F32) + bi_ref[...]) + 1.0
    log_a = (-0.5 * RG_C * jax.nn.softplus(-lam_ref[...])) * r_plus
    a = jnp.exp(log_a)
    z = -jnp.tanh(log_a) * (a * a + 1.0)
    b = jnp.where(z > 0.0, z * lax.rsqrt(z), 0.0) * (i_plus * (0.5 * xc))
    return a, b


def _rglru_tile(x, gg, first_tile, cw_ref, cb_ref, wr_ref, br_ref, wi_ref, bi_ref, lam_ref,
                o_ref, hl_ref, xbuf_ref, h_ref, obuf_ref):
    tt = x.shape[0]

    @pl.when(first_tile)
    def _():
        xbuf_ref[...] = jnp.zeros(xbuf_ref.shape, F32)
        h_ref[...] = jnp.zeros(h_ref.shape, F32)

    x_ext = jnp.concatenate([xbuf_ref[...], x], axis=0)
    row = lax.broadcasted_iota(jnp.int32, (tt, D_RNN), 0) % SUBLANES
    xc = cb_ref[...] + x * cw_ref[CONV_W - 1:CONV_W, :]
    for s in range(1, CONV_W):
        xc = xc + pltpu.roll(x_ext, s, axis=0)[SUBLANES:] * cw_ref[CONV_W - 1 - s:CONV_W - s, :]
    xbuf_ref[...] = x[tt - SUBLANES:]
    a, b = _rglru_coeffs(xc, wr_ref, br_ref, wi_ref, bi_ref, lam_ref)

    groups = (tt // SUBLANES, SUBLANES, D_RNN)
    a, b, row = a.reshape(groups), b.reshape(groups), row.reshape(groups)
    shift = 1
    while shift < SUBLANES:
        a_prev = pltpu.roll(a, shift, axis=1)
        b_prev = pltpu.roll(b, shift, axis=1)
        ok = row >= shift
        b = jnp.where(ok, a * b_prev + b, b)
        a = jnp.where(ok, a * a_prev, a)
        shift *= 2
    h = h_ref[...]
    for g in range(tt // SUBLANES):
        sl = slice(g * SUBLANES, (g + 1) * SUBLANES)
        hg = a[g] * h + b[g]
        obuf_ref[sl, :] = hg * gg[sl, :]
        h = hg[SUBLANES - 1:SUBLANES, :]
    o_ref[...] = obuf_ref[...].astype(o_ref.dtype)
    h_ref[...] = h
    hl_ref[...] = h


def _sample_rnn_kernel(xr_ref, gg_ref, conv_ref, h0_ref, cw_ref, cb_ref, wr_ref, br_ref, wi_ref, bi_ref, lam_ref,
                       o_ref, hl_ref):
    t_new = xr_ref.shape[0]
    past = [conv_ref[j] for j in range(CONV_W - 1)] + [xr_ref[t] for t in range(t_new)]
    h = h0_ref[...]
    for t in range(t_new):
        xc = cb_ref[...] + sum(past[t + j] * cw_ref[j:j + 1, :] for j in range(CONV_W))
        a, b = _rglru_coeffs(xc, wr_ref, br_ref, wi_ref, bi_ref, lam_ref)
        h = a * h + b
        o_ref[t] = h * gg_ref[t]
    hl_ref[...] = h


def _sample_rnn(xr_t, gg_t, conv_t, h0, weights):
    return pl.pallas_call(
        _sample_rnn_kernel,
        out_shape=[jax.ShapeDtypeStruct(xr_t.shape, F32), jax.ShapeDtypeStruct(h0.shape, F32)],
        compiler_params=pltpu.CompilerParams(vmem_limit_bytes=VMEM_LIMIT),
        name="sample_rnn",
    )(xr_t, gg_t, conv_t, h0, *weights)


def _projector(x_ref, g_ref, w_ref):
    hb = _rms(x_ref[...], g_ref[...]).astype(BF16)
    return lambda lo, width: jnp.dot(hb, w_ref[:, lo:lo + width], preferred_element_type=F32)


def _store_feature_major(kt_ref, vt_ref, k, v):
    groups, _, width = kt_ref.shape
    for r in range(groups):
        kt_ref[r] = k[r * width:(r + 1) * width, :].T
        vt_ref[r] = v[r * width:(r + 1) * width, :].T


def _sample_in_proj_kernel(x_ref, g_ref, w_ref, q_ref, k_ref, v_ref, kt_ref, vt_ref, xr_ref, gg_ref):
    proj = _projector(x_ref, g_ref, w_ref)
    q_ref[...] = proj(0, D_ATTN)
    k = proj(D_ATTN, D_ATTN)
    v = proj(2 * D_ATTN, D_ATTN)
    k_ref[...] = k
    v_ref[...] = v
    _store_feature_major(kt_ref, vt_ref, k, v)
    xr_ref[...] = proj(3 * D_ATTN, D_RNN)
    gg_ref[...] = jax.nn.gelu(proj(3 * D_ATTN + D_RNN, D_RNN))


def _prompt_in_proj_kernel(tiles_per_seq, x_ref, g_ref, w_ref, *refs):
    rnn_w = refs[:7]
    q_ref, kb_ref, vtb_ref, means_ref, kt_ref, vt_ref, orn_ref, xtail_ref, hl_ref = refs[7:16]
    xbuf_ref, h_ref, obuf_ref = refs[16:]
    proj = _projector(x_ref, g_ref, w_ref)
    xr = proj(3 * D_ATTN, D_RNN)
    gg = jax.nn.gelu(proj(3 * D_ATTN + D_RNN, D_RNN))
    xtail_ref[...] = xr[xr.shape[0] - SUBLANES:]
    _rglru_tile(xr, gg, pl.program_id(0) % tiles_per_seq == 0, *rnn_w, orn_ref, hl_ref, xbuf_ref, h_ref, obuf_ref)
    q_ref[...] = proj(0, D_ATTN)
    k = proj(D_ATTN, D_ATTN)
    v = proj(2 * D_ATTN, D_ATTN)
    _store_feature_major(kt_ref, vt_ref, k, v)
    kb_ref[...] = k.astype(BF16)
    pad_row = lax.broadcasted_iota(jnp.int32, (BF16_SUBLANES, MOBA_BLOCK), 0)
    ones_rows = jnp.where(pad_row == 0, 1.0, 0.0).astype(BF16)
    for r in range(vtb_ref.shape[0]):
        blk = slice(r * MOBA_BLOCK, (r + 1) * MOBA_BLOCK)
        v_t = v[blk, :].T.astype(BF16)
        for h in range(N_HEADS):
            vtb_ref[r, h] = jnp.concatenate([v_t[h * HEAD_DIM:(h + 1) * HEAD_DIM], ones_rows], axis=0)
        means_ref[r] = jnp.sum(k[blk, :], axis=0, keepdims=True) * (1.0 / MOBA_BLOCK)


def _feature_major_spec(n, tm, group_len):
    tiles_per_group = max(group_len // tm, 1)
    block = (max(tm // group_len, 1), D_ATTN, min(tm, group_len))
    spec = pl.BlockSpec(block, lambda i: (i // tiles_per_group, 0, i % tiles_per_group))
    return spec, jax.ShapeDtypeStruct((n // group_len, D_ATTN, group_len), F32)


def _sample_in_proj(x2d, g, w_in_b, group_len):
    n, d_model = x2d.shape
    assert n % group_len == 0
    t_spec, t_shape = _feature_major_spec(n, n, group_len)
    row = lambda width: pl.BlockSpec((n, width), lambda i: (i, 0))
    f32 = lambda width: jax.ShapeDtypeStruct((n, width), F32)
    return pl.pallas_call(
        _sample_in_proj_kernel,
        grid=(1,),
        in_specs=[row(d_model), _resident((1, d_model)), _resident(w_in_b.shape)],
        out_specs=[row(D_ATTN)] * 3 + [t_spec, t_spec, row(D_RNN), row(D_RNN)],
        out_shape=[f32(D_ATTN)] * 3 + [t_shape, t_shape, f32(D_RNN), f32(D_RNN)],
        compiler_params=_params("arbitrary"),
        name="sample_in_proj",
    )(x2d, g, w_in_b)


def _prompt_in_proj(x2d, g, w_in_b, rnn_w, tm, seq_len):
    n, d_model = x2d.shape
    assert n % seq_len == 0 and seq_len % tm == 0 and tm % MOBA_BLOCK == 0
    tiles_per_seq = seq_len // tm
    n_seq = n // seq_len
    blocks = tm // MOBA_BLOCK
    t_spec, t_shape = _feature_major_spec(n, tm, seq_len)
    row = lambda width: pl.BlockSpec((tm, width), lambda i: (i, 0))
    per_seq = lambda rows: pl.BlockSpec((None, rows, D_RNN), lambda i: (i // tiles_per_seq, 0, 0))
    return pl.pallas_call(
        functools.partial(_prompt_in_proj_kernel, tiles_per_seq),
        grid=(n // tm,),
        in_specs=[row(d_model), _resident((1, d_model)), _resident(w_in_b.shape)]
                 + [_resident(w.shape) for w in rnn_w],
        out_specs=[row(D_ATTN), row(D_ATTN),
                   pl.BlockSpec((blocks, N_HEADS, V_ROWS, MOBA_BLOCK), lambda i: (i, 0, 0, 0)),
                   pl.BlockSpec((blocks, 1, D_ATTN), lambda i: (i, 0, 0)),
                   t_spec, t_spec, row(D_RNN), per_seq(SUBLANES), per_seq(1)],
        out_shape=[jax.ShapeDtypeStruct((n, D_ATTN), F32), jax.ShapeDtypeStruct((n, D_ATTN), BF16),
                   jax.ShapeDtypeStruct((n // MOBA_BLOCK, N_HEADS, V_ROWS, MOBA_BLOCK), BF16),
                   jax.ShapeDtypeStruct((n // MOBA_BLOCK, 1, D_ATTN), F32),
                   t_shape, t_shape, jax.ShapeDtypeStruct((n, D_RNN), BF16),
                   jax.ShapeDtypeStruct((n_seq, SUBLANES, D_RNN), F32),
                   jax.ShapeDtypeStruct((n_seq, 1, D_RNN), F32)],
        scratch_shapes=[pltpu.VMEM((SUBLANES, D_RNN), F32), pltpu.VMEM((1, D_RNN), F32),
                        pltpu.VMEM((tm, D_RNN), F32)],
        compiler_params=_params("arbitrary"),
        name="prompt_in_proj",
    )(x2d, g, w_in_b, *rnn_w)


def _bias_table_kernel(rb_ref, bucket_ref, out_ref):
    bucket = bucket_ref[...]
    for h in range(N_HEADS):
        acc = jnp.full(bucket.shape, NEG_INF, F32)
        for b in range(NUM_BUCKETS):
            acc = jnp.where(bucket == b, rb_ref[b, h] * LOG2E, acc)
        out_ref[h] = acc


def _prompt_bias_tables(rel_bias):
    kr = np.arange(2 * MOBA_BLOCK)[:, None]
    qr = np.arange(MOBA_BLOCK)[None, :]
    dist = qr + MOBA_BLOCK - kr
    bucket = np.where(dist >= 0, _t5_bucket_np(dist), -1).astype(np.int32)
    return pl.pallas_call(
        _bias_table_kernel,
        in_specs=[pl.BlockSpec(memory_space=pltpu.SMEM), pl.BlockSpec(memory_space=pltpu.VMEM)],
        out_specs=pl.BlockSpec(memory_space=pltpu.VMEM),
        out_shape=jax.ShapeDtypeStruct((N_HEADS,) + bucket.shape, F32),
        name="prompt_bias_tables",
    )(rel_bias, jnp.asarray(bucket))


def _sample_bias_kernel(rbt_ref, bucket_ref, out_ref):
    bucket = bucket_ref[...]
    acc = jnp.full(bucket.shape, NEG_INF, F32)
    for b in range(NUM_BUCKETS):
        acc = jnp.where(bucket == b, rbt_ref[:, b:b + 1], acc)
    out_ref[...] = acc


def _sample_bias_table(rel_bias, past_len, t_new, own_width):
    t = np.repeat(np.arange(t_new), N_HEADS)[:, None]
    kpos = np.arange(past_len + own_width)[None, :]
    dist = past_len + t - kpos
    valid = (dist >= 0) & (kpos < past_len + t_new)
    bucket = np.where(valid, _t5_bucket_np(dist), -1).astype(np.int32)
    rbt = jnp.tile(rel_bias.T, (t_new, 1))
    return pl.pallas_call(
        _sample_bias_kernel,
        out_shape=jax.ShapeDtypeStruct(bucket.shape, F32),
        name="sample_bias_table",
    )(rbt, jnp.asarray(bucket))


def _select_by_rank(scores, n_valid, axis):
    nb = scores.shape[axis]
    idx = lax.broadcasted_iota(jnp.int32, scores.shape, axis)
    beaten = jnp.zeros(scores.shape, jnp.int32)
    for n in range(nb):
        other = lax.slice_in_dim(scores, n, n + 1, axis=axis)
        beats = (other > scores) | ((other == scores) & (n < idx))
        beaten = beaten + jnp.where(beats, jnp.where(n < n_valid, 1, 0), 0)
    return (beaten < MOBA_TOP_K) & (idx < n_valid)


def _select_by_rounds(scores, n_valid, axis):
    nb = scores.shape[axis]
    idx = lax.broadcasted_iota(jnp.int32, scores.shape, axis)
    valid = idx < n_valid
    left = jnp.where(valid, scores, -jnp.inf)
    selected = jnp.zeros(scores.shape, jnp.bool_)
    for _ in range(MOBA_TOP_K):
        best = jnp.max(left, axis=axis, keepdims=True)
        first = jnp.min(jnp.where(left == best, idx, nb), axis=axis, keepdims=True)
        pick = idx == first
        selected = selected | pick
        left = jnp.where(pick, -jnp.inf, left)
    return selected & valid


def _prompt_attn_kernel(rb_ref, q_ref, kb_ref, vtb_ref, means_ref, tab_ref, o_ref,
                        sel_ref, qm_ref, m_ref, acc_ref, s0_ref, s1_ref, x0_ref, x1_ref):
    i = pl.program_id(1)
    nb = kb_ref.shape[0]
    prev = jnp.maximum(i - 1, 0)
    pair_rows = 2 * HEAD_DIM
    qt = q_ref[...].T

    means = means_ref[...]
    row_in_pair = lax.broadcasted_iota(jnp.int32, (pair_rows, MOBA_BLOCK), 0) // HEAD_DIM
    block_id = lax.broadcasted_iota(jnp.int32, (nb, MOBA_BLOCK), 0)
    for h in range(N_HEADS):
        hs = slice(h * HEAD_DIM, (h + 1) * HEAD_DIM)
        far_bias = rb_ref[NUM_BUCKETS - 1, h] * LOG2E
        block_scores = jnp.dot(means[:, hs], qt[hs, :], precision=HIGHEST, preferred_element_type=F32)
        selected = _select_by_rounds(block_scores, i, axis=0)
        sel_ref[h * nb:(h + 1) * nb, :] = jnp.where(
            selected, jnp.where(block_id == i - 1, 0.0, far_bias), NEG_INF)
        q_pair = qt[(h // 2) * pair_rows:(h // 2 + 1) * pair_rows, :] * (HEAD_DIM ** -0.5 * LOG2E)
        qm_ref[h] = jnp.where(row_in_pair == h % 2, q_pair, 0.0).astype(BF16)

    even, odd = (s0_ref, x0_ref), (s1_ref, x1_ref)

    def scores(j, table_rows, h, buf):
        s_ref, smax_ref = buf
        pair = h // 2
        k_pair = kb_ref[j, :, pair * pair_rows:(pair + 1) * pair_rows]
        s = jnp.dot(k_pair, qm_ref[h], preferred_element_type=F32)
        if table_rows is not None:
            s = s + tab_ref[h, table_rows, :]
        s_ref[h] = s
        smax_ref[h:h + 1, :] = jnp.max(s, axis=0, keepdims=True)

    def consume(j, first, h, buf):
        s_ref, smax_ref = buf
        s = s_ref[h]
        s_max = smax_ref[h:h + 1, :]
        if first:
            m_new = s_max
            shift = m_new
        else:
            sel = sel_ref[pl.ds(h * nb + j, 1), :]
            m_old = m_ref[h:h + 1, :]
            m_new = jnp.maximum(m_old, s_max + sel)
            alpha = jnp.exp2(m_old - m_new)
            shift = m_new - sel
        m_ref[h:h + 1, :] = m_new
        p = jnp.exp2(s - shift).astype(BF16)
        pv = jnp.dot(vtb_ref[j, h], p, preferred_element_type=F32)
        acc_ref[h] = pv if first else alpha * acc_ref[h] + pv

    own_rows = slice(MOBA_BLOCK, 2 * MOBA_BLOCK)
    prev_rows = slice(0, MOBA_BLOCK)
    n_far = prev
    last_far = jnp.maximum(n_far - 1, 0)
    far = lambda j: jnp.minimum(j, last_far)
    heads = range(N_HEADS)

    def consume_then_refill(j, first, j_refill, buf):
        for h in heads:
            consume(j, first, h, buf)
            scores(far(j_refill), None, h, buf)

    for h in heads:
        scores(i, own_rows, h, even)
    for h in heads:
        scores(prev, prev_rows, h, odd)
    consume_then_refill(i, True, 0, even)
    consume_then_refill(prev, False, 1, odd)

    def far_pair(t, carry):
        consume_then_refill(2 * t, False, 2 * t + 2, even)
        consume_then_refill(2 * t + 1, False, 2 * t + 3, odd)
        return carry

    lax.fori_loop(0, n_far // 2, far_pair, 0)

    @pl.when(n_far % 2 == 1)
    def _():
        for h in heads:
            consume(n_far - 1, False, h, even)

    out_t = jnp.concatenate(
        [acc_ref[h, :HEAD_DIM, :] / acc_ref[h, HEAD_DIM:HEAD_DIM + 1, :] for h in heads], axis=0)
    o_ref[...] = out_t.T.astype(o_ref.dtype)


def _prompt_attention(rel_bias, q, kb4, vtb4, means, tables):
    b, nb = kb4.shape[:2]
    t = nb * MOBA_BLOCK
    return pl.pallas_call(
        _prompt_attn_kernel,
        grid=(b, nb),
        in_specs=[pl.BlockSpec(memory_space=pltpu.SMEM),
                  pl.BlockSpec((None, MOBA_BLOCK, D_ATTN), lambda bi, i: (bi, i, 0)),
                  pl.BlockSpec((None, nb, MOBA_BLOCK, D_ATTN), lambda bi, i: (bi, 0, 0, 0)),
                  pl.BlockSpec((None, nb, N_HEADS, V_ROWS, MOBA_BLOCK), lambda bi, i: (bi, 0, 0, 0, 0)),
                  pl.BlockSpec((None, nb, D_ATTN), lambda bi, i: (bi, 0, 0)),
                  _resident(tables.shape)],
        out_specs=pl.BlockSpec((None, MOBA_BLOCK, D_ATTN), lambda bi, i: (bi, i, 0)),
        out_shape=jax.ShapeDtypeStruct((b, t, D_ATTN), BF16),
        scratch_shapes=[pltpu.VMEM((N_HEADS * nb, MOBA_BLOCK), F32),
                        pltpu.VMEM((N_HEADS, 2 * HEAD_DIM, MOBA_BLOCK), BF16),
                        pltpu.VMEM((N_HEADS, MOBA_BLOCK), F32),
                        pltpu.VMEM((N_HEADS, V_ROWS, MOBA_BLOCK), F32),
                        pltpu.VMEM((N_HEADS, MOBA_BLOCK, MOBA_BLOCK), F32),
                        pltpu.VMEM((N_HEADS, MOBA_BLOCK, MOBA_BLOCK), F32),
                        pltpu.VMEM((N_HEADS, MOBA_BLOCK), F32),
                        pltpu.VMEM((N_HEADS, MOBA_BLOCK), F32)],
        compiler_params=_params("parallel", "arbitrary"),
        name="prompt_attention",
    )(rel_bias, q, kb4, vtb4, means, tables)


def _sample_attn_kernel(pt_ref, q_ref, kn_ref, vn_ref, bias_ref, kt_hbm, vt_hbm, o_ref,
                        kbuf_ref, vbuf_ref, knp_ref, vnp_ref, sem):
    step = pl.program_id(0)
    t_new, seqs_per_step, _ = q_ref.shape
    n_slots, n_pages, _, page = kbuf_ref.shape
    n_seq = pl.num_programs(0) * seqs_per_step
    pages_per_block = MOBA_BLOCK // page
    nb = n_pages // pages_per_block
    past_len = n_pages * page

    def page_copies(seq):
        slot = seq % n_slots
        copies = []
        for p in range(n_pages):
            phys = pt_ref[seq, p]
            copies.append(pltpu.make_async_copy(kt_hbm.at[phys], kbuf_ref.at[slot, p], sem.at[slot, 0]))
            copies.append(pltpu.make_async_copy(vt_hbm.at[phys], vbuf_ref.at[slot, p], sem.at[slot, 1]))
        return copies

    lookahead = n_slots - 1

    @pl.when(step == 0)
    def _():
        for first in range(lookahead):
            for copy in page_copies(first):
                copy.start()

    head_of_row = lax.broadcasted_iota(jnp.int32, (N_HEADS, D_ATTN), 0)
    head_of_lane = lax.broadcasted_iota(jnp.int32, (N_HEADS, D_ATTN), 1) // HEAD_DIM
    head_mask = head_of_row == head_of_lane
    lane = lax.broadcasted_iota(jnp.int32, (D_ATTN, page), 1)
    nt = (((1,), (1,)), ((), ()))

    def one_sequence(r, carry):
        seq = step * seqs_per_step + r
        slot = seq % n_slots
        row = pl.ds(r, 1)

        @pl.when(seq + lookahead < n_seq)
        def _():
            for copy in page_copies(seq + lookahead):
                copy.start()

        for copy in page_copies(seq):
            copy.wait()

        qrows = jnp.concatenate(
            [jnp.where(head_mask, jnp.broadcast_to(q_ref[t, row, :], (N_HEADS, D_ATTN)), 0.0)
             for t in range(t_new)], axis=0)

        means_t = jnp.zeros((D_ATTN, page), F32)
        for n in range(nb):
            total = sum(kbuf_ref[slot, n * pages_per_block + i] for i in range(pages_per_block))
            means_t = jnp.where(lane == n, jnp.sum(total, axis=1, keepdims=True) * (1.0 / MOBA_BLOCK), means_t)
        scores = jnp.dot(qrows, means_t, precision=HIGHEST, preferred_element_type=F32)[:, :nb]
        selected = jnp.where(_select_by_rank(scores, nb, axis=1), 1.0, 0.0)

        qb = (qrows * (HEAD_DIM ** -0.5)).astype(BF16)
        s_pages = []
        for p in range(n_pages):
            s = jnp.dot(qb, kbuf_ref[slot, p].astype(BF16), preferred_element_type=F32)
            keep = selected[:, p // pages_per_block:p // pages_per_block + 1] > 0.5
            s_pages.append(jnp.where(keep, s + bias_ref[:, p * page:(p + 1) * page], NEG_INF))
        s_past = jnp.concatenate(s_pages, axis=1)

        knp_ref[...] = jnp.zeros(knp_ref.shape, F32)
        vnp_ref[...] = jnp.zeros(vnp_ref.shape, F32)
        for t in range(t_new):
            knp_ref[t:t + 1, :] = kn_ref[t, row, :]
            vnp_ref[t:t + 1, :] = vn_ref[t, row, :]
        s_own = (lax.dot_general(qb, knp_ref[...].astype(BF16), nt, preferred_element_type=F32)
                 + bias_ref[:, past_len:])

        m = jnp.maximum(jnp.max(s_past, axis=1, keepdims=True), jnp.max(s_own, axis=1, keepdims=True))
        p_past = jnp.exp(s_past - m)
        p_own = jnp.exp(s_own - m)
        l = jnp.sum(p_past, axis=1, keepdims=True) + jnp.sum(p_own, axis=1, keepdims=True)
        o = jnp.dot(p_own.astype(BF16), vnp_ref[...].astype(BF16), preferred_element_type=F32)
        for p in range(n_pages):
            o = o + lax.dot_general(p_past[:, p * page:(p + 1) * page].astype(BF16),
                                    vbuf_ref[slot, p].astype(BF16), nt, preferred_element_type=F32)
        o = o / l
        for t in range(t_new):
            o_ref[t, row, :] = jnp.sum(jnp.where(head_mask, o[t * N_HEADS:(t + 1) * N_HEADS, :], 0.0),
                                       axis=0, keepdims=True)
        return carry

    lax.fori_loop(0, seqs_per_step, one_sequence, 0)


def _sample_attention(page_table, q, k_new, v_new, bias, cache_kt, cache_vt):
    t_new, bs, _ = q.shape
    n_pages = page_table.shape[1]
    page = cache_kt.shape[2]
    own_width = bias.shape[1] - n_pages * page
    seqs_per_step = SUBLANES
    n_slots = 3
    assert bs % seqs_per_step == 0 and bs >= n_slots
    per_step = pl.BlockSpec((t_new, seqs_per_step, D_ATTN), lambda s, pt: (0, s, 0))
    in_hbm = pl.BlockSpec(memory_space=pl.ANY)
    grid_spec = pltpu.PrefetchScalarGridSpec(
        num_scalar_prefetch=1,
        grid=(bs // seqs_per_step,),
        in_specs=[per_step, per_step, per_step, pl.BlockSpec(bias.shape, lambda s, pt: (0, 0)), in_hbm, in_hbm],
        out_specs=per_step,
        scratch_shapes=[pltpu.VMEM((n_slots, n_pages, D_ATTN, page), F32),
                        pltpu.VMEM((n_slots, n_pages, D_ATTN, page), F32),
                        pltpu.VMEM((own_width, D_ATTN), F32), pltpu.VMEM((own_width, D_ATTN), F32),
                        pltpu.SemaphoreType.DMA((n_slots, 2))],
    )
    return pl.pallas_call(
        _sample_attn_kernel,
        grid_spec=grid_spec,
        out_shape=jax.ShapeDtypeStruct((t_new, bs, D_ATTN), F32),
        compiler_params=_params("arbitrary"),
        name="sample_attention",
    )(page_table, q, k_new, v_new, bias, cache_kt, cache_vt)


def _tail_kernel(chunk, x_ref, oa_ref, orn_ref, gpre_ref, wg_ref, wpa_ref, wpr_ref, wo_ref, gpost_ref,
                 fpre_ref, wgu_ref, wd_ref, fpost_ref, y_ref):
    d_model = x_ref.shape[-1]
    d_ff = wd_ref.shape[0]
    x = x_ref[...]
    hb = _rms(x, gpre_ref[...]).astype(BF16)
    pa = jnp.dot(oa_ref[...].astype(BF16), wpa_ref[...], preferred_element_type=F32)
    merged = jax.nn.sigmoid(jnp.dot(hb, wg_ref[:, :d_model], preferred_element_type=F32)) * pa
    pr = jnp.dot(orn_ref[...].astype(BF16), wpr_ref[...], preferred_element_type=F32)
    merged = merged + jax.nn.sigmoid(jnp.dot(hb, wg_ref[:, d_model:], preferred_element_type=F32)) * pr
    x1 = x + _rms(jnp.dot(merged.astype(BF16), wo_ref[...], preferred_element_type=F32), gpost_ref[...])

    fb = _rms(x1, fpre_ref[...]).astype(BF16)
    y = jnp.zeros(x.shape, F32)
    for c in range(0, d_ff, chunk):
        gate = jnp.dot(fb, wgu_ref[:, c:c + chunk], preferred_element_type=F32)
        up = jnp.dot(fb, wgu_ref[:, d_ff + c:d_ff + c + chunk], preferred_element_type=F32)
        act = (jax.nn.silu(gate) * up).astype(BF16)
        y = y + jnp.dot(act, wd_ref[c:c + chunk, :], preferred_element_type=F32)
    y_ref[...] = x1 + _rms(y, fpost_ref[...])


def _tail(x2d, oa, orn, weights, tm, chunk):
    n, d_model = x2d.shape
    assert n % tm == 0 and weights[8].shape[0] % chunk == 0
    row = lambda width: pl.BlockSpec((tm, width), lambda i: (i, 0))
    return pl.pallas_call(
        functools.partial(_tail_kernel, chunk),
        grid=(n // tm,),
        in_specs=[row(d_model), row(D_ATTN), row(D_RNN)] + [_resident(w.shape) for w in weights],
        out_specs=row(d_model),
        out_shape=jax.ShapeDtypeStruct((n, d_model), F32),
        compiler_params=_params("parallel"),
        name="tail",
    )(x2d, oa, orn, *weights)


def _block_diag(w):
    nblk, c, _ = w.shape
    eye = jnp.eye(nblk, dtype=w.dtype)
    return (w[:, :, None, :] * eye[:, None, :, None]).reshape(nblk * c, nblk * c)


def _rows_from_feature_major(xt):
    g, _, w = xt.shape
    return jnp.transpose(xt.reshape(g, N_HEADS, HEAD_DIM, w), (3, 0, 1, 2))[None]


def kernel(x_prompt, x_sample, cache_k, cache_v, page_table, state_conv, state_h, norm_mix_pre, norm_mix_post,
           w_in, rel_bias, conv_w, conv_b, rg_w_r, rg_b_r, rg_w_i, rg_b_i, rg_lambda, w_proj_attn, w_proj_rnn,
           w_out, norm_ffn_pre, norm_ffn_post, w_gate_up, w_down):
    depth = w_in.shape[0]
    assert depth == 1
    bp, tp, d_model = x_prompt.shape
    bs, ts, _ = x_sample.shape
    n_phys, page = cache_k.shape[1:3]
    n_pages = page_table.shape[1]
    past_len = n_pages * page
    assert tp % MOBA_BLOCK == 0 and past_len % MOBA_BLOCK == 0 and MOBA_BLOCK % page == 0
    assert past_len // MOBA_BLOCK > MOBA_TOP_K and CONV_W - 1 <= ts <= MOBA_BLOCK
    nbp = tp // MOBA_BLOCK
    l = 0
    row2 = lambda v: v.reshape(1, -1)

    n_stream_cols = 3 * D_ATTN + 2 * D_RNN
    w_in_b = w_in[l, :, :n_stream_cols].astype(BF16)
    w_gates = w_in[l, :, n_stream_cols:].astype(BF16)
    rnn_w = (conv_w[l], row2(conv_b[l]), (0.5 * _block_diag(rg_w_r[l])).astype(BF16), row2(0.5 * rg_b_r[l]),
             (0.5 * _block_diag(rg_w_i[l])).astype(BF16), row2(0.5 * rg_b_i[l]), row2(rg_lambda[l]))
    wpa, wpr, wo = w_proj_attn[l].astype(BF16), w_proj_rnn[l].astype(BF16), w_out[l].astype(BF16)
    wgu, wd = w_gate_up[l].astype(BF16), w_down[l].astype(BF16)
    g_pre, g_post = row2(norm_mix_pre[l]), row2(norm_mix_post[l])
    gf_pre, gf_post = row2(norm_ffn_pre[l]), row2(norm_ffn_post[l])

    tail_w = (g_pre, w_gates, wpa, wpr, wo, g_post, gf_pre, wgu, wd, gf_post)

    tm = 512
    xp2 = x_prompt.reshape(bp * tp, d_model)
    q, kb, vtb, means, kt, vt, orn, xtail, h_last = _prompt_in_proj(xp2, g_pre, w_in_b, rnn_w, tm, tp)
    tables = _prompt_bias_tables(rel_bias)
    oa = _prompt_attention(rel_bias, q.reshape(bp, tp, D_ATTN), kb.reshape(bp, nbp, MOBA_BLOCK, D_ATTN),
                           vtb.reshape(bp, nbp, N_HEADS, V_ROWS, MOBA_BLOCK), means.reshape(bp, nbp, D_ATTN), tables)
    y_prompt = _tail(xp2, oa.reshape(bp * tp, D_ATTN), orn, tail_w, tm, 256)
    y_prompt = y_prompt.reshape(bp, tp, d_model)
    new_k_prompt = jnp.swapaxes(_rows_from_feature_major(kt), 1, 2)
    new_v_prompt = jnp.swapaxes(_rows_from_feature_major(vt), 1, 2)
    new_conv_prompt = xtail[:, SUBLANES - (CONV_W - 1):, :][None]
    new_h_prompt = h_last.reshape(1, bp, D_RNN).astype(state_h.dtype)

    ns = bs * ts
    xs2 = jnp.swapaxes(x_sample, 0, 1).reshape(ns, d_model)
    qs, ks, vs, kts, vts, xrs, ggs = _sample_in_proj(xs2, g_pre, w_in_b, bs)
    own_width = 128
    bias_s = _sample_bias_table(rel_bias, past_len, ts, own_width)
    feature_major_pages = lambda c: jnp.transpose(c, (0, 2, 3, 1)).reshape(n_phys, D_ATTN, page)
    tm3 = lambda a: a.reshape(ts, bs, a.shape[-1])
    oas = _sample_attention(page_table, tm3(qs), tm3(ks), tm3(vs), bias_s,
                            feature_major_pages(cache_k[l]), feature_major_pages(cache_v[l]))
    xrs3 = tm3(xrs)
    orns, hs_last = _sample_rnn(xrs3, tm3(ggs), jnp.swapaxes(state_conv[l], 0, 1), state_h[l], rnn_w)
    y_sample = _tail(xs2, oas.reshape(ns, D_ATTN), orns.reshape(ns, D_RNN), tail_w, ns, 256)
    y_sample = jnp.swapaxes(y_sample.reshape(ts, bs, d_model), 0, 1)
    new_k_sample = _rows_from_feature_major(kts)
    new_v_sample = _rows_from_feature_major(vts)
    new_conv_sample = jnp.swapaxes(xrs3[ts - (CONV_W - 1):], 0, 1)[None]
    new_h_sample = hs_last[None].astype(state_h.dtype)

    return (y_prompt, y_sample, new_k_prompt, new_v_prompt, new_k_sample, new_v_sample,
            new_conv_prompt, new_h_prompt, new_conv_sample, new_h_sample)
```

```python
import functools
import math

import numpy as np
import jax
import jax.numpy as jnp
from jax import lax
from jax.experimental import pallas as pl
from jax.experimental.pallas import tpu as pltpu

N_HEADS = 8
HEAD_DIM = 64
D_ATTN = N_HEADS * HEAD_DIM
D_RNN = 512
N_RG_BLOCKS = 8
CONV_W = 4
RG_C = 8.0
MOBA_BLOCK = 256
MOBA_TOP_K = 3
NUM_BUCKETS = 32
MAX_DISTANCE = 128
RMS_EPS = 1e-6
NEG_INF = -1e30
BF16_SUBLANES = 16
V_ROWS = HEAD_DIM + BF16_SUBLANES
LOG2E = math.log2(math.e)

SUBLANES = 8
LANES = 128
ROW_TILE = 512
FFN_CHUNK = 256
VMEM_LIMIT = 56 * 1024 * 1024
F32 = jnp.float32
BF16 = jnp.bfloat16
HIGHEST = lax.Precision.HIGHEST


def _params(*sem):
    return pltpu.CompilerParams(dimension_semantics=sem, vmem_limit_bytes=VMEM_LIMIT)


def _resident(shape):
    nd = len(shape)
    return pl.BlockSpec(shape, lambda *_: (0,) * nd, pipeline_mode=pl.Buffered(1))


def _rms(x, g):
    return x * lax.rsqrt(jnp.mean(x * x, axis=-1, keepdims=True) + RMS_EPS) * g


def _t5_bucket_np(n):
    n = np.maximum(n, 0)
    max_exact = NUM_BUCKETS // 2
    nf = np.maximum(n, 1).astype(np.float32)
    large = max_exact + (np.log(nf / np.float32(max_exact)) / np.float32(math.log(MAX_DISTANCE / max_exact))
                         * np.float32(NUM_BUCKETS - max_exact)).astype(np.int32)
    large = np.minimum(large, NUM_BUCKETS - 1)
    return np.where(n < max_exact, n, large).astype(np.int32)


def _rglru_coeffs(xc, wr_ref, br_ref, wi_ref, bi_ref, lam_ref):
    xb = xc.astype(BF16)
    r_plus = jnp.tanh(jnp.dot(xb, wr_ref[...], preferred_element_type=F32) + br_ref[...]) + 1.0
    i_plus = jnp.tanh(jnp.dot(xb, wi_ref[...], preferred_element_type=F32) + bi_ref[...]) + 1.0
    log_a = (-0.5 * RG_C * jax.nn.softplus(-lam_ref[...])) * r_plus
    a = jnp.exp(log_a)
    z = -jnp.tanh(log_a) * (a * a + 1.0)
    b = jnp.where(z > 0.0, z * lax.rsqrt(z), 0.0) * (i_plus * (0.5 * xc))
    return a, b


def _rglru_tile(x, gg, first_tile, cw_ref, cb_ref, wr_ref, br_ref, wi_ref, bi_ref, lam_ref,
                o_ref, hl_ref, xbuf_ref, h_ref, obuf_ref):
    tt = x.shape[0]

    @pl.when(first_tile)
    def _():
        xbuf_ref[...] = jnp.zeros(xbuf_ref.shape, F32)
        h_ref[...] = jnp.zeros(h_ref.shape, F32)

    x_ext = jnp.concatenate([xbuf_ref[...], x], axis=0)
    row = lax.broadcasted_iota(jnp.int32, (tt, D_RNN), 0) % SUBLANES
    xc = cb_ref[...] + x * cw_ref[CONV_W - 1:CONV_W, :]
    for s in range(1, CONV_W):
        xc = xc + pltpu.roll(x_ext, s, axis=0)[SUBLANES:] * cw_ref[CONV_W - 1 - s:CONV_W - s, :]
    xbuf_ref[...] = x[tt - SUBLANES:]
    a, b = _rglru_coeffs(xc, wr_ref, br_ref, wi_ref, bi_ref, lam_ref)

    groups = (tt // SUBLANES, SUBLANES, D_RNN)
    a, b, row = a.reshape(groups), b.reshape(groups), row.reshape(groups)
    shift = 1
    while shift < SUBLANES:
        a_prev = pltpu.roll(a, shift, axis=1)
        b_prev = pltpu.roll(b, shift, axis=1)
        ok = row >= shift
        b = jnp.where(ok, a * b_prev + b, b)
        a = jnp.where(ok, a * a_prev, a)
        shift *= 2
    h = h_ref[...]
    for g in range(tt // SUBLANES):
        sl = slice(g * SUBLANES, (g + 1) * SUBLANES)
        hg = a[g] * h + b[g]
        obuf_ref[sl, :] = hg * gg[sl, :]
        h = hg[SUBLANES - 1:SUBLANES, :]
    o_ref[...] = obuf_ref[...].astype(o_ref.dtype)
    h_ref[...] = h
    hl_ref[...] = h


def _sample_rnn_kernel(xr_ref, gg_ref, conv_ref, h0_ref, cw_ref, cb_ref, wr_ref, br_ref, wi_ref, bi_ref, lam_ref,
                       o_ref, hl_ref):
    t_new = xr_ref.shape[0]
    past = [conv_ref[j] for j in range(CONV_W - 1)] + [xr_ref[t] for t in range(t_new)]
    h = h0_ref[...]
    for t in range(t_new):
        xc = cb_ref[...] + sum(past[t + j] * cw_ref[j:j + 1, :] for j in range(CONV_W))
        a, b = _rglru_coeffs(xc, wr_ref, br_ref, wi_ref, bi_ref, lam_ref)
        h = a * h + b
        o_ref[t] = h * gg_ref[t]
    hl_ref[...] = h


def _sample_rnn(xr_t, gg_t, conv_t, h0, weights):
    return pl.pallas_call(
        _sample_rnn_kernel,
        out_shape=[jax.ShapeDtypeStruct(xr_t.shape, F32), jax.ShapeDtypeStruct(h0.shape, F32)],
        compiler_params=pltpu.CompilerParams(vmem_limit_bytes=VMEM_LIMIT),
        name="sample_rnn",
    )(xr_t, gg_t, conv_t, h0, *weights)


def _projector(x_ref, g_ref, w_ref):
    hb = _rms(x_ref[...], g_ref[...]).astype(BF16)
    return lambda lo, width: jnp.dot(hb, w_ref[:, lo:lo + width], preferred_element_type=F32)


def _store_feature_major(kt_ref, vt_ref, k, v):
    groups, _, width = kt_ref.shape
    for r in range(groups):
        kt_ref[r] = k[r * width:(r + 1) * width, :].T
        vt_ref[r] = v[r * width:(r + 1) * width, :].T


def _sample_in_proj_kernel(x_ref, g_ref, w_ref, q_ref, k_ref, v_ref, kt_ref, vt_ref, xr_ref, gg_ref):
    proj = _projector(x_ref, g_ref, w_ref)
    q_ref[...] = proj(0, D_ATTN)
    k = proj(D_ATTN, D_ATTN)
    v = proj(2 * D_ATTN, D_ATTN)
    k_ref[...] = k
    v_ref[...] = v
    _store_feature_major(kt_ref, vt_ref, k, v)
    xr_ref[...] = proj(3 * D_ATTN, D_RNN)
    gg_ref[...] = jax.nn.gelu(proj(3 * D_ATTN + D_RNN, D_RNN))


def _prompt_in_proj_kernel(tiles_per_seq, x_ref, g_ref, w_ref, *refs):
    rnn_w = refs[:7]
    q_ref, kb_ref, vtb_ref, means_ref, kt_ref, vt_ref, orn_ref, xtail_ref, hl_ref = refs[7:16]
    xbuf_ref, h_ref, obuf_ref = refs[16:]
    proj = _projector(x_ref, g_ref, w_ref)
    xr = proj(3 * D_ATTN, D_RNN)
    gg = jax.nn.gelu(proj(3 * D_ATTN + D_RNN, D_RNN))
    xtail_ref[...] = xr[xr.shape[0] - SUBLANES:]
    _rglru_tile(xr, gg, pl.program_id(0) % tiles_per_seq == 0, *rnn_w, orn_ref, hl_ref, xbuf_ref, h_ref, obuf_ref)
    q_ref[...] = proj(0, D_ATTN)
    k = proj(D_ATTN, D_ATTN)
    v = proj(2 * D_ATTN, D_ATTN)
    _store_feature_major(kt_ref, vt_ref, k, v)
    kb_ref[...] = k.astype(BF16)
    pad_row = lax.broadcasted_iota(jnp.int32, (BF16_SUBLANES, MOBA_BLOCK), 0)
    ones_rows = jnp.where(pad_row == 0, 1.0, 0.0).astype(BF16)
    for r in range(vtb_ref.shape[0]):
        blk = slice(r * MOBA_BLOCK, (r + 1) * MOBA_BLOCK)
        v_t = v[blk, :].T.astype(BF16)
        for h in range(N_HEADS):
            vtb_ref[r, h] = jnp.concatenate([v_t[h * HEAD_DIM:(h + 1) * HEAD_DIM], ones_rows], axis=0)
        means_ref[r] = jnp.sum(k[blk, :], axis=0, keepdims=True) * (1.0 / MOBA_BLOCK)


def _feature_major_spec(n, tm, group_len):
    tiles_per_group = max(group_len // tm, 1)
    block = (max(tm // group_len, 1), D_ATTN, min(tm, group_len))
    spec = pl.BlockSpec(block, lambda i: (i // tiles_per_group, 0, i % tiles_per_group))
    return spec, jax.ShapeDtypeStruct((n // group_len, D_ATTN, group_len), F32)


def _sample_in_proj(x2d, g, w_in_b, group_len):
    n, d_model = x2d.shape
    assert n % group_len == 0
    t_spec, t_shape = _feature_major_spec(n, n, group_len)
    row = lambda width: pl.BlockSpec((n, width), lambda i: (i, 0))
    f32 = lambda width: jax.ShapeDtypeStruct((n, width), F32)
    return pl.pallas_call(
        _sample_in_proj_kernel,
        grid=(1,),
        in_specs=[row(d_model), _resident((1, d_model)), _resident(w_in_b.shape)],
        out_specs=[row(D_ATTN)] * 3 + [t_spec, t_spec, row(D_RNN), row(D_RNN)],
        out_shape=[f32(D_ATTN)] * 3 + [t_shape, t_shape, f32(D_RNN), f32(D_RNN)],
        compiler_params=_params("arbitrary"),
        name="sample_in_proj",
    )(x2d, g, w_in_b)


def _prompt_in_proj(x2d, g, w_in_b, rnn_w, tm, seq_len):
    n, d_model = x2d.shape
    assert n % seq_len == 0 and seq_len % tm == 0 and tm % MOBA_BLOCK == 0
    tiles_per_seq = seq_len // tm
    n_seq = n // seq_len
    blocks = tm // MOBA_BLOCK
    t_spec, t_shape = _feature_major_spec(n, tm, seq_len)
    row = lambda width: pl.BlockSpec((tm, width), lambda i: (i, 0))
    per_seq = lambda rows: pl.BlockSpec((None, rows, D_RNN), lambda i: (i // tiles_per_seq, 0, 0))
    return pl.pallas_call(
        functools.partial(_prompt_in_proj_kernel, tiles_per_seq),
        grid=(n // tm,),
        in_specs=[row(d_model), _resident((1, d_model)), _resident(w_in_b.shape)]
                 + [_resident(w.shape) for w in rnn_w],
        out_specs=[row(D_ATTN), row(D_ATTN),
                   pl.BlockSpec((blocks, N_HEADS, V_ROWS, MOBA_BLOCK), lambda i: (i, 0, 0, 0)),
                   pl.BlockSpec((blocks, 1, D_ATTN), lambda i: (i, 0, 0)),
                   t_spec, t_spec, row(D_RNN), per_seq(SUBLANES), per_seq(1)],
        out_shape=[jax.ShapeDtypeStruct((n, D_ATTN), F32), jax.ShapeDtypeStruct((n, D_ATTN), BF16),
                   jax.ShapeDtypeStruct((n // MOBA_BLOCK, N_HEADS, V_ROWS, MOBA_BLOCK), BF16),
                   jax.ShapeDtypeStruct((n // MOBA_BLOCK, 1, D_ATTN), F32),
                   t_shape, t_shape, jax.ShapeDtypeStruct((n, D_RNN), BF16),
                   jax.ShapeDtypeStruct((n_seq, SUBLANES, D_RNN), F32),
                   jax.ShapeDtypeStruct((n_seq, 1, D_RNN), F32)],
        scratch_shapes=[pltpu.VMEM((SUBLANES, D_RNN), F32), pltpu.VMEM((1, D_RNN), F32),
                        pltpu.VMEM((tm, D_RNN), F32)],
        compiler_params=_params("arbitrary"),
        name="prompt_in_proj",
    )(x2d, g, w_in_b, *rnn_w)


def _bias_table_kernel(rb_ref, bucket_ref, out_ref):
    bucket = bucket_ref[...]
    for h in range(N_HEADS):
        acc = jnp.full(bucket.shape, NEG_INF, F32)
        for b in range(NUM_BUCKETS):
            acc = jnp.where(bucket == b, rb_ref[b, h] * LOG2E, acc)
        out_ref[h] = acc


def _prompt_bias_tables(rel_bias):
    kr = np.arange(2 * MOBA_BLOCK)[:, None]
    qr = np.arange(MOBA_BLOCK)[None, :]
    dist = qr + MOBA_BLOCK - kr
    bucket = np.where(dist >= 0, _t5_bucket_np(dist), -1).astype(np.int32)
    return pl.pallas_call(
        _bias_table_kernel,
        in_specs=[pl.BlockSpec(memory_space=pltpu.SMEM), pl.BlockSpec(memory_space=pltpu.VMEM)],
        out_specs=pl.BlockSpec(memory_space=pltpu.VMEM),
        out_shape=jax.ShapeDtypeStruct((N_HEADS,) + bucket.shape, F32),
        name="prompt_bias_tables",
    )(rel_bias, jnp.asarray(bucket))


def _sample_bias_kernel(rbt_ref, bucket_ref, out_ref):
    bucket = bucket_ref[...]
    acc = jnp.full(bucket.shape, NEG_INF, F32)
    for b in range(NUM_BUCKETS):
        acc = jnp.where(bucket == b, rbt_ref[:, b:b + 1], acc)
    out_ref[...] = acc


def _sample_bias_table(rel_bias, past_len, t_new, own_width):
    t = np.repeat(np.arange(t_new), N_HEADS)[:, None]
    kpos = np.arange(past_len + own_width)[None, :]
    dist = past_len + t - kpos
    valid = (dist >= 0) & (kpos < past_len + t_new)
    bucket = np.where(valid, _t5_bucket_np(dist), -1).astype(np.int32)
    rbt = jnp.tile(rel_bias.T, (t_new, 1))
    return pl.pallas_call(
        _sample_bias_kernel,
        out_shape=jax.ShapeDtypeStruct(bucket.shape, F32),
        name="sample_bias_table",
    )(rbt, jnp.asarray(bucket))


def _select_by_rank(scores, n_valid, axis):
    nb = scores.shape[axis]
    idx = lax.broadcasted_iota(jnp.int32, scores.shape, axis)
    beaten = jnp.zeros(scores.shape, jnp.int32)
    for n in range(nb):
        other = lax.slice_in_dim(scores, n, n + 1, axis=axis)
        beats = (other > scores) | ((other == scores) & (n < idx))
        beaten = beaten + jnp.where(beats, jnp.where(n < n_valid, 1, 0), 0)
    return (beaten < MOBA_TOP_K) & (idx < n_valid)


def _select_by_rounds(scores, n_valid, axis):
    nb = scores.shape[axis]
    idx = lax.broadcasted_iota(jnp.int32, scores.shape, axis)
    valid = idx < n_valid
    left = jnp.where(valid, scores, -jnp.inf)
    selected = jnp.zeros(scores.shape, jnp.bool_)
    for _ in range(MOBA_TOP_K):
        best = jnp.max(left, axis=axis, keepdims=True)
        first = jnp.min(jnp.where(left == best, idx, nb), axis=axis, keepdims=True)
        pick = idx == first
        selected = selected | pick
        left = jnp.where(pick, -jnp.inf, left)
    return selected & valid


def _prompt_attn_kernel(rb_ref, q_ref, kb_ref, vtb_ref, means_ref, tab_ref, o_ref,
                        sel_ref, qm_ref, m_ref, acc_ref, s0_ref, s1_ref, x0_ref, x1_ref):
    i = pl.program_id(1)
    nb = kb_ref.shape[0]
    prev = jnp.maximum(i - 1, 0)
    pair_rows = 2 * HEAD_DIM
    qt = q_ref[...].T

    means = means_ref[...]
    row_in_pair = lax.broadcasted_iota(jnp.int32, (pair_rows, MOBA_BLOCK), 0) // HEAD_DIM
    block_id = lax.broadcasted_iota(jnp.int32, (nb, MOBA_BLOCK), 0)
    for h in range(N_HEADS):
        hs = slice(h * HEAD_DIM, (h + 1) * HEAD_DIM)
        far_bias = rb_ref[NUM_BUCKETS - 1, h] * LOG2E
        block_scores = jnp.dot(means[:, hs], qt[hs, :], precision=HIGHEST, preferred_element_type=F32)
        selected = _select_by_rounds(block_scores, i, axis=0)
        sel_ref[h * nb:(h + 1) * nb, :] = jnp.where(
            selected, jnp.where(block_id == i - 1, 0.0, far_bias), NEG_INF)
        q_pair = qt[(h // 2) * pair_rows:(h // 2 + 1) * pair_rows, :] * (HEAD_DIM ** -0.5 * LOG2E)
        qm_ref[h] = jnp.where(row_in_pair == h % 2, q_pair, 0.0).astype(BF16)

    even, odd = (s0_ref, x0_ref), (s1_ref, x1_ref)

    def scores(j, table_rows, h, buf):
        s_ref, smax_ref = buf
        pair = h // 2
        k_pair = kb_ref[j, :, pair * pair_rows:(pair + 1) * pair_rows]
        s = jnp.dot(k_pair, qm_ref[h], preferred_element_type=F32)
        if table_rows is not None:
            s = s + tab_ref[h, table_rows, :]
        s_ref[h] = s
        smax_ref[h:h + 1, :] = jnp.max(s, axis=0, keepdims=True)

    def consume(j, first, h, buf):
        s_ref, smax_ref = buf
        s = s_ref[h]
        s_max = smax_ref[h:h + 1, :]
        if first:
            m_new = s_max
            shift = m_new
        else:
            sel = sel_ref[pl.ds(h * nb + j, 1), :]
            m_old = m_ref[h:h + 1, :]
            m_new = jnp.maximum(m_old, s_max + sel)
            alpha = jnp.exp2(m_old - m_new)
            shift = m_new - sel
        m_ref[h:h + 1, :] = m_new
        p = jnp.exp2(s - shift).astype(BF16)
        pv = jnp.dot(vtb_ref[j, h], p, preferred_element_type=F32)
        acc_ref[h] = pv if first else alpha * acc_ref[h] + pv

    own_rows = slice(MOBA_BLOCK, 2 * MOBA_BLOCK)
    prev_rows = slice(0, MOBA_BLOCK)
    n_far = prev
    last_far = jnp.maximum(n_far - 1, 0)
    far = lambda j: jnp.minimum(j, last_far)
    heads = range(N_HEADS)

    def consume_then_refill(j, first, j_refill, buf):
        for h in heads:
            consume(j, first, h, buf)
            scores(far(j_refill), None, h, buf)

    for h in heads:
        scores(i, own_rows, h, even)
    for h in heads:
        scores(prev, prev_rows, h, odd)
    consume_then_refill(i, True, 0, even)
    consume_then_refill(prev, False, 1, odd)

    def far_pair(t, carry):
        consume_then_refill(2 * t, False, 2 * t + 2, even)
        consume_then_refill(2 * t + 1, False, 2 * t + 3, odd)
        return carry

    lax.fori_loop(0, n_far // 2, far_pair, 0)

    @pl.when(n_far % 2 == 1)
    def _():
        for h in heads:
            consume(n_far - 1, False, h, even)

    out_t = jnp.concatenate(
        [acc_ref[h, :HEAD_DIM, :] / acc_ref[h, HEAD_DIM:HEAD_DIM + 1, :] for h in heads], axis=0)
    o_ref[...] = out_t.T.astype(o_ref.dtype)


def _prompt_attention(rel_bias, q, kb4, vtb4, means, tables):
    b, nb = kb4.shape[:2]
    t = nb * MOBA_BLOCK
    return pl.pallas_call(
        _prompt_attn_kernel,
        grid=(b, nb),
        in_specs=[pl.BlockSpec(memory_space=pltpu.SMEM),
                  pl.BlockSpec((None, MOBA_BLOCK, D_ATTN), lambda bi, i: (bi, i, 0)),
                  pl.BlockSpec((None, nb, MOBA_BLOCK, D_ATTN), lambda bi, i: (bi, 0, 0, 0)),
                  pl.BlockSpec((None, nb, N_HEADS, V_ROWS, MOBA_BLOCK), lambda bi, i: (bi, 0, 0, 0, 0)),
                  pl.BlockSpec((None, nb, D_ATTN), lambda bi, i: (bi, 0, 0)),
                  _resident(tables.shape)],
        out_specs=pl.BlockSpec((None, MOBA_BLOCK, D_ATTN), lambda bi, i: (bi, i, 0)),
        out_shape=jax.ShapeDtypeStruct((b, t, D_ATTN), BF16),
        scratch_shapes=[pltpu.VMEM((N_HEADS * nb, MOBA_BLOCK), F32),
                        pltpu.VMEM((N_HEADS, 2 * HEAD_DIM, MOBA_BLOCK), BF16),
                        pltpu.VMEM((N_HEADS, MOBA_BLOCK), F32),
                        pltpu.VMEM((N_HEADS, V_ROWS, MOBA_BLOCK), F32),
                        pltpu.VMEM((N_HEADS, MOBA_BLOCK, MOBA_BLOCK), F32),
                        pltpu.VMEM((N_HEADS, MOBA_BLOCK, MOBA_BLOCK), F32),
                        pltpu.VMEM((N_HEADS, MOBA_BLOCK), F32),
                        pltpu.VMEM((N_HEADS, MOBA_BLOCK), F32)],
        compiler_params=_params("parallel", "arbitrary"),
        name="prompt_attention",
    )(rel_bias, q, kb4, vtb4, means, tables)


def _sample_attn_kernel(pt_ref, q_ref, kn_ref, vn_ref, bias_ref, kt_hbm, vt_hbm, o_ref,
                        kbuf_ref, vbuf_ref, knp_ref, vnp_ref, sem):
    step = pl.program_id(0)
    t_new, seqs_per_step, _ = q_ref.shape
    n_slots, n_pages, _, page = kbuf_ref.shape
    n_seq = pl.num_programs(0) * seqs_per_step
    pages_per_block = MOBA_BLOCK // page
    nb = n_pages // pages_per_block
    past_len = n_pages * page

    def page_copies(seq):
        slot = seq % n_slots
        copies = []
        for p in range(n_pages):
            phys = pt_ref[seq, p]
            copies.append(pltpu.make_async_copy(kt_hbm.at[phys], kbuf_ref.at[slot, p], sem.at[slot, 0]))
            copies.append(pltpu.make_async_copy(vt_hbm.at[phys], vbuf_ref.at[slot, p], sem.at[slot, 1]))
        return copies

    lookahead = n_slots - 1

    @pl.when(step == 0)
    def _():
        for first in range(lookahead):
            for copy in page_copies(first):
                copy.start()

    head_of_row = lax.broadcasted_iota(jnp.int32, (N_HEADS, D_ATTN), 0)
    head_of_lane = lax.broadcasted_iota(jnp.int32, (N_HEADS, D_ATTN), 1) // HEAD_DIM
    head_mask = head_of_row == head_of_lane
    lane = lax.broadcasted_iota(jnp.int32, (D_ATTN, page), 1)
    nt = (((1,), (1,)), ((), ()))

    def one_sequence(r, carry):
        seq = step * seqs_per_step + r
        slot = seq % n_slots
        row = pl.ds(r, 1)

        @pl.when(seq + lookahead < n_seq)
        def _():
            for copy in page_copies(seq + lookahead):
                copy.start()

        for copy in page_copies(seq):
            copy.wait()

        qrows = jnp.concatenate(
            [jnp.where(head_mask, jnp.broadcast_to(q_ref[t, row, :], (N_HEADS, D_ATTN)), 0.0)
             for t in range(t_new)], axis=0)

        means_t = jnp.zeros((D_ATTN, page), F32)
        for n in range(nb):
            total = sum(kbuf_ref[slot, n * pages_per_block + i] for i in range(pages_per_block))
            means_t = jnp.where(lane == n, jnp.sum(total, axis=1, keepdims=True) * (1.0 / MOBA_BLOCK), means_t)
        scores = jnp.dot(qrows, means_t, precision=HIGHEST, preferred_element_type=F32)[:, :nb]
        selected = jnp.where(_select_by_rank(scores, nb, axis=1), 1.0, 0.0)

        qb = (qrows * (HEAD_DIM ** -0.5)).astype(BF16)
        s_pages = []
        for p in range(n_pages):
            s = jnp.dot(qb, kbuf_ref[slot, p].astype(BF16), preferred_element_type=F32)
            keep = selected[:, p // pages_per_block:p // pages_per_block + 1] > 0.5
            s_pages.append(jnp.where(keep, s + bias_ref[:, p * page:(p + 1) * page], NEG_INF))
        s_past = jnp.concatenate(s_pages, axis=1)

        knp_ref[...] = jnp.zeros(knp_ref.shape, F32)
        vnp_ref[...] = jnp.zeros(vnp_ref.shape, F32)
        for t in range(t_new):
            knp_ref[t:t + 1, :] = kn_ref[t, row, :]
            vnp_ref[t:t + 1, :] = vn_ref[t, row, :]
        s_own = (lax.dot_general(qb, knp_ref[...].astype(BF16), nt, preferred_element_type=F32)
                 + bias_ref[:, past_len:])

        m = jnp.maximum(jnp.max(s_past, axis=1, keepdims=True), jnp.max(s_own, axis=1, keepdims=True))
        p_past = jnp.exp(s_past - m)
        p_own = jnp.exp(s_own - m)
        l = jnp.sum(p_past, axis=1, keepdims=True) + jnp.sum(p_own, axis=1, keepdims=True)
        o = jnp.dot(p_own.astype(BF16), vnp_ref[...].astype(BF16), preferred_element_type=F32)
        for p in range(n_pages):
            o = o + lax.dot_general(p_past[:, p * page:(p + 1) * page].astype(BF16),
                                    vbuf_ref[slot, p].astype(BF16), nt, preferred_element_type=F32)
        o = o / l
        for t in range(t_new):
            o_ref[t, row, :] = jnp.sum(jnp.where(head_mask, o[t * N_HEADS:(t + 1) * N_HEADS, :], 0.0),
                                       axis=0, keepdims=True)
        return carry

    lax.fori_loop(0, seqs_per_step, one_sequence, 0)


def _sample_attention(page_table, q, k_new, v_new, bias, cache_kt, cache_vt):
    t_new, bs, _ = q.shape
    n_pages = page_table.shape[1]
    page = cache_kt.shape[2]
    own_width = bias.shape[1] - n_pages * page
    seqs_per_step = SUBLANES
    n_slots = 3
    assert bs % seqs_per_step == 0 and bs >= n_slots
    per_step = pl.BlockSpec((t_new, seqs_per_step, D_ATTN), lambda s, pt: (0, s, 0))
    in_hbm = pl.BlockSpec(memory_space=pl.ANY)
    grid_spec = pltpu.PrefetchScalarGridSpec(
        num_scalar_prefetch=1,
        grid=(bs // seqs_per_step,),
        in_specs=[per_step, per_step, per_step, pl.BlockSpec(bias.shape, lambda s, pt: (0, 0)), in_hbm, in_hbm],
        out_specs=per_step,
        scratch_shapes=[pltpu.VMEM((n_slots, n_pages, D_ATTN, page), F32),
                        pltpu.VMEM((n_slots, n_pages, D_ATTN, page), F32),
                        pltpu.VMEM((own_width, D_ATTN), F32), pltpu.VMEM((own_width, D_ATTN), F32),
                        pltpu.SemaphoreType.DMA((n_slots, 2))],
    )
    return pl.pallas_call(
        _sample_attn_kernel,
        grid_spec=grid_spec,
        out_shape=jax.ShapeDtypeStruct((t_new, bs, D_ATTN), F32),
        compiler_params=_params("arbitrary"),
        name="sample_attention",
    )(page_table, q, k_new, v_new, bias, cache_kt, cache_vt)


def _tail_kernel(chunk, x_ref, oa_ref, orn_ref, gpre_ref, wg_ref, wpa_ref, wpr_ref, wo_ref, gpost_ref,
                 fpre_ref, wgu_ref, wd_ref, fpost_ref, y_ref):
    d_model = x_ref.shape[-1]
    d_ff = wd_ref.shape[0]
    x = x_ref[...]
    hb = _rms(x, gpre_ref[...]).astype(BF16)
    pa = jnp.dot(oa_ref[...].astype(BF16), wpa_ref[...], preferred_element_type=F32)
    merged = jax.nn.sigmoid(jnp.dot(hb, wg_ref[:, :d_model], preferred_element_type=F32)) * pa
    pr = jnp.dot(orn_ref[...].astype(BF16), wpr_ref[...], preferred_element_type=F32)
    merged = merged + jax.nn.sigmoid(jnp.dot(hb, wg_ref[:, d_model:], preferred_element_type=F32)) * pr
    x1 = x + _rms(jnp.dot(merged.astype(BF16), wo_ref[...], preferred_element_type=F32), gpost_ref[...])

    fb = _rms(x1, fpre_ref[...]).astype(BF16)
    y = jnp.zeros(x.shape, F32)
    for c in range(0, d_ff, chunk):
        gate = jnp.dot(fb, wgu_ref[:, c:c + chunk], preferred_element_type=F32)
        up = jnp.dot(fb, wgu_ref[:, d_ff + c:d_ff + c + chunk], preferred_element_type=F32)
        act = (jax.nn.silu(gate) * up).astype(BF16)
        y = y + jnp.dot(act, wd_ref[c:c + chunk, :], preferred_element_type=F32)
    y_ref[...] = x1 + _rms(y, fpost_ref[...])


def _tail(x2d, oa, orn, weights, tm, chunk):
    n, d_model = x2d.shape
    assert n % tm == 0 and weights[8].shape[0] % chunk == 0
    row = lambda width: pl.BlockSpec((tm, width), lambda i: (i, 0))
    return pl.pallas_call(
        functools.partial(_tail_kernel, chunk),
        grid=(n // tm,),
        in_specs=[row(d_model), row(D_ATTN), row(D_RNN)] + [_resident(w.shape) for w in weights],
        out_specs=row(d_model),
        out_shape=jax.ShapeDtypeStruct((n, d_model), F32),
        compiler_params=_params("parallel"),
        name="tail",
    )(x2d, oa, orn, *weights)


def _block_diag(w):
    nblk, c, _ = w.shape
    eye = jnp.eye(nblk, dtype=w.dtype)
    return (w[:, :, None, :] * eye[:, None, :, None]).reshape(nblk * c, nblk * c)


def _rows_from_feature_major(xt):
    g, _, w = xt.shape
    return jnp.transpose(xt.reshape(g, N_HEADS, HEAD_DIM, w), (3, 0, 1, 2))[None]


def kernel(x_prompt, x_sample, cache_k, cache_v, page_table, state_conv, state_h, norm_mix_pre, norm_mix_post,
           w_in, rel_bias, conv_w, conv_b, rg_w_r, rg_b_r, rg_w_i, rg_b_i, rg_lambda, w_proj_attn, w_proj_rnn,
           w_out, norm_ffn_pre, norm_ffn_post, w_gate_up, w_down):
    depth = w_in.shape[0]
    assert depth == 1
    bp, tp, d_model = x_prompt.shape
    bs, ts, _ = x_sample.shape
    n_phys, page = cache_k.shape[1:3]
    n_pages = page_table.shape[1]
    past_len = n_pages * page
    assert tp % MOBA_BLOCK == 0 and past_len % MOBA_BLOCK == 0 and MOBA_BLOCK % page == 0
    assert past_len // MOBA_BLOCK > MOBA_TOP_K and CONV_W - 1 <= ts <= MOBA_BLOCK
    nbp = tp // MOBA_BLOCK
    l = 0
    row2 = lambda v: v.reshape(1, -1)

    n_stream_cols = 3 * D_ATTN + 2 * D_RNN
    w_in_b = w_in[l, :, :n_stream_cols].astype(BF16)
    w_gates = w_in[l, :, n_stream_cols:].astype(BF16)
    rnn_w = (conv_w[l], row2(conv_b[l]), (0.5 * _block_diag(rg_w_r[l])).astype(BF16), row2(0.5 * rg_b_r[l]),
             (0.5 * _block_diag(rg_w_i[l])).astype(BF16), row2(0.5 * rg_b_i[l]), row2(rg_lambda[l]))
    wpa, wpr, wo = w_proj_attn[l].astype(BF16), w_proj_rnn[l].astype(BF16), w_out[l].astype(BF16)
    wgu, wd = w_gate_up[l].astype(BF16), w_down[l].astype(BF16)
    g_pre, g_post = row2(norm_mix_pre[l]), row2(norm_mix_post[l])
    gf_pre, gf_post = row2(norm_ffn_pre[l]), row2(norm_ffn_post[l])

    tail_w = (g_pre, w_gates, wpa, wpr, wo, g_post, gf_pre, wgu, wd, gf_post)

    tm = ROW_TILE
    xp2 = x_prompt.reshape(bp * tp, d_model)
    q, kb, vtb, means, kt, vt, orn, xtail, h_last = _prompt_in_proj(xp2, g_pre, w_in_b, rnn_w, tm, tp)
    tables = _prompt_bias_tables(rel_bias)
    oa = _prompt_attention(rel_bias, q.reshape(bp, tp, D_ATTN), kb.reshape(bp, nbp, MOBA_BLOCK, D_ATTN),
                           vtb.reshape(bp, nbp, N_HEADS, V_ROWS, MOBA_BLOCK), means.reshape(bp, nbp, D_ATTN), tables)
    y_prompt = _tail(xp2, oa.reshape(bp * tp, D_ATTN), orn, tail_w, tm, FFN_CHUNK)
    y_prompt = y_prompt.reshape(bp, tp, d_model)
    new_k_prompt = jnp.swapaxes(_rows_from_feature_major(kt), 1, 2)
    new_v_prompt = jnp.swapaxes(_rows_from_feature_major(vt), 1, 2)
    new_conv_prompt = xtail[:, SUBLANES - (CONV_W - 1):, :][None]
    new_h_prompt = h_last.reshape(1, bp, D_RNN).astype(state_h.dtype)

    ns = bs * ts
    xs2 = jnp.swapaxes(x_sample, 0, 1).reshape(ns, d_model)
    qs, ks, vs, kts, vts, xrs, ggs = _sample_in_proj(xs2, g_pre, w_in_b, bs)
    own_width = LANES
    bias_s = _sample_bias_table(rel_bias, past_len, ts, own_width)
    feature_major_pages = lambda c: jnp.transpose(c, (0, 2, 3, 1)).reshape(n_phys, D_ATTN, page)
    tm3 = lambda a: a.reshape(ts, bs, a.shape[-1])
    oas = _sample_attention(page_table, tm3(qs), tm3(ks), tm3(vs), bias_s,
                            feature_major_pages(cache_k[l]), feature_major_pages(cache_v[l]))
    xrs3 = tm3(xrs)
    orns, hs_last = _sample_rnn(xrs3, tm3(ggs), jnp.swapaxes(state_conv[l], 0, 1), state_h[l], rnn_w)
    y_sample = _tail(xs2, oas.reshape(ns, D_ATTN), orns.reshape(ns, D_RNN), tail_w, ns, FFN_CHUNK)
    y_sample = jnp.swapaxes(y_sample.reshape(ts, bs, d_model), 0, 1)
    new_k_sample = _rows_from_feature_major(kts)
    new_v_sample = _rows_from_feature_major(vts)
    new_conv_sample = jnp.swapaxes(xrs3[ts - (CONV_W - 1):], 0, 1)[None]
    new_h_sample = hs_last[None].astype(state_h.dtype)

    return (y_prompt, y_sample, new_k_prompt, new_v_prompt, new_k_sample, new_v_sample,
            new_conv_prompt, new_h_prompt, new_conv_sample, new_h_sample)
```

```python
import functools
import math

import numpy as np
import jax
import jax.numpy as jnp
from jax import lax
from jax.experimental import pallas as pl
from jax.experimental.pallas import tpu as pltpu

N_HEADS = 8
HEAD_DIM = 64
D_ATTN = N_HEADS * HEAD_DIM
D_RNN = 512
N_RG_BLOCKS = 8
CONV_W = 4
RG_C = 8.0
MOBA_BLOCK = 256
MOBA_TOP_K = 3
NUM_BUCKETS = 32
MAX_DISTANCE = 128
RMS_EPS = 1e-6
NEG_INF = -1e30
BF16_SUBLANES = 16
V_ROWS = HEAD_DIM + BF16_SUBLANES
LOG2E = math.log2(math.e)

SUBLANES = 8
LANES = 128
ROW_TILE = 512
FFN_CHUNK = 256
VMEM_LIMIT = 56 * 1024 * 1024
F32 = jnp.float32
BF16 = jnp.bfloat16
HIGHEST = lax.Precision.HIGHEST


def _params(*sem):
    return pltpu.CompilerParams(dimension_semantics=sem, vmem_limit_bytes=VMEM_LIMIT)


def _resident(shape):
    nd = len(shape)
    return pl.BlockSpec(shape, lambda *_: (0,) * nd, pipeline_mode=pl.Buffered(1))


def _rms(x, g):
    return x * lax.rsqrt(jnp.mean(x * x, axis=-1, keepdims=True) + RMS_EPS) * g


def _split_bf16(x):
    hi = x.astype(BF16)
    return hi, (x - hi.astype(F32)).astype(BF16)


def _t5_bucket_np(n):
    n = np.maximum(n, 0)
    max_exact = NUM_BUCKETS // 2
    nf = np.maximum(n, 1).astype(np.float32)
    large = max_exact + (np.log(nf / np.float32(max_exact)) / np.float32(math.log(MAX_DISTANCE / max_exact))
                         * np.float32(NUM_BUCKETS - max_exact)).astype(np.int32)
    large = np.minimum(large, NUM_BUCKETS - 1)
    return np.where(n < max_exact, n, large).astype(np.int32)


def _rglru_coeffs(xc, wr_ref, br_ref, wi_ref, bi_ref, lam_ref):
    xb = xc.astype(BF16)
    r_plus = jnp.tanh(jnp.dot(xb, wr_ref[...], preferred_element_type=F32) + br_ref[...]) + 1.0
    i_plus = jnp.tanh(jnp.dot(xb, wi_ref[...], preferred_element_type=F32) + bi_ref[...]) + 1.0
    log_a = (-0.5 * RG_C * jax.nn.softplus(-lam_ref[...])) * r_plus
    a = jnp.exp(log_a)
    z = -jnp.tanh(log_a) * (a * a + 1.0)
    b = jnp.where(z > 0.0, z * lax.rsqrt(z), 0.0) * (i_plus * (0.5 * xc))
    return a, b


def _rglru_tile(x, gg, first_tile, cw_ref, cb_ref, wr_ref, br_ref, wi_ref, bi_ref, lam_ref,
                o_ref, hl_ref, xbuf_ref, h_ref, obuf_ref):
    tt = x.shape[0]

    @pl.when(first_tile)
    def _():
        xbuf_ref[...] = jnp.zeros(xbuf_ref.shape, F32)
        h_ref[...] = jnp.zeros(h_ref.shape, F32)

    x_ext = jnp.concatenate([xbuf_ref[...], x], axis=0)
    row = lax.broadcasted_iota(jnp.int32, (tt, D_RNN), 0) % SUBLANES
    xc = cb_ref[...] + x * cw_ref[CONV_W - 1:CONV_W, :]
    for s in range(1, CONV_W):
        xc = xc + pltpu.roll(x_ext, s, axis=0)[SUBLANES:] * cw_ref[CONV_W - 1 - s:CONV_W - s, :]
    xbuf_ref[...] = x[tt - SUBLANES:]
    a, b = _rglru_coeffs(xc, wr_ref, br_ref, wi_ref, bi_ref, lam_ref)

    groups = (tt // SUBLANES, SUBLANES, D_RNN)
    a, b, row = a.reshape(groups), b.reshape(groups), row.reshape(groups)
    shift = 1
    while shift < SUBLANES:
        a_prev = pltpu.roll(a, shift, axis=1)
        b_prev = pltpu.roll(b, shift, axis=1)
        ok = row >= shift
        b = jnp.where(ok, a * b_prev + b, b)
        a = jnp.where(ok, a * a_prev, a)
        shift *= 2
    h = h_ref[...]
    for g in range(tt // SUBLANES):
        sl = slice(g * SUBLANES, (g + 1) * SUBLANES)
        hg = a[g] * h + b[g]
        obuf_ref[sl, :] = hg * gg[sl, :]
        h = hg[SUBLANES - 1:SUBLANES, :]
    o_ref[...] = obuf_ref[...].astype(o_ref.dtype)
    h_ref[...] = h
    hl_ref[...] = h


def _sample_rnn_kernel(xr_ref, gg_ref, conv_ref, h0_ref, cw_ref, cb_ref, wr_ref, br_ref, wi_ref, bi_ref, lam_ref,
                       o_ref, hl_ref):
    t_new = xr_ref.shape[0]
    past = [conv_ref[j] for j in range(CONV_W - 1)] + [xr_ref[t] for t in range(t_new)]
    h = h0_ref[...]
    for t in range(t_new):
        xc = cb_ref[...] + sum(past[t + j] * cw_ref[j:j + 1, :] for j in range(CONV_W))
        a, b = _rglru_coeffs(xc, wr_ref, br_ref, wi_ref, bi_ref, lam_ref)
        h = a * h + b
        o_ref[t] = h * gg_ref[t]
    hl_ref[...] = h


def _sample_rnn(xr_t, gg_t, conv_t, h0, weights):
    return pl.pallas_call(
        _sample_rnn_kernel,
        out_shape=[jax.ShapeDtypeStruct(xr_t.shape, F32), jax.ShapeDtypeStruct(h0.shape, F32)],
        compiler_params=pltpu.CompilerParams(vmem_limit_bytes=VMEM_LIMIT),
        name="sample_rnn",
    )(xr_t, gg_t, conv_t, h0, *weights)


def _projector(x_ref, g_ref, w_ref):
    hb = _rms(x_ref[...], g_ref[...]).astype(BF16)
    return lambda lo, width: jnp.dot(hb, w_ref[:, lo:lo + width], preferred_element_type=F32)


def _store_feature_major(kt_ref, vt_ref, k, v):
    groups, _, width = kt_ref.shape
    for r in range(groups):
        kt_ref[r] = k[r * width:(r + 1) * width, :].T
        vt_ref[r] = v[r * width:(r + 1) * width, :].T


def _sample_in_proj_kernel(x_ref, g_ref, w_ref, q_ref, k_ref, v_ref, kt_ref, vt_ref, xr_ref, gg_ref):
    proj = _projector(x_ref, g_ref, w_ref)
    q_ref[...] = proj(0, D_ATTN)
    k = proj(D_ATTN, D_ATTN)
    v = proj(2 * D_ATTN, D_ATTN)
    k_ref[...] = k
    v_ref[...] = v
    _store_feature_major(kt_ref, vt_ref, k, v)
    xr_ref[...] = proj(3 * D_ATTN, D_RNN)
    gg_ref[...] = jax.nn.gelu(proj(3 * D_ATTN + D_RNN, D_RNN))


def _prompt_in_proj_kernel(tiles_per_seq, x_ref, g_ref, w_ref, *refs):
    rnn_w = refs[:7]
    q_ref, kb_ref, vtb_ref, means_ref, kt_ref, vt_ref, orn_ref, xtail_ref, hl_ref = refs[7:16]
    xbuf_ref, h_ref, obuf_ref = refs[16:]
    proj = _projector(x_ref, g_ref, w_ref)
    xr = proj(3 * D_ATTN, D_RNN)
    gg = jax.nn.gelu(proj(3 * D_ATTN + D_RNN, D_RNN))
    xtail_ref[...] = xr[xr.shape[0] - SUBLANES:]
    _rglru_tile(xr, gg, pl.program_id(0) % tiles_per_seq == 0, *rnn_w, orn_ref, hl_ref, xbuf_ref, h_ref, obuf_ref)
    q_ref[...] = proj(0, D_ATTN)
    k = proj(D_ATTN, D_ATTN)
    v = proj(2 * D_ATTN, D_ATTN)
    _store_feature_major(kt_ref, vt_ref, k, v)
    kb_ref[...] = k.astype(BF16)
    pad_row = lax.broadcasted_iota(jnp.int32, (BF16_SUBLANES, MOBA_BLOCK), 0)
    ones_rows = jnp.where(pad_row == 0, 1.0, 0.0).astype(BF16)
    for r in range(vtb_ref.shape[0]):
        blk = slice(r * MOBA_BLOCK, (r + 1) * MOBA_BLOCK)
        v_t = v[blk, :].T.astype(BF16)
        for h in range(N_HEADS):
            vtb_ref[r, h] = jnp.concatenate([v_t[h * HEAD_DIM:(h + 1) * HEAD_DIM], ones_rows], axis=0)
        means_ref[r] = jnp.sum(k[blk, :], axis=0, keepdims=True) * (1.0 / MOBA_BLOCK)


def _feature_major_spec(n, tm, group_len):
    tiles_per_group = max(group_len // tm, 1)
    block = (max(tm // group_len, 1), D_ATTN, min(tm, group_len))
    spec = pl.BlockSpec(block, lambda i: (i // tiles_per_group, 0, i % tiles_per_group))
    return spec, jax.ShapeDtypeStruct((n // group_len, D_ATTN, group_len), F32)


def _sample_in_proj(x2d, g, w_in_b, group_len):
    n, d_model = x2d.shape
    assert n % group_len == 0
    t_spec, t_shape = _feature_major_spec(n, n, group_len)
    row = lambda width: pl.BlockSpec((n, width), lambda i: (i, 0))
    f32 = lambda width: jax.ShapeDtypeStruct((n, width), F32)
    return pl.pallas_call(
        _sample_in_proj_kernel,
        grid=(1,),
        in_specs=[row(d_model), _resident((1, d_model)), _resident(w_in_b.shape)],
        out_specs=[row(D_ATTN)] * 3 + [t_spec, t_spec, row(D_RNN), row(D_RNN)],
        out_shape=[f32(D_ATTN)] * 3 + [t_shape, t_shape, f32(D_RNN), f32(D_RNN)],
        compiler_params=_params("arbitrary"),
        name="sample_in_proj",
    )(x2d, g, w_in_b)


def _prompt_in_proj(x2d, g, w_in_b, rnn_w, tm, seq_len):
    n, d_model = x2d.shape
    assert n % seq_len == 0 and seq_len % tm == 0 and tm % MOBA_BLOCK == 0
    tiles_per_seq = seq_len // tm
    n_seq = n // seq_len
    blocks = tm // MOBA_BLOCK
    t_spec, t_shape = _feature_major_spec(n, tm, seq_len)
    row = lambda width: pl.BlockSpec((tm, width), lambda i: (i, 0))
    per_seq = lambda rows: pl.BlockSpec((None, rows, D_RNN), lambda i: (i // tiles_per_seq, 0, 0))
    return pl.pallas_call(
        functools.partial(_prompt_in_proj_kernel, tiles_per_seq),
        grid=(n // tm,),
        in_specs=[row(d_model), _resident((1, d_model)), _resident(w_in_b.shape)]
                 + [_resident(w.shape) for w in rnn_w],
        out_specs=[row(D_ATTN), row(D_ATTN),
                   pl.BlockSpec((blocks, N_HEADS, V_ROWS, MOBA_BLOCK), lambda i: (i, 0, 0, 0)),
                   pl.BlockSpec((blocks, 1, D_ATTN), lambda i: (i, 0, 0)),
                   t_spec, t_spec, row(D_RNN), per_seq(SUBLANES), per_seq(1)],
        out_shape=[jax.ShapeDtypeStruct((n, D_ATTN), F32), jax.ShapeDtypeStruct((n, D_ATTN), BF16),
                   jax.ShapeDtypeStruct((n // MOBA_BLOCK, N_HEADS, V_ROWS, MOBA_BLOCK), BF16),
                   jax.ShapeDtypeStruct((n // MOBA_BLOCK, 1, D_ATTN), F32),
                   t_shape, t_shape, jax.ShapeDtypeStruct((n, D_RNN), BF16),
                   jax.ShapeDtypeStruct((n_seq, SUBLANES, D_RNN), F32),
                   jax.ShapeDtypeStruct((n_seq, 1, D_RNN), F32)],
        scratch_shapes=[pltpu.VMEM((SUBLANES, D_RNN), F32), pltpu.VMEM((1, D_RNN), F32),
                        pltpu.VMEM((tm, D_RNN), F32)],
        compiler_params=_params("arbitrary"),
        name="prompt_in_proj",
    )(x2d, g, w_in_b, *rnn_w)


def _bias_table_kernel(rb_ref, bucket_ref, out_ref):
    bucket = bucket_ref[...]
    for h in range(N_HEADS):
        acc = jnp.full(bucket.shape, NEG_INF, F32)
        for b in range(NUM_BUCKETS):
            acc = jnp.where(bucket == b, rb_ref[b, h] * LOG2E, acc)
        out_ref[h] = acc


def _prompt_bias_tables(rel_bias):
    kr = np.arange(2 * MOBA_BLOCK)[:, None]
    qr = np.arange(MOBA_BLOCK)[None, :]
    dist = qr + MOBA_BLOCK - kr
    bucket = np.where(dist >= 0, _t5_bucket_np(dist), -1).astype(np.int32)
    return pl.pallas_call(
        _bias_table_kernel,
        in_specs=[pl.BlockSpec(memory_space=pltpu.SMEM), pl.BlockSpec(memory_space=pltpu.VMEM)],
        out_specs=pl.BlockSpec(memory_space=pltpu.VMEM),
        out_shape=jax.ShapeDtypeStruct((N_HEADS,) + bucket.shape, F32),
        name="prompt_bias_tables",
    )(rel_bias, jnp.asarray(bucket))


def _sample_bias_kernel(rbt_ref, bucket_ref, out_ref):
    bucket = bucket_ref[...]
    acc = jnp.full(bucket.shape, NEG_INF, F32)
    for b in range(NUM_BUCKETS):
        acc = jnp.where(bucket == b, rbt_ref[:, b:b + 1], acc)
    out_ref[...] = acc


def _sample_bias_table(rel_bias, past_len, t_new, own_width):
    t = np.repeat(np.arange(t_new), N_HEADS)[:, None]
    kpos = np.arange(past_len + own_width)[None, :]
    dist = past_len + t - kpos
    valid = (dist >= 0) & (kpos < past_len + t_new)
    bucket = np.where(valid, _t5_bucket_np(dist), -1).astype(np.int32)
    rbt = jnp.tile(rel_bias.T, (t_new, 1))
    return pl.pallas_call(
        _sample_bias_kernel,
        out_shape=jax.ShapeDtypeStruct(bucket.shape, F32),
        name="sample_bias_table",
    )(rbt, jnp.asarray(bucket))


def _select_by_rank(scores, n_valid, axis):
    nb = scores.shape[axis]
    idx = lax.broadcasted_iota(jnp.int32, scores.shape, axis)
    beaten = jnp.zeros(scores.shape, jnp.int32)
    for n in range(nb):
        other = lax.slice_in_dim(scores, n, n + 1, axis=axis)
        beats = (other > scores) | ((other == scores) & (n < idx))
        beaten = beaten + jnp.where(beats, jnp.where(n < n_valid, 1, 0), 0)
    return (beaten < MOBA_TOP_K) & (idx < n_valid)


def _select_by_rounds(scores, n_valid, axis):
    nb = scores.shape[axis]
    idx = lax.broadcasted_iota(jnp.int32, scores.shape, axis)
    valid = idx < n_valid
    left = jnp.where(valid, scores, -jnp.inf)
    selected = jnp.zeros(scores.shape, jnp.bool_)
    for _ in range(MOBA_TOP_K):
        best = jnp.max(left, axis=axis, keepdims=True)
        first = jnp.min(jnp.where(left == best, idx, nb), axis=axis, keepdims=True)
        pick = idx == first
        selected = selected | pick
        left = jnp.where(pick, -jnp.inf, left)
    return selected & valid


def _prompt_attn_kernel(rb_ref, q_ref, kb_ref, vtb_ref, means_ref, tab_ref, o_ref,
                        sel_ref, qm_ref, m_ref, acc_ref, s0_ref, s1_ref, x0_ref, x1_ref):
    i = pl.program_id(1)
    nb = kb_ref.shape[0]
    prev = jnp.maximum(i - 1, 0)
    pair_rows = 2 * HEAD_DIM
    qt = q_ref[...].T

    means = means_ref[...]
    means_hi, means_lo = _split_bf16(means)
    qt_hi, qt_lo = _split_bf16(qt)
    row_in_pair = lax.broadcasted_iota(jnp.int32, (pair_rows, MOBA_BLOCK), 0) // HEAD_DIM
    block_id = lax.broadcasted_iota(jnp.int32, (nb, MOBA_BLOCK), 0)
    for h in range(N_HEADS):
        hs = slice(h * HEAD_DIM, (h + 1) * HEAD_DIM)
        far_bias = rb_ref[NUM_BUCKETS - 1, h] * LOG2E
        block_scores = (jnp.dot(means_hi[:, hs], qt_hi[hs, :], preferred_element_type=F32)
                        + jnp.dot(means_hi[:, hs], qt_lo[hs, :], preferred_element_type=F32)
                        + jnp.dot(means_lo[:, hs], qt_hi[hs, :], preferred_element_type=F32))
        selected = _select_by_rounds(block_scores, i, axis=0)
        sel_ref[h * nb:(h + 1) * nb, :] = jnp.where(
            selected, jnp.where(block_id == i - 1, 0.0, far_bias), NEG_INF)
        q_pair = qt[(h // 2) * pair_rows:(h // 2 + 1) * pair_rows, :] * (HEAD_DIM ** -0.5 * LOG2E)
        qm_ref[h] = jnp.where(row_in_pair == h % 2, q_pair, 0.0).astype(BF16)

    even, odd = (s0_ref, x0_ref), (s1_ref, x1_ref)

    def scores(j, table_rows, h, buf):
        s_ref, smax_ref = buf
        pair = h // 2
        k_pair = kb_ref[j, :, pair * pair_rows:(pair + 1) * pair_rows]
        s = jnp.dot(k_pair, qm_ref[h], preferred_element_type=F32)
        if table_rows is not None:
            s = s + tab_ref[h, table_rows, :]
        s_ref[h] = s
        smax_ref[h:h + 1, :] = jnp.max(s, axis=0, keepdims=True)

    def consume(j, first, h, buf):
        s_ref, smax_ref = buf
        s = s_ref[h]
        s_max = smax_ref[h:h + 1, :]
        if first:
            m_new = s_max
            shift = m_new
        else:
            sel = sel_ref[pl.ds(h * nb + j, 1), :]
            m_old = m_ref[h:h + 1, :]
            m_new = jnp.maximum(m_old, s_max + sel)
            alpha = jnp.exp2(m_old - m_new)
            shift = m_new - sel
        m_ref[h:h + 1, :] = m_new
        p = jnp.exp2(s - shift).astype(BF16)
        pv = jnp.dot(vtb_ref[j, h], p, preferred_element_type=F32)
        acc_ref[h] = pv if first else alpha * acc_ref[h] + pv

    own_rows = slice(MOBA_BLOCK, 2 * MOBA_BLOCK)
    prev_rows = slice(0, MOBA_BLOCK)
    n_far = prev
    last_far = jnp.maximum(n_far - 1, 0)
    far = lambda j: jnp.minimum(j, last_far)
    heads = range(N_HEADS)

    def consume_then_refill(j, first, j_refill, buf):
        for h in heads:
            consume(j, first, h, buf)
            scores(far(j_refill), None, h, buf)

    for h in heads:
        scores(i, own_rows, h, even)
    for h in heads:
        scores(prev, prev_rows, h, odd)
    consume_then_refill(i, True, 0, even)
    consume_then_refill(prev, False, 1, odd)

    def far_pair(t, carry):
        consume_then_refill(2 * t, False, 2 * t + 2, even)
        consume_then_refill(2 * t + 1, False, 2 * t + 3, odd)
        return carry

    lax.fori_loop(0, n_far // 2, far_pair, 0)

    @pl.when(n_far % 2 == 1)
    def _():
        for h in heads:
            consume(n_far - 1, False, h, even)

    out_t = jnp.concatenate(
        [acc_ref[h, :HEAD_DIM, :] / acc_ref[h, HEAD_DIM:HEAD_DIM + 1, :] for h in heads], axis=0)
    o_ref[...] = out_t.T.astype(o_ref.dtype)


def _prompt_attention(rel_bias, q, kb4, vtb4, means, tables):
    b, nb = kb4.shape[:2]
    t = nb * MOBA_BLOCK
    return pl.pallas_call(
        _prompt_attn_kernel,
        grid=(b, nb),
        in_specs=[pl.BlockSpec(memory_space=pltpu.SMEM),
                  pl.BlockSpec((None, MOBA_BLOCK, D_ATTN), lambda bi, i: (bi, i, 0)),
                  pl.BlockSpec((None, nb, MOBA_BLOCK, D_ATTN), lambda bi, i: (bi, 0, 0, 0)),
                  pl.BlockSpec((None, nb, N_HEADS, V_ROWS, MOBA_BLOCK), lambda bi, i: (bi, 0, 0, 0, 0)),
                  pl.BlockSpec((None, nb, D_ATTN), lambda bi, i: (bi, 0, 0)),
                  _resident(tables.shape)],
        out_specs=pl.BlockSpec((None, MOBA_BLOCK, D_ATTN), lambda bi, i: (bi, i, 0)),
        out_shape=jax.ShapeDtypeStruct((b, t, D_ATTN), BF16),
        scratch_shapes=[pltpu.VMEM((N_HEADS * nb, MOBA_BLOCK), F32),
                        pltpu.VMEM((N_HEADS, 2 * HEAD_DIM, MOBA_BLOCK), BF16),
                        pltpu.VMEM((N_HEADS, MOBA_BLOCK), F32),
                        pltpu.VMEM((N_HEADS, V_ROWS, MOBA_BLOCK), F32),
                        pltpu.VMEM((N_HEADS, MOBA_BLOCK, MOBA_BLOCK), F32),
                        pltpu.VMEM((N_HEADS, MOBA_BLOCK, MOBA_BLOCK), F32),
                        pltpu.VMEM((N_HEADS, MOBA_BLOCK), F32),
                        pltpu.VMEM((N_HEADS, MOBA_BLOCK), F32)],
        compiler_params=_params("parallel", "arbitrary"),
        name="prompt_attention",
    )(rel_bias, q, kb4, vtb4, means, tables)


def _sample_attn_kernel(pt_ref, q_ref, kn_ref, vn_ref, bias_ref, kt_hbm, vt_hbm, o_ref,
                        kbuf_ref, vbuf_ref, knp_ref, vnp_ref, sem):
    step = pl.program_id(0)
    t_new, seqs_per_step, _ = q_ref.shape
    n_slots, _, past_len = kbuf_ref.shape
    n_pages = pt_ref.shape[1]
    page = past_len // n_pages
    n_seq = pl.num_programs(0) * seqs_per_step
    nb = past_len // MOBA_BLOCK

    def page_copies(seq):
        slot = seq % n_slots
        copies = []
        for p in range(n_pages):
            phys = pt_ref[seq, p]
            cols = pl.ds(p * page, page)
            copies.append(pltpu.make_async_copy(kt_hbm.at[phys], kbuf_ref.at[slot, :, cols], sem.at[slot, 0]))
            copies.append(pltpu.make_async_copy(vt_hbm.at[phys], vbuf_ref.at[slot, :, cols], sem.at[slot, 1]))
        return copies

    lookahead = n_slots - 1

    @pl.when(step == 0)
    def _():
        for first in range(lookahead):
            for copy in page_copies(first):
                copy.start()

    head_of_row = lax.broadcasted_iota(jnp.int32, (N_HEADS, D_ATTN), 0)
    head_of_lane = lax.broadcasted_iota(jnp.int32, (N_HEADS, D_ATTN), 1) // HEAD_DIM
    head_mask = head_of_row == head_of_lane
    lane = lax.broadcasted_iota(jnp.int32, (D_ATTN, LANES), 1)
    nt = (((1,), (1,)), ((), ()))

    def one_sequence(r, carry):
        seq = step * seqs_per_step + r
        slot = seq % n_slots
        row = pl.ds(r, 1)

        @pl.when(seq + lookahead < n_seq)
        def _():
            for copy in page_copies(seq + lookahead):
                copy.start()

        for copy in page_copies(seq):
            copy.wait()

        qrows = jnp.concatenate(
            [jnp.where(head_mask, jnp.broadcast_to(q_ref[t, row, :], (N_HEADS, D_ATTN)), 0.0)
             for t in range(t_new)], axis=0)

        means_t = jnp.zeros((D_ATTN, LANES), F32)
        for n in range(nb):
            block = kbuf_ref[slot, :, n * MOBA_BLOCK:(n + 1) * MOBA_BLOCK]
            means_t = jnp.where(lane == n, jnp.sum(block, axis=1, keepdims=True) * (1.0 / MOBA_BLOCK), means_t)
        scores = jnp.dot(qrows, means_t, precision=HIGHEST, preferred_element_type=F32)[:, :nb]
        selected = jnp.where(_select_by_rank(scores, nb, axis=1), 1.0, 0.0)

        qb = (qrows * (HEAD_DIM ** -0.5)).astype(BF16)
        s_all = jnp.dot(qb, kbuf_ref[slot].astype(BF16), preferred_element_type=F32)
        s_blocks = []
        for n in range(nb):
            cols = slice(n * MOBA_BLOCK, (n + 1) * MOBA_BLOCK)
            s_blocks.append(jnp.where(selected[:, n:n + 1] > 0.5, s_all[:, cols] + bias_ref[:, cols], NEG_INF))
        s_past = jnp.concatenate(s_blocks, axis=1)

        knp_ref[...] = jnp.zeros(knp_ref.shape, F32)
        vnp_ref[...] = jnp.zeros(vnp_ref.shape, F32)
        for t in range(t_new):
            knp_ref[t:t + 1, :] = kn_ref[t, row, :]
            vnp_ref[t:t + 1, :] = vn_ref[t, row, :]
        s_own = (lax.dot_general(qb, knp_ref[...].astype(BF16), nt, preferred_element_type=F32)
                 + bias_ref[:, past_len:])

        m = jnp.maximum(jnp.max(s_past, axis=1, keepdims=True), jnp.max(s_own, axis=1, keepdims=True))
        p_past = jnp.exp(s_past - m)
        p_own = jnp.exp(s_own - m)
        l = jnp.sum(p_past, axis=1, keepdims=True) + jnp.sum(p_own, axis=1, keepdims=True)
        o = (jnp.dot(p_own.astype(BF16), vnp_ref[...].astype(BF16), preferred_element_type=F32)
             + lax.dot_general(p_past.astype(BF16), vbuf_ref[slot].astype(BF16), nt, preferred_element_type=F32))
        o = o / l
        for t in range(t_new):
            o_ref[t, row, :] = jnp.sum(jnp.where(head_mask, o[t * N_HEADS:(t + 1) * N_HEADS, :], 0.0),
                                       axis=0, keepdims=True)
        return carry

    lax.fori_loop(0, seqs_per_step, one_sequence, 0)


def _sample_attention(page_table, q, k_new, v_new, bias, cache_kt, cache_vt):
    t_new, bs, _ = q.shape
    n_pages = page_table.shape[1]
    page = cache_kt.shape[2]
    own_width = bias.shape[1] - n_pages * page
    seqs_per_step = SUBLANES
    n_slots = 3
    assert bs % seqs_per_step == 0 and bs >= n_slots
    per_step = pl.BlockSpec((t_new, seqs_per_step, D_ATTN), lambda s, pt: (0, s, 0))
    in_hbm = pl.BlockSpec(memory_space=pl.ANY)
    grid_spec = pltpu.PrefetchScalarGridSpec(
        num_scalar_prefetch=1,
        grid=(bs // seqs_per_step,),
        in_specs=[per_step, per_step, per_step, pl.BlockSpec(bias.shape, lambda s, pt: (0, 0)), in_hbm, in_hbm],
        out_specs=per_step,
        scratch_shapes=[pltpu.VMEM((n_slots, D_ATTN, n_pages * page), F32),
                        pltpu.VMEM((n_slots, D_ATTN, n_pages * page), F32),
                        pltpu.VMEM((own_width, D_ATTN), F32), pltpu.VMEM((own_width, D_ATTN), F32),
                        pltpu.SemaphoreType.DMA((n_slots, 2))],
    )
    return pl.pallas_call(
        _sample_attn_kernel,
        grid_spec=grid_spec,
        out_shape=jax.ShapeDtypeStruct((t_new, bs, D_ATTN), F32),
        compiler_params=_params("arbitrary"),
        name="sample_attention",
    )(page_table, q, k_new, v_new, bias, cache_kt, cache_vt)


def _tail_kernel(chunk, x_ref, oa_ref, orn_ref, gpre_ref, wg_ref, wpa_ref, wpr_ref, wo_ref, gpost_ref,
                 fpre_ref, wgu_ref, wd_ref, fpost_ref, y_ref):
    d_model = x_ref.shape[-1]
    d_ff = wd_ref.shape[0]
    x = x_ref[...]
    hb = _rms(x, gpre_ref[...]).astype(BF16)
    pa = jnp.dot(oa_ref[...].astype(BF16), wpa_ref[...], preferred_element_type=F32)
    merged = jax.nn.sigmoid(jnp.dot(hb, wg_ref[:, :d_model], preferred_element_type=F32)) * pa
    pr = jnp.dot(orn_ref[...].astype(BF16), wpr_ref[...], preferred_element_type=F32)
    merged = merged + jax.nn.sigmoid(jnp.dot(hb, wg_ref[:, d_model:], preferred_element_type=F32)) * pr
    x1 = x + _rms(jnp.dot(merged.astype(BF16), wo_ref[...], preferred_element_type=F32), gpost_ref[...])

    fb = _rms(x1, fpre_ref[...]).astype(BF16)
    y = jnp.zeros(x.shape, F32)
    for c in range(0, d_ff, chunk):
        gate = jnp.dot(fb, wgu_ref[:, c:c + chunk], preferred_element_type=F32)
        up = jnp.dot(fb, wgu_ref[:, d_ff + c:d_ff + c + chunk], preferred_element_type=F32)
        act = (jax.nn.silu(gate) * up).astype(BF16)
        y = y + jnp.dot(act, wd_ref[c:c + chunk, :], preferred_element_type=F32)
    y_ref[...] = x1 + _rms(y, fpost_ref[...])


def _tail(x2d, oa, orn, weights, tm, chunk):
    n, d_model = x2d.shape
    assert n % tm == 0 and weights[8].shape[0] % chunk == 0
    row = lambda width: pl.BlockSpec((tm, width), lambda i: (i, 0))
    return pl.pallas_call(
        functools.partial(_tail_kernel, chunk),
        grid=(n // tm,),
        in_specs=[row(d_model), row(D_ATTN), row(D_RNN)] + [_resident(w.shape) for w in weights],
        out_specs=row(d_model),
        out_shape=jax.ShapeDtypeStruct((n, d_model), F32),
        compiler_params=_params("parallel"),
        name="tail",
    )(x2d, oa, orn, *weights)


def _block_diag(w):
    nblk, c, _ = w.shape
    eye = jnp.eye(nblk, dtype=w.dtype)
    return (w[:, :, None, :] * eye[:, None, :, None]).reshape(nblk * c, nblk * c)


def _rows_from_feature_major(xt):
    g, _, w = xt.shape
    return jnp.transpose(xt.reshape(g, N_HEADS, HEAD_DIM, w), (3, 0, 1, 2))[None]


def kernel(x_prompt, x_sample, cache_k, cache_v, page_table, state_conv, state_h, norm_mix_pre, norm_mix_post,
           w_in, rel_bias, conv_w, conv_b, rg_w_r, rg_b_r, rg_w_i, rg_b_i, rg_lambda, w_proj_attn, w_proj_rnn,
           w_out, norm_ffn_pre, norm_ffn_post, w_gate_up, w_down):
    depth = w_in.shape[0]
    assert depth == 1
    bp, tp, d_model = x_prompt.shape
    bs, ts, _ = x_sample.shape
    n_phys, page = cache_k.shape[1:3]
    n_pages = page_table.shape[1]
    past_len = n_pages * page
    assert tp % MOBA_BLOCK == 0 and past_len % MOBA_BLOCK == 0 and MOBA_BLOCK % page == 0
    assert past_len // MOBA_BLOCK > MOBA_TOP_K and CONV_W - 1 <= ts <= MOBA_BLOCK
    nbp = tp // MOBA_BLOCK
    l = 0
    row2 = lambda v: v.reshape(1, -1)

    n_stream_cols = 3 * D_ATTN + 2 * D_RNN
    w_in_b = w_in[l, :, :n_stream_cols].astype(BF16)
    w_gates = w_in[l, :, n_stream_cols:].astype(BF16)
    rnn_w = (conv_w[l], row2(conv_b[l]), (0.5 * _block_diag(rg_w_r[l])).astype(BF16), row2(0.5 * rg_b_r[l]),
             (0.5 * _block_diag(rg_w_i[l])).astype(BF16), row2(0.5 * rg_b_i[l]), row2(rg_lambda[l]))
    wpa, wpr, wo = w_proj_attn[l].astype(BF16), w_proj_rnn[l].astype(BF16), w_out[l].astype(BF16)
    wgu, wd = w_gate_up[l].astype(BF16), w_down[l].astype(BF16)
    g_pre, g_post = row2(norm_mix_pre[l]), row2(norm_mix_post[l])
    gf_pre, gf_post = row2(norm_ffn_pre[l]), row2(norm_ffn_post[l])

    tail_w = (g_pre, w_gates, wpa, wpr, wo, g_post, gf_pre, wgu, wd, gf_post)

    tm = ROW_TILE
    xp2 = x_prompt.reshape(bp * tp, d_model)
    q, kb, vtb, means, kt, vt, orn, xtail, h_last = _prompt_in_proj(xp2, g_pre, w_in_b, rnn_w, tm, tp)
    tables = _prompt_bias_tables(rel_bias)
    oa = _prompt_attention(rel_bias, q.reshape(bp, tp, D_ATTN), kb.reshape(bp, nbp, MOBA_BLOCK, D_ATTN),
                           vtb.reshape(bp, nbp, N_HEADS, V_ROWS, MOBA_BLOCK), means.reshape(bp, nbp, D_ATTN), tables)
    y_prompt = _tail(xp2, oa.reshape(bp * tp, D_ATTN), orn, tail_w, tm, FFN_CHUNK)
    y_prompt = y_prompt.reshape(bp, tp, d_model)
    new_k_prompt = jnp.swapaxes(_rows_from_feature_major(kt), 1, 2)
    new_v_prompt = jnp.swapaxes(_rows_from_feature_major(vt), 1, 2)
    new_conv_prompt = xtail[:, SUBLANES - (CONV_W - 1):, :][None]
    new_h_prompt = h_last.reshape(1, bp, D_RNN).astype(state_h.dtype)

    ns = bs * ts
    xs2 = jnp.swapaxes(x_sample, 0, 1).reshape(ns, d_model)
    qs, ks, vs, kts, vts, xrs, ggs = _sample_in_proj(xs2, g_pre, w_in_b, bs)
    own_width = LANES
    bias_s = _sample_bias_table(rel_bias, past_len, ts, own_width)
    feature_major_pages = lambda c: jnp.transpose(c, (0, 2, 3, 1)).reshape(n_phys, D_ATTN, page)
    tm3 = lambda a: a.reshape(ts, bs, a.shape[-1])
    oas = _sample_attention(page_table, tm3(qs), tm3(ks), tm3(vs), bias_s,
                            feature_major_pages(cache_k[l]), feature_major_pages(cache_v[l]))
    xrs3 = tm3(xrs)
    orns, hs_last = _sample_rnn(xrs3, tm3(ggs), jnp.swapaxes(state_conv[l], 0, 1), state_h[l], rnn_w)
    y_sample = _tail(xs2, oas.reshape(ns, D_ATTN), orns.reshape(ns, D_RNN), tail_w, ns, FFN_CHUNK)
    y_sample = jnp.swapaxes(y_sample.reshape(ts, bs, d_model), 0, 1)
    new_k_sample = _rows_from_feature_major(kts)
    new_v_sample = _rows_from_feature_major(vts)
    new_conv_sample = jnp.swapaxes(xrs3[ts - (CONV_W - 1):], 0, 1)[None]
    new_h_sample = hs_last[None].astype(state_h.dtype)

    return (y_prompt, y_sample, new_k_prompt, new_v_prompt, new_k_sample, new_v_sample,
            new_conv_prompt, new_h_prompt, new_conv_sample, new_h_sample)
```

```python
import functools
import math

import numpy as np
import jax
import jax.numpy as jnp
from jax import lax
from jax.experimental import pallas as pl
from jax.experimental.pallas import tpu as pltpu

N_HEADS = 8
HEAD_DIM = 64
D_ATTN = N_HEADS * HEAD_DIM
D_RNN = 512
N_RG_BLOCKS = 8
CONV_W = 4
RG_C = 8.0
MOBA_BLOCK = 256
MOBA_TOP_K = 3
NUM_BUCKETS = 32
MAX_DISTANCE = 128
RMS_EPS = 1e-6
NEG_INF = -1e30
BF16_SUBLANES = 16
V_ROWS = HEAD_DIM + BF16_SUBLANES
LOG2E = math.log2(math.e)

SUBLANES = 8
LANES = 128
ROW_TILE = 512
FFN_CHUNK = 256
VMEM_LIMIT = 56 * 1024 * 1024
F32 = jnp.float32
BF16 = jnp.bfloat16
HIGHEST = lax.Precision.HIGHEST


def _params(*sem):
    return pltpu.CompilerParams(dimension_semantics=sem, vmem_limit_bytes=VMEM_LIMIT)


def _resident(shape):
    nd = len(shape)
    return pl.BlockSpec(shape, lambda *_: (0,) * nd, pipeline_mode=pl.Buffered(1))


def _rms(x, g):
    return x * lax.rsqrt(jnp.mean(x * x, axis=-1, keepdims=True) + RMS_EPS) * g


def _split_bf16(x):
    hi = x.astype(BF16)
    return hi, (x - hi.astype(F32)).astype(BF16)


def _t5_bucket_np(n):
    n = np.maximum(n, 0)
    max_exact = NUM_BUCKETS // 2
    nf = np.maximum(n, 1).astype(np.float32)
    large = max_exact + (np.log(nf / np.float32(max_exact)) / np.float32(math.log(MAX_DISTANCE / max_exact))
                         * np.float32(NUM_BUCKETS - max_exact)).astype(np.int32)
    large = np.minimum(large, NUM_BUCKETS - 1)
    return np.where(n < max_exact, n, large).astype(np.int32)


def _rglru_coeffs(xc, wr_ref, br_ref, wi_ref, bi_ref, lam_ref):
    xb = xc.astype(BF16)
    r_plus = jnp.tanh(jnp.dot(xb, wr_ref[...], preferred_element_type=F32) + br_ref[...]) + 1.0
    i_plus = jnp.tanh(jnp.dot(xb, wi_ref[...], preferred_element_type=F32) + bi_ref[...]) + 1.0
    log_a = (-0.5 * RG_C * jax.nn.softplus(-lam_ref[...])) * r_plus
    a = jnp.exp(log_a)
    z = -jnp.tanh(log_a) * (a * a + 1.0)
    b = jnp.where(z > 0.0, z * lax.rsqrt(z), 0.0) * (i_plus * (0.5 * xc))
    return a, b


def _rglru_tile(x, gg, first_tile, cw_ref, cb_ref, wr_ref, br_ref, wi_ref, bi_ref, lam_ref,
                o_ref, hl_ref, xbuf_ref, h_ref, obuf_ref):
    tt = x.shape[0]

    @pl.when(first_tile)
    def _():
        xbuf_ref[...] = jnp.zeros(xbuf_ref.shape, F32)
        h_ref[...] = jnp.zeros(h_ref.shape, F32)

    x_ext = jnp.concatenate([xbuf_ref[...], x], axis=0)
    row = lax.broadcasted_iota(jnp.int32, (tt, D_RNN), 0) % SUBLANES
    xc = cb_ref[...] + x * cw_ref[CONV_W - 1:CONV_W, :]
    for s in range(1, CONV_W):
        xc = xc + pltpu.roll(x_ext, s, axis=0)[SUBLANES:] * cw_ref[CONV_W - 1 - s:CONV_W - s, :]
    xbuf_ref[...] = x[tt - SUBLANES:]
    a, b = _rglru_coeffs(xc, wr_ref, br_ref, wi_ref, bi_ref, lam_ref)

    groups = (tt // SUBLANES, SUBLANES, D_RNN)
    a, b, row = a.reshape(groups), b.reshape(groups), row.reshape(groups)
    shift = 1
    while shift < SUBLANES:
        a_prev = pltpu.roll(a, shift, axis=1)
        b_prev = pltpu.roll(b, shift, axis=1)
        ok = row >= shift
        b = jnp.where(ok, a * b_prev + b, b)
        a = jnp.where(ok, a * a_prev, a)
        shift *= 2
    h = h_ref[...]
    for g in range(tt // SUBLANES):
        sl = slice(g * SUBLANES, (g + 1) * SUBLANES)
        hg = a[g] * h + b[g]
        obuf_ref[sl, :] = hg * gg[sl, :]
        h = hg[SUBLANES - 1:SUBLANES, :]
    o_ref[...] = obuf_ref[...].astype(o_ref.dtype)
    h_ref[...] = h
    hl_ref[...] = h


def _sample_rnn_kernel(xr_ref, gg_ref, conv_ref, h0_ref, cw_ref, cb_ref, wr_ref, br_ref, wi_ref, bi_ref, lam_ref,
                       o_ref, hl_ref):
    t_new = xr_ref.shape[0]
    past = [conv_ref[j] for j in range(CONV_W - 1)] + [xr_ref[t] for t in range(t_new)]
    h = h0_ref[...]
    for t in range(t_new):
        xc = cb_ref[...] + sum(past[t + j] * cw_ref[j:j + 1, :] for j in range(CONV_W))
        a, b = _rglru_coeffs(xc, wr_ref, br_ref, wi_ref, bi_ref, lam_ref)
        h = a * h + b
        o_ref[t] = h * gg_ref[t]
    hl_ref[...] = h


def _sample_rnn(xr_t, gg_t, conv_t, h0, weights):
    return pl.pallas_call(
        _sample_rnn_kernel,
        out_shape=[jax.ShapeDtypeStruct(xr_t.shape, F32), jax.ShapeDtypeStruct(h0.shape, F32)],
        compiler_params=pltpu.CompilerParams(vmem_limit_bytes=VMEM_LIMIT),
        name="sample_rnn",
    )(xr_t, gg_t, conv_t, h0, *weights)


def _projector(x_ref, g_ref, w_ref):
    hb = _rms(x_ref[...], g_ref[...]).astype(BF16)
    return lambda lo, width: jnp.dot(hb, w_ref[:, lo:lo + width], preferred_element_type=F32)


def _store_feature_major(kt_ref, vt_ref, k, v):
    groups, _, width = kt_ref.shape
    for r in range(groups):
        kt_ref[r] = k[r * width:(r + 1) * width, :].T
        vt_ref[r] = v[r * width:(r + 1) * width, :].T


def _sample_in_proj_kernel(x_ref, g_ref, w_ref, q_ref, k_ref, v_ref, kt_ref, vt_ref, xr_ref, gg_ref):
    proj = _projector(x_ref, g_ref, w_ref)
    q_ref[...] = proj(0, D_ATTN)
    k = proj(D_ATTN, D_ATTN)
    v = proj(2 * D_ATTN, D_ATTN)
    k_ref[...] = k
    v_ref[...] = v
    _store_feature_major(kt_ref, vt_ref, k, v)
    xr_ref[...] = proj(3 * D_ATTN, D_RNN)
    gg_ref[...] = jax.nn.gelu(proj(3 * D_ATTN + D_RNN, D_RNN))


def _prompt_in_proj_kernel(tiles_per_seq, seqs_per_tile, n_sample_seq, pt_ref, x_ref, g_ref, w_ref, *refs):
    rnn_w = refs[:7]
    sample_in = refs[7:13]
    q_ref, kb_ref, vtb_ref, means_ref, kt_ref, vt_ref, orn_ref, xtail_ref, hl_ref, os_ref = refs[13:23]
    xbuf_ref, h_ref, obuf_ref = refs[23:26]
    sample_scratch = refs[26:]

    step = pl.program_id(0)
    tiles_per_block = os_ref.shape[1] // seqs_per_tile
    first_seq = step * seqs_per_tile
    row0 = (step % tiles_per_block) * seqs_per_tile
    seqs_before = seqs_per_tile // 2

    def sample_attention(offset, count):
        _sample_attention_steps(first_seq + offset, count, row0 + offset, n_sample_seq, pt_ref,
                                *sample_in, os_ref, *sample_scratch)

    if seqs_before:
        sample_attention(0, seqs_before)

    proj = _projector(x_ref, g_ref, w_ref)
    xr = proj(3 * D_ATTN, D_RNN)
    gg = jax.nn.gelu(proj(3 * D_ATTN + D_RNN, D_RNN))
    xtail_ref[...] = xr[xr.shape[0] - SUBLANES:]
    _rglru_tile(xr, gg, pl.program_id(0) % tiles_per_seq == 0, *rnn_w, orn_ref, hl_ref, xbuf_ref, h_ref, obuf_ref)
    q_ref[...] = proj(0, D_ATTN)
    k = proj(D_ATTN, D_ATTN)
    v = proj(2 * D_ATTN, D_ATTN)
    _store_feature_major(kt_ref, vt_ref, k, v)
    kb_ref[...] = k.astype(BF16)
    pad_row = lax.broadcasted_iota(jnp.int32, (BF16_SUBLANES, MOBA_BLOCK), 0)
    ones_rows = jnp.where(pad_row == 0, 1.0, 0.0).astype(BF16)
    for r in range(vtb_ref.shape[0]):
        blk = slice(r * MOBA_BLOCK, (r + 1) * MOBA_BLOCK)
        v_t = v[blk, :].T.astype(BF16)
        for h in range(N_HEADS):
            vtb_ref[r, h] = jnp.concatenate([v_t[h * HEAD_DIM:(h + 1) * HEAD_DIM], ones_rows], axis=0)
        means_ref[r] = jnp.sum(k[blk, :], axis=0, keepdims=True) * (1.0 / MOBA_BLOCK)

    sample_attention(seqs_before, seqs_per_tile - seqs_before)


def _feature_major_spec(n, tm, group_len):
    tiles_per_group = max(group_len // tm, 1)
    block = (max(tm // group_len, 1), D_ATTN, min(tm, group_len))
    spec = pl.BlockSpec(block, lambda i, *_: (i // tiles_per_group, 0, i % tiles_per_group))
    return spec, jax.ShapeDtypeStruct((n // group_len, D_ATTN, group_len), F32)


def _sample_in_proj(x2d, g, w_in_b, group_len):
    n, d_model = x2d.shape
    assert n % group_len == 0
    t_spec, t_shape = _feature_major_spec(n, n, group_len)
    row = lambda width: pl.BlockSpec((n, width), lambda i: (i, 0))
    f32 = lambda width: jax.ShapeDtypeStruct((n, width), F32)
    return pl.pallas_call(
        _sample_in_proj_kernel,
        grid=(1,),
        in_specs=[row(d_model), _resident((1, d_model)), _resident(w_in_b.shape)],
        out_specs=[row(D_ATTN)] * 3 + [t_spec, t_spec, row(D_RNN), row(D_RNN)],
        out_shape=[f32(D_ATTN)] * 3 + [t_shape, t_shape, f32(D_RNN), f32(D_RNN)],
        compiler_params=_params("arbitrary"),
        name="sample_in_proj",
    )(x2d, g, w_in_b)


def _prompt_in_proj(x2d, g, w_in_b, rnn_w, tm, seq_len, page_table, qs, ks_new, vs_new, bias_s, cache_kt, cache_vt):
    n, d_model = x2d.shape
    assert n % seq_len == 0 and seq_len % tm == 0 and tm % MOBA_BLOCK == 0
    tiles_per_seq = seq_len // tm
    n_tiles = n // tm
    n_seq = n // seq_len
    blocks = tm // MOBA_BLOCK
    t_new, bs, _ = qs.shape
    n_pages = page_table.shape[1]
    page = cache_kt.shape[2]
    own_width = bias_s.shape[1] - n_pages * page
    n_slots = 3
    assert bs % n_tiles == 0 and bs >= n_slots
    seqs_per_tile = bs // n_tiles
    rows_per_block = max(SUBLANES, seqs_per_tile)
    assert rows_per_block % seqs_per_tile == 0 and bs % rows_per_block == 0
    tiles_per_block = rows_per_block // seqs_per_tile
    t_spec, t_shape = _feature_major_spec(n, tm, seq_len)
    row = lambda width: pl.BlockSpec((tm, width), lambda i, pt: (i, 0))
    per_seq = lambda rows: pl.BlockSpec((None, rows, D_RNN), lambda i, pt: (i // tiles_per_seq, 0, 0))
    sample_rows = pl.BlockSpec((t_new, rows_per_block, D_ATTN), lambda i, pt: (0, i // tiles_per_block, 0))
    in_hbm = pl.BlockSpec(memory_space=pl.ANY)
    grid_spec = pltpu.PrefetchScalarGridSpec(
        num_scalar_prefetch=1,
        grid=(n_tiles,),
        in_specs=[row(d_model), _resident((1, d_model)), _resident(w_in_b.shape)]
                 + [_resident(w.shape) for w in rnn_w]
                 + [sample_rows, sample_rows, sample_rows, _resident(bias_s.shape), in_hbm, in_hbm],
        out_specs=[row(D_ATTN), row(D_ATTN),
                   pl.BlockSpec((blocks, N_HEADS, V_ROWS, MOBA_BLOCK), lambda i, pt: (i, 0, 0, 0)),
                   pl.BlockSpec((blocks, 1, D_ATTN), lambda i, pt: (i, 0, 0)),
                   t_spec, t_spec, row(D_RNN), per_seq(SUBLANES), per_seq(1), sample_rows],
        scratch_shapes=[pltpu.VMEM((SUBLANES, D_RNN), F32), pltpu.VMEM((1, D_RNN), F32),
                        pltpu.VMEM((tm, D_RNN), F32),
                        pltpu.VMEM((n_slots, D_ATTN, n_pages * page), F32),
                        pltpu.VMEM((n_slots, D_ATTN, n_pages * page), F32),
                        pltpu.VMEM((own_width, D_ATTN), F32), pltpu.VMEM((own_width, D_ATTN), F32),
                        pltpu.SemaphoreType.DMA((n_slots, 2))],
    )
    return pl.pallas_call(
        functools.partial(_prompt_in_proj_kernel, tiles_per_seq, seqs_per_tile, bs),
        grid_spec=grid_spec,
        out_shape=[jax.ShapeDtypeStruct((n, D_ATTN), F32), jax.ShapeDtypeStruct((n, D_ATTN), BF16),
                   jax.ShapeDtypeStruct((n // MOBA_BLOCK, N_HEADS, V_ROWS, MOBA_BLOCK), BF16),
                   jax.ShapeDtypeStruct((n // MOBA_BLOCK, 1, D_ATTN), F32),
                   t_shape, t_shape, jax.ShapeDtypeStruct((n, D_RNN), BF16),
                   jax.ShapeDtypeStruct((n_seq, SUBLANES, D_RNN), F32),
                   jax.ShapeDtypeStruct((n_seq, 1, D_RNN), F32),
                   jax.ShapeDtypeStruct((t_new, bs, D_ATTN), F32)],
        compiler_params=_params("arbitrary"),
        name="prompt_in_proj",
    )(page_table, x2d, g, w_in_b, *rnn_w, qs, ks_new, vs_new, bias_s, cache_kt, cache_vt)


def _bias_table_kernel(rb_ref, bucket_ref, out_ref):
    bucket = bucket_ref[...]
    for h in range(N_HEADS):
        acc = jnp.full(bucket.shape, NEG_INF, F32)
        for b in range(NUM_BUCKETS):
            acc = jnp.where(bucket == b, rb_ref[b, h] * LOG2E, acc)
        out_ref[h] = acc


def _prompt_bias_tables(rel_bias):
    kr = np.arange(2 * MOBA_BLOCK)[:, None]
    qr = np.arange(MOBA_BLOCK)[None, :]
    dist = qr + MOBA_BLOCK - kr
    bucket = np.where(dist >= 0, _t5_bucket_np(dist), -1).astype(np.int32)
    return pl.pallas_call(
        _bias_table_kernel,
        in_specs=[pl.BlockSpec(memory_space=pltpu.SMEM), pl.BlockSpec(memory_space=pltpu.VMEM)],
        out_specs=pl.BlockSpec(memory_space=pltpu.VMEM),
        out_shape=jax.ShapeDtypeStruct((N_HEADS,) + bucket.shape, F32),
        name="prompt_bias_tables",
    )(rel_bias, jnp.asarray(bucket))


def _sample_bias_kernel(rbt_ref, bucket_ref, out_ref):
    bucket = bucket_ref[...]
    acc = jnp.full(bucket.shape, NEG_INF, F32)
    for b in range(NUM_BUCKETS):
        acc = jnp.where(bucket == b, rbt_ref[:, b:b + 1], acc)
    out_ref[...] = acc


def _sample_bias_table(rel_bias, past_len, t_new, own_width):
    t = np.repeat(np.arange(t_new), N_HEADS)[:, None]
    kpos = np.arange(past_len + own_width)[None, :]
    dist = past_len + t - kpos
    valid = (dist >= 0) & (kpos < past_len + t_new)
    bucket = np.where(valid, _t5_bucket_np(dist), -1).astype(np.int32)
    rbt = jnp.tile(rel_bias.T, (t_new, 1))
    return pl.pallas_call(
        _sample_bias_kernel,
        out_shape=jax.ShapeDtypeStruct(bucket.shape, F32),
        name="sample_bias_table",
    )(rbt, jnp.asarray(bucket))


def _select_by_rank(scores, n_valid, axis):
    nb = scores.shape[axis]
    idx = lax.broadcasted_iota(jnp.int32, scores.shape, axis)
    beaten = jnp.zeros(scores.shape, jnp.int32)
    for n in range(nb):
        other = lax.slice_in_dim(scores, n, n + 1, axis=axis)
        beats = (other > scores) | ((other == scores) & (n < idx))
        beaten = beaten + jnp.where(beats, jnp.where(n < n_valid, 1, 0), 0)
    return (beaten < MOBA_TOP_K) & (idx < n_valid)


def _select_by_rounds(scores, n_valid, axis):
    nb = scores.shape[axis]
    idx = lax.broadcasted_iota(jnp.int32, scores.shape, axis)
    valid = idx < n_valid
    left = jnp.where(valid, scores, -jnp.inf)
    selected = jnp.zeros(scores.shape, jnp.bool_)
    for _ in range(MOBA_TOP_K):
        best = jnp.max(left, axis=axis, keepdims=True)
        first = jnp.min(jnp.where(left == best, idx, nb), axis=axis, keepdims=True)
        pick = idx == first
        selected = selected | pick
        left = jnp.where(pick, -jnp.inf, left)
    return selected & valid


def _prompt_attn_kernel(rb_ref, q_ref, kb_ref, vtb_ref, means_ref, tab_ref, o_ref,
                        sel_ref, qm_ref, m_ref, acc_ref, s0_ref, s1_ref, x0_ref, x1_ref):
    i = pl.program_id(1)
    nb = kb_ref.shape[0]
    prev = jnp.maximum(i - 1, 0)
    pair_rows = 2 * HEAD_DIM
    qt = q_ref[...].T

    means = means_ref[...]
    means_hi, means_lo = _split_bf16(means)
    qt_hi, qt_lo = _split_bf16(qt)
    row_in_pair = lax.broadcasted_iota(jnp.int32, (pair_rows, MOBA_BLOCK), 0) // HEAD_DIM
    block_id = lax.broadcasted_iota(jnp.int32, (nb, MOBA_BLOCK), 0)
    for h in range(N_HEADS):
        hs = slice(h * HEAD_DIM, (h + 1) * HEAD_DIM)
        far_bias = rb_ref[NUM_BUCKETS - 1, h] * LOG2E
        block_scores = (jnp.dot(means_hi[:, hs], qt_hi[hs, :], preferred_element_type=F32)
                        + jnp.dot(means_hi[:, hs], qt_lo[hs, :], preferred_element_type=F32)
                        + jnp.dot(means_lo[:, hs], qt_hi[hs, :], preferred_element_type=F32))
        selected = _select_by_rounds(block_scores, i, axis=0)
        sel_ref[h * nb:(h + 1) * nb, :] = jnp.where(
            selected, jnp.where(block_id == i - 1, 0.0, far_bias), NEG_INF)
        q_pair = qt[(h // 2) * pair_rows:(h // 2 + 1) * pair_rows, :] * (HEAD_DIM ** -0.5 * LOG2E)
        qm_ref[h] = jnp.where(row_in_pair == h % 2, q_pair, 0.0).astype(BF16)

    even, odd = (s0_ref, x0_ref), (s1_ref, x1_ref)

    def scores(j, table_rows, h, buf):
        s_ref, smax_ref = buf
        pair = h // 2
        k_pair = kb_ref[j, :, pair * pair_rows:(pair + 1) * pair_rows]
        s = jnp.dot(k_pair, qm_ref[h], preferred_element_type=F32)
        if table_rows is not None:
            s = s + tab_ref[h, table_rows, :]
        s_ref[h] = s
        smax_ref[h:h + 1, :] = jnp.max(s, axis=0, keepdims=True)

    def consume(j, first, h, buf):
        s_ref, smax_ref = buf
        s = s_ref[h]
        s_max = smax_ref[h:h + 1, :]
        if first:
            m_new = s_max
            shift = m_new
        else:
            sel = sel_ref[pl.ds(h * nb + j, 1), :]
            m_old = m_ref[h:h + 1, :]
            m_new = jnp.maximum(m_old, s_max + sel)
            alpha = jnp.exp2(m_old - m_new)
            shift = m_new - sel
        m_ref[h:h + 1, :] = m_new
        p = jnp.exp2(s - shift).astype(BF16)
        pv = jnp.dot(vtb_ref[j, h], p, preferred_element_type=F32)
        acc_ref[h] = pv if first else alpha * acc_ref[h] + pv

    own_rows = slice(MOBA_BLOCK, 2 * MOBA_BLOCK)
    prev_rows = slice(0, MOBA_BLOCK)
    n_far = prev
    last_far = jnp.maximum(n_far - 1, 0)
    far = lambda j: jnp.minimum(j, last_far)
    heads = range(N_HEADS)

    def consume_then_refill(j, first, j_refill, buf):
        for h in heads:
            consume(j, first, h, buf)
            scores(far(j_refill), None, h, buf)

    for h in heads:
        scores(i, own_rows, h, even)
    for h in heads:
        scores(prev, prev_rows, h, odd)
    consume_then_refill(i, True, 0, even)
    consume_then_refill(prev, False, 1, odd)

    def far_pair(t, carry):
        consume_then_refill(2 * t, False, 2 * t + 2, even)
        consume_then_refill(2 * t + 1, False, 2 * t + 3, odd)
        return carry

    lax.fori_loop(0, n_far // 2, far_pair, 0)

    @pl.when(n_far % 2 == 1)
    def _():
        for h in heads:
            consume(n_far - 1, False, h, even)

    out_t = jnp.concatenate(
        [acc_ref[h, :HEAD_DIM, :] / acc_ref[h, HEAD_DIM:HEAD_DIM + 1, :] for h in heads], axis=0)
    o_ref[...] = out_t.T.astype(o_ref.dtype)


def _prompt_attention(rel_bias, q, kb4, vtb4, means, tables):
    b, nb = kb4.shape[:2]
    t = nb * MOBA_BLOCK
    return pl.pallas_call(
        _prompt_attn_kernel,
        grid=(b, nb),
        in_specs=[pl.BlockSpec(memory_space=pltpu.SMEM),
                  pl.BlockSpec((None, MOBA_BLOCK, D_ATTN), lambda bi, i: (bi, i, 0)),
                  pl.BlockSpec((None, nb, MOBA_BLOCK, D_ATTN), lambda bi, i: (bi, 0, 0, 0)),
                  pl.BlockSpec((None, nb, N_HEADS, V_ROWS, MOBA_BLOCK), lambda bi, i: (bi, 0, 0, 0, 0)),
                  pl.BlockSpec((None, nb, D_ATTN), lambda bi, i: (bi, 0, 0)),
                  _resident(tables.shape)],
        out_specs=pl.BlockSpec((None, MOBA_BLOCK, D_ATTN), lambda bi, i: (bi, i, 0)),
        out_shape=jax.ShapeDtypeStruct((b, t, D_ATTN), BF16),
        scratch_shapes=[pltpu.VMEM((N_HEADS * nb, MOBA_BLOCK), F32),
                        pltpu.VMEM((N_HEADS, 2 * HEAD_DIM, MOBA_BLOCK), BF16),
                        pltpu.VMEM((N_HEADS, MOBA_BLOCK), F32),
                        pltpu.VMEM((N_HEADS, V_ROWS, MOBA_BLOCK), F32),
                        pltpu.VMEM((N_HEADS, MOBA_BLOCK, MOBA_BLOCK), F32),
                        pltpu.VMEM((N_HEADS, MOBA_BLOCK, MOBA_BLOCK), F32),
                        pltpu.VMEM((N_HEADS, MOBA_BLOCK), F32),
                        pltpu.VMEM((N_HEADS, MOBA_BLOCK), F32)],
        compiler_params=_params("parallel", "arbitrary"),
        name="prompt_attention",
    )(rel_bias, q, kb4, vtb4, means, tables)


def _sample_attention_steps(first_seq, seqs_here, row0, n_seq, pt_ref, q_ref, kn_ref, vn_ref, bias_ref, kt_hbm, vt_hbm,
                            o_ref, kbuf_ref, vbuf_ref, knp_ref, vnp_ref, sem):
    t_new = q_ref.shape[0]
    n_slots, _, past_len = kbuf_ref.shape
    n_pages = pt_ref.shape[1]
    page = past_len // n_pages
    nb = past_len // MOBA_BLOCK

    def page_copies(seq):
        slot = seq % n_slots
        copies = []
        for p in range(n_pages):
            phys = pt_ref[seq, p]
            cols = pl.ds(p * page, page)
            copies.append(pltpu.make_async_copy(kt_hbm.at[phys], kbuf_ref.at[slot, :, cols], sem.at[slot, 0]))
            copies.append(pltpu.make_async_copy(vt_hbm.at[phys], vbuf_ref.at[slot, :, cols], sem.at[slot, 1]))
        return copies

    lookahead = n_slots - 1

    @pl.when(first_seq == 0)
    def _():
        for first in range(lookahead):
            for copy in page_copies(first):
                copy.start()

    head_of_row = lax.broadcasted_iota(jnp.int32, (N_HEADS, D_ATTN), 0)
    head_of_lane = lax.broadcasted_iota(jnp.int32, (N_HEADS, D_ATTN), 1) // HEAD_DIM
    head_mask = head_of_row == head_of_lane
    lane = lax.broadcasted_iota(jnp.int32, (D_ATTN, LANES), 1)
    nt = (((1,), (1,)), ((), ()))

    def one_sequence(r, carry):
        seq = first_seq + r
        slot = seq % n_slots
        row = pl.ds(row0 + r, 1)

        @pl.when(seq + lookahead < n_seq)
        def _():
            for copy in page_copies(seq + lookahead):
                copy.start()

        for copy in page_copies(seq):
            copy.wait()

        qrows = jnp.concatenate(
            [jnp.where(head_mask, jnp.broadcast_to(q_ref[t, row, :], (N_HEADS, D_ATTN)), 0.0)
             for t in range(t_new)], axis=0)

        means_t = jnp.zeros((D_ATTN, LANES), F32)
        for n in range(nb):
            block = kbuf_ref[slot, :, n * MOBA_BLOCK:(n + 1) * MOBA_BLOCK]
            means_t = jnp.where(lane == n, jnp.sum(block, axis=1, keepdims=True) * (1.0 / MOBA_BLOCK), means_t)
        scores = jnp.dot(qrows, means_t, precision=HIGHEST, preferred_element_type=F32)[:, :nb]
        selected = jnp.where(_select_by_rank(scores, nb, axis=1), 1.0, 0.0)

        qb = (qrows * (HEAD_DIM ** -0.5)).astype(BF16)
        s_all = jnp.dot(qb, kbuf_ref[slot].astype(BF16), preferred_element_type=F32)
        s_blocks = []
        for n in range(nb):
            cols = slice(n * MOBA_BLOCK, (n + 1) * MOBA_BLOCK)
            s_blocks.append(jnp.where(selected[:, n:n + 1] > 0.5, s_all[:, cols] + bias_ref[:, cols], NEG_INF))
        s_past = jnp.concatenate(s_blocks, axis=1)

        knp_ref[...] = jnp.zeros(knp_ref.shape, F32)
        vnp_ref[...] = jnp.zeros(vnp_ref.shape, F32)
        for t in range(t_new):
            knp_ref[t:t + 1, :] = kn_ref[t, row, :]
            vnp_ref[t:t + 1, :] = vn_ref[t, row, :]
        s_own = (lax.dot_general(qb, knp_ref[...].astype(BF16), nt, preferred_element_type=F32)
                 + bias_ref[:, past_len:])

        m = jnp.maximum(jnp.max(s_past, axis=1, keepdims=True), jnp.max(s_own, axis=1, keepdims=True))
        p_past = jnp.exp(s_past - m)
        p_own = jnp.exp(s_own - m)
        l = jnp.sum(p_past, axis=1, keepdims=True) + jnp.sum(p_own, axis=1, keepdims=True)
        o = (jnp.dot(p_own.astype(BF16), vnp_ref[...].astype(BF16), preferred_element_type=F32)
             + lax.dot_general(p_past.astype(BF16), vbuf_ref[slot].astype(BF16), nt, preferred_element_type=F32))
        o = o / l
        for t in range(t_new):
            o_ref[t, row, :] = jnp.sum(jnp.where(head_mask, o[t * N_HEADS:(t + 1) * N_HEADS, :], 0.0),
                                       axis=0, keepdims=True)
        return carry

    lax.fori_loop(0, seqs_here, one_sequence, 0)


def _tail_kernel(chunk, x_ref, oa_ref, orn_ref, gpre_ref, wg_ref, wpa_ref, wpr_ref, wo_ref, gpost_ref,
                 fpre_ref, wgu_ref, wd_ref, fpost_ref, y_ref):
    d_model = x_ref.shape[-1]
    d_ff = wd_ref.shape[0]
    x = x_ref[...]
    hb = _rms(x, gpre_ref[...]).astype(BF16)
    pa = jnp.dot(oa_ref[...].astype(BF16), wpa_ref[...], preferred_element_type=F32)
    merged = jax.nn.sigmoid(jnp.dot(hb, wg_ref[:, :d_model], preferred_element_type=F32)) * pa
    pr = jnp.dot(orn_ref[...].astype(BF16), wpr_ref[...], preferred_element_type=F32)
    merged = merged + jax.nn.sigmoid(jnp.dot(hb, wg_ref[:, d_model:], preferred_element_type=F32)) * pr
    x1 = x + _rms(jnp.dot(merged.astype(BF16), wo_ref[...], preferred_element_type=F32), gpost_ref[...])

    fb = _rms(x1, fpre_ref[...]).astype(BF16)
    y = jnp.zeros(x.shape, F32)
    for c in range(0, d_ff, chunk):
        gate = jnp.dot(fb, wgu_ref[:, c:c + chunk], preferred_element_type=F32)
        up = jnp.dot(fb, wgu_ref[:, d_ff + c:d_ff + c + chunk], preferred_element_type=F32)
        act = (jax.nn.silu(gate) * up).astype(BF16)
        y = y + jnp.dot(act, wd_ref[c:c + chunk, :], preferred_element_type=F32)
    y_ref[...] = x1 + _rms(y, fpost_ref[...])


def _tail(x2d, oa, orn, weights, tm, chunk):
    n, d_model = x2d.shape
    assert n % tm == 0 and weights[8].shape[0] % chunk == 0
    row = lambda width: pl.BlockSpec((tm, width), lambda i: (i, 0))
    return pl.pallas_call(
        functools.partial(_tail_kernel, chunk),
        grid=(n // tm,),
        in_specs=[row(d_model), row(D_ATTN), row(D_RNN)] + [_resident(w.shape) for w in weights],
        out_specs=row(d_model),
        out_shape=jax.ShapeDtypeStruct((n, d_model), F32),
        compiler_params=_params("parallel"),
        name="tail",
    )(x2d, oa, orn, *weights)


def _block_diag(w):
    nblk, c, _ = w.shape
    eye = jnp.eye(nblk, dtype=w.dtype)
    return (w[:, :, None, :] * eye[:, None, :, None]).reshape(nblk * c, nblk * c)


def _rows_from_feature_major(xt):
    g, _, w = xt.shape
    return jnp.transpose(xt.reshape(g, N_HEADS, HEAD_DIM, w), (3, 0, 1, 2))[None]


def kernel(x_prompt, x_sample, cache_k, cache_v, page_table, state_conv, state_h, norm_mix_pre, norm_mix_post,
           w_in, rel_bias, conv_w, conv_b, rg_w_r, rg_b_r, rg_w_i, rg_b_i, rg_lambda, w_proj_attn, w_proj_rnn,
           w_out, norm_ffn_pre, norm_ffn_post, w_gate_up, w_down):
    depth = w_in.shape[0]
    assert depth == 1
    bp, tp, d_model = x_prompt.shape
    bs, ts, _ = x_sample.shape
    n_phys, page = cache_k.shape[1:3]
    n_pages = page_table.shape[1]
    past_len = n_pages * page
    assert tp % MOBA_BLOCK == 0 and past_len % MOBA_BLOCK == 0 and MOBA_BLOCK % page == 0
    assert past_len // MOBA_BLOCK > MOBA_TOP_K and CONV_W - 1 <= ts <= MOBA_BLOCK
    nbp = tp // MOBA_BLOCK
    l = 0
    row2 = lambda v: v.reshape(1, -1)

    n_stream_cols = 3 * D_ATTN + 2 * D_RNN
    w_in_b = w_in[l, :, :n_stream_cols].astype(BF16)
    w_gates = w_in[l, :, n_stream_cols:].astype(BF16)
    rnn_w = (conv_w[l], row2(conv_b[l]), (0.5 * _block_diag(rg_w_r[l])).astype(BF16), row2(0.5 * rg_b_r[l]),
             (0.5 * _block_diag(rg_w_i[l])).astype(BF16), row2(0.5 * rg_b_i[l]), row2(rg_lambda[l]))
    wpa, wpr, wo = w_proj_attn[l].astype(BF16), w_proj_rnn[l].astype(BF16), w_out[l].astype(BF16)
    wgu, wd = w_gate_up[l].astype(BF16), w_down[l].astype(BF16)
    g_pre, g_post = row2(norm_mix_pre[l]), row2(norm_mix_post[l])
    gf_pre, gf_post = row2(norm_ffn_pre[l]), row2(norm_ffn_post[l])

    tail_w = (g_pre, w_gates, wpa, wpr, wo, g_post, gf_pre, wgu, wd, gf_post)

    ns = bs * ts
    xs2 = jnp.swapaxes(x_sample, 0, 1).reshape(ns, d_model)
    qs, ks, vs, kts, vts, xrs, ggs = _sample_in_proj(xs2, g_pre, w_in_b, bs)
    own_width = LANES
    bias_s = _sample_bias_table(rel_bias, past_len, ts, own_width)
    feature_major_pages = lambda c: jnp.transpose(c, (0, 2, 3, 1)).reshape(n_phys, D_ATTN, page)
    tm3 = lambda a: a.reshape(ts, bs, a.shape[-1])

    tm = ROW_TILE
    xp2 = x_prompt.reshape(bp * tp, d_model)
    q, kb, vtb, means, kt, vt, orn, xtail, h_last, oas = _prompt_in_proj(
        xp2, g_pre, w_in_b, rnn_w, tm, tp, page_table, tm3(qs), tm3(ks), tm3(vs), bias_s,
        feature_major_pages(cache_k[l]), feature_major_pages(cache_v[l]))
    tables = _prompt_bias_tables(rel_bias)
    oa = _prompt_attention(rel_bias, q.reshape(bp, tp, D_ATTN), kb.reshape(bp, nbp, MOBA_BLOCK, D_ATTN),
                           vtb.reshape(bp, nbp, N_HEADS, V_ROWS, MOBA_BLOCK), means.reshape(bp, nbp, D_ATTN), tables)
    y_prompt = _tail(xp2, oa.reshape(bp * tp, D_ATTN), orn, tail_w, tm, FFN_CHUNK)
    y_prompt = y_prompt.reshape(bp, tp, d_model)
    new_k_prompt = jnp.swapaxes(_rows_from_feature_major(kt), 1, 2)
    new_v_prompt = jnp.swapaxes(_rows_from_feature_major(vt), 1, 2)
    new_conv_prompt = xtail[:, SUBLANES - (CONV_W - 1):, :][None]
    new_h_prompt = h_last.reshape(1, bp, D_RNN).astype(state_h.dtype)

    xrs3 = tm3(xrs)
    orns, hs_last = _sample_rnn(xrs3, tm3(ggs), jnp.swapaxes(state_conv[l], 0, 1), state_h[l], rnn_w)
    y_sample = _tail(xs2, oas.reshape(ns, D_ATTN), orns.reshape(ns, D_RNN), tail_w, ns, FFN_CHUNK)
    y_sample = jnp.swapaxes(y_sample.reshape(ts, bs, d_model), 0, 1)
    new_k_sample = _rows_from_feature_major(kts)
    new_v_sample = _rows_from_feature_major(vts)
    new_conv_sample = jnp.swapaxes(xrs3[ts - (CONV_W - 1):], 0, 1)[None]
    new_h_sample = hs_last[None].astype(state_h.dtype)

    return (y_prompt, y_sample, new_k_prompt, new_v_prompt, new_k_sample, new_v_sample,
            new_conv_prompt, new_h_prompt, new_conv_sample, new_h_sample)
```
